```python
import jax
import jax.numpy as jnp
from jax import lax
import numpy as np

D_MODEL = 2048
BATCH = 4
SEQ = 4096
DEPTH = 1
DEC_BATCH = 16
DEC_SEQ = 16
PAST_LEN = 4096

CHUNK = 64
Q_BLOCK = 128
EPS = 1e-6

MLA_HEADS = D_MODEL // 128
Q_LORA = 512
KV_LORA = 512
NOPE_DIM = 128
ROPE_DIM = 64
V_DIM = 128
ROPE_BASE = 10000.0
ATTN_SCALE = (NOPE_DIM + ROPE_DIM) ** -0.5

GDN_HEADS = D_MODEL // 128
GDN_DK = 128
GDN_DV = 128
CONV_W = 4
GDN_QKV = GDN_HEADS * (2 * GDN_DK + GDN_DV)

N_GROUPS = 8
EXPERTS_PER_GROUP = 8
N_EXPERTS = N_GROUPS * EXPERTS_PER_GROUP
TOP_K = 2
D_EXPERT = 512
MOE_BLOCK = 128

IN_SPLITS = (Q_LORA, KV_LORA, ROPE_DIM, GDN_QKV, GDN_HEADS, GDN_HEADS, GDN_HEADS * GDN_DV, D_MODEL, D_MODEL)
IN_TOTAL = Q_LORA + KV_LORA + ROPE_DIM + GDN_QKV + 2 * GDN_HEADS + GDN_HEADS * GDN_DV + 2 * D_MODEL

kernel_name = 'hybrid_mla_gdn_hmoe_stream_step'


def rmsnorm(x, g):
    xf = x.astype(jnp.float32)
    y = xf * lax.rsqrt(jnp.mean(xf * xf, axis=-1, keepdims=True) + EPS)
    return (y * g.astype(jnp.float32)).astype(x.dtype)


def l2norm(x):
    xf = x.astype(jnp.float32)
    return xf * lax.rsqrt(jnp.sum(xf * xf, axis=-1, keepdims=True) + EPS)


def rope(x, pos):
    half = ROPE_DIM // 2
    inv = ROPE_BASE ** (-jnp.arange(half, dtype=jnp.float32) / half)
    ang = pos.astype(jnp.float32)[:, None] * inv[None, :]
    shp = (ang.shape[0],) + (1,) * (x.ndim - 3) + (half,)
    cos = jnp.cos(ang).reshape(shp)
    sin = jnp.sin(ang).reshape(shp)
    x1 = x[..., :half].astype(jnp.float32)
    x2 = x[..., half:].astype(jnp.float32)
    return jnp.concatenate([x1 * cos - x2 * sin, x2 * cos + x1 * sin], axis=-1).astype(x.dtype)


def split_in(u):
    cuts, acc = [], 0
    for size in IN_SPLITS[:-1]:
        acc += size
        cuts.append(acc)
    return jnp.split(u, cuts, axis=-1)


def mla_project(q_lat, kv_lat, k_r, pos, q_norm_g, w_uq, kv_norm_g):
    q = jnp.einsum('bsc,chd->bshd', rmsnorm(q_lat, q_norm_g), w_uq)
    q_nope = q[..., :NOPE_DIM]
    q_rope = rope(q[..., NOPE_DIM:], pos)
    ckv = rmsnorm(kv_lat, kv_norm_g)
    kr = rope(k_r, pos)
    return q_nope, q_rope, ckv, kr


def mla_prompt(q_nope, q_rope, ckv, kr, pos, w_uk, w_uv):
    b, s = ckv.shape[:2]
    k_nope = jnp.einsum('bkc,chd->bkhd', ckv, w_uk)
    v = jnp.einsum('bkc,chd->bkhd', ckv, w_uv)
    key_chunk = pos // CHUNK
    nb = s // Q_BLOCK

    def blocks(t):
        return jnp.moveaxis(t.reshape((b, nb, Q_BLOCK) + t.shape[2:]), 1, 0)

    def attend(args):
        qn, qr, qc = args
        sc = (jnp.einsum('bqhd,bkhd->bhqk', qn, k_nope, preferred_element_type=jnp.float32)
              + jnp.einsum('bqhr,bkr->bhqk', qr, kr, preferred_element_type=jnp.float32)) * ATTN_SCALE
        sc = jnp.where(key_chunk[None, :] <= (qc // CHUNK)[:, None], sc, -jnp.inf)
        p = jax.nn.softmax(sc, axis=-1).astype(v.dtype)
        return jnp.einsum('bhqk,bkhd->bqhd', p, v)

    o = lax.map(attend, (blocks(q_nope), blocks(q_rope), pos.reshape(nb, Q_BLOCK)))
    return jnp.moveaxis(o, 0, 1).reshape(b, s, MLA_HEADS * V_DIM)


def mla_sample(q_nope, q_rope, ckv, kr, pos, ckv_past, kr_past, w_uk, w_uv):
    b, s = ckv.shape[:2]
    past = ckv_past.shape[1]
    q_abs = jnp.einsum('bqhd,chd->bqhc', q_nope, w_uk)
    s_past = (jnp.einsum('bqhc,bkc->bhqk', q_abs, ckv_past, preferred_element_type=jnp.float32)
              + jnp.einsum('bqhr,bkr->bhqk', q_rope, kr_past, preferred_element_type=jnp.float32)) * ATTN_SCALE
    s_new = (jnp.einsum('bqhc,bkc->bhqk', q_abs, ckv, preferred_element_type=jnp.float32)
             + jnp.einsum('bqhr,bkr->bhqk', q_rope, kr, preferred_element_type=jnp.float32)) * ATTN_SCALE
    qc = pos // CHUNK
    s_new = jnp.where(qc[None, :] <= qc[:, None], s_new, -jnp.inf)
    p = jax.nn.softmax(jnp.concatenate([s_past, s_new], axis=-1), axis=-1).astype(ckv.dtype)
    o_lat = (jnp.einsum('bhqk,bkc->bqhc', p[..., :past], ckv_past.astype(ckv.dtype))
             + jnp.einsum('bhqk,bkc->bqhc', p[..., past:], ckv))
    return jnp.einsum('bqhc,chd->bqhd', o_lat, w_uv).reshape(b, s, MLA_HEADS * V_DIM)


def causal_conv(x, conv0, conv_w):
    s = x.shape[1]
    xp = jnp.concatenate([conv0.astype(x.dtype), x], axis=1)
    y = xp[:, 0:s] * conv_w[0]
    for j in range(1, CONV_W):
        y = y + xp[:, j:j + s] * conv_w[j]
    return jax.nn.silu(y), xp[:, s:]


def gdn_chunk(S, q, k, v, g, beta):
    L = q.shape[2]
    G = jnp.cumsum(g, axis=-1)
    tri = jnp.tril(jnp.ones((L, L), dtype=bool))
    strict = jnp.tril(jnp.ones((L, L), dtype=bool), -1)
    diff = G[..., :, None] - G[..., None, :]
    decay = jnp.where(tri, jnp.exp(jnp.where(tri, diff, 0.0)), 0.0)
    kb = k * beta[..., None]
    lmat = jnp.where(strict, jnp.einsum('bhid,bhjd->bhij', kb, k) * decay, 0.0)
    eye = jnp.eye(L, dtype=jnp.float32)
    T = lax.linalg.triangular_solve(eye + lmat, jnp.broadcast_to(eye, lmat.shape),
                                    left_side=True, lower=True, unit_diagonal=True)
    eG = jnp.exp(G)[..., None]
    u = T @ (v * beta[..., None])
    w = T @ (kb * eG)
    v_new = u - w @ S
    attn = jnp.where(tri, jnp.einsum('bhid,bhjd->bhij', q, k) * decay, 0.0)
    o = (q * eG) @ S + attn @ v_new
    gL = G[..., -1]
    S_new = S * jnp.exp(gL)[..., None, None] + jnp.einsum(
        'bhld,bhle->bhde', k * jnp.exp(gL[..., None] - G)[..., None], v_new)
    return S_new, o


def gdn_branch(qkv, a, bb, z, S0, conv0, conv_w, a_log, dt_bias, gdn_norm_g):
    b, s, _ = qkv.shape
    qkv_c, conv_new = causal_conv(qkv, conv0, conv_w)
    q, k, v = jnp.split(qkv_c, [GDN_HEADS * GDN_DK, 2 * GDN_HEADS * GDN_DK], axis=-1)
    q = l2norm(q.reshape(b, s, GDN_HEADS, GDN_DK)) * (GDN_DK ** -0.5)
    k = l2norm(k.reshape(b, s, GDN_HEADS, GDN_DK))
    v = v.reshape(b, s, GDN_HEADS, GDN_DV).astype(jnp.float32)
    g = -jnp.exp(a_log.astype(jnp.float32)) * jax.nn.softplus(a.astype(jnp.float32) + dt_bias.astype(jnp.float32))
    beta = jax.nn.sigmoid(bb.astype(jnp.float32))
    L = min(s, CHUNK)
    n = s // L

    def to_chunks(t):
        t = t.reshape((b, n, L) + t.shape[2:])
        return jnp.moveaxis(t, (1, 3), (0, 2))

    S_new, o = lax.scan(lambda S, inp: gdn_chunk(S, *inp), S0.astype(jnp.float32),
                        (to_chunks(q), to_chunks(k), to_chunks(v), to_chunks(g), to_chunks(beta)))
    o = jnp.moveaxis(o, (0, 2), (1, 3)).reshape(b, s, GDN_HEADS, GDN_DV)
    zf = z.reshape(b, s, GDN_HEADS, GDN_DV).astype(jnp.float32)
    o = rmsnorm(o, gdn_norm_g) * jax.nn.silu(zf)
    return o.reshape(b, s, GDN_HEADS * GDN_DV).astype(qkv.dtype), S_new, conv_new


def route(xt, w_group, b_group, w_router, b_router):
    xf = xt.astype(jnp.float32)
    pg = jax.nn.softmax(xf @ w_group.astype(jnp.float32) + b_group.astype(jnp.float32), axis=-1)
    pg_top, grp = lax.top_k(pg, 1)
    le = jnp.einsum('nd,gde->nge', xf, w_router.astype(jnp.float32)) + b_router.astype(jnp.float32)
    le_sel = jnp.take_along_axis(le, grp[:, :, None], axis=1)[:, 0]
    top_v, top_i = lax.top_k(le_sel, TOP_K)
    weight = pg_top * jax.nn.softmax(top_v, axis=-1)
    expert = grp * EXPERTS_PER_GROUP + top_i
    return expert, weight


def moe_ffn(h, w_group, b_group, w_router, b_router, w_gate, w_up, w_down):
    b, s, d = h.shape
    xt = h.reshape(-1, d)
    n = xt.shape[0]
    expert, weight = route(xt, w_group, b_group, w_router, b_router)
    a = n * TOP_K
    e_flat = expert.reshape(-1)
    w_flat = weight.reshape(-1)
    tok = jnp.repeat(jnp.arange(n, dtype=jnp.int32), TOP_K)
    order = jnp.argsort(e_flat)
    e_s, tok_s, w_s = e_flat[order], tok[order], w_flat[order]
    counts = jnp.zeros((N_EXPERTS,), jnp.int32).at[e_flat].add(1)
    start = jnp.cumsum(counts) - counts
    padded = (counts + MOE_BLOCK - 1) // MOE_BLOCK * MOE_BLOCK
    pend = jnp.cumsum(padded)
    pstart = pend - padded
    dest = pstart[e_s] + (jnp.arange(a, dtype=jnp.int32) - start[e_s])
    n_blocks = (a + N_EXPERTS * (MOE_BLOCK - 1) + MOE_BLOCK - 1) // MOE_BLOCK
    n_slots = n_blocks * MOE_BLOCK
    slot_tok = jnp.full((n_slots,), n, jnp.int32).at[dest].set(tok_s)
    slot_w = jnp.zeros((n_slots,), h.dtype).at[dest].set(w_s.astype(h.dtype))
    block_e = jnp.minimum(jnp.searchsorted(pend, jnp.arange(n_blocks, dtype=jnp.int32) * MOE_BLOCK, side='right'),
                          N_EXPERTS - 1)
    x_pad = jnp.concatenate([xt, jnp.zeros((1, d), xt.dtype)], axis=0)

    def run_block(args):
        tok_b, w_b, e = args
        xb = x_pad[tok_b]
        hid = jax.nn.silu(xb @ w_gate[e]) * (xb @ w_up[e])
        return (hid @ w_down[e]) * w_b[:, None]

    out = lax.map(run_block, (slot_tok.reshape(n_blocks, MOE_BLOCK), slot_w.reshape(n_blocks, MOE_BLOCK), block_e))
    y = jnp.zeros((n + 1, d), h.dtype).at[slot_tok].add(out.reshape(n_slots, d))
    return y[:n].reshape(b, s, d)


def layer(x, pos, ckv_past, kr_past, S0, conv0, lw):
    (attn_norm_g, w_in, q_norm_g, w_uq, kv_norm_g, w_uk, w_uv, conv_w, a_log, dt_bias, gdn_norm_g, w_out,
     ffn_norm_g, w_group, b_group, w_router, b_router, w_gate, w_up, w_down) = lw
    h = rmsnorm(x, attn_norm_g)
    u = h @ w_in
    q_lat, kv_lat, k_r, qkv, a, bb, z, gate_a, gate_b = split_in(u)
    q_nope, q_rope, ckv, kr = mla_project(q_lat, kv_lat, k_r, pos, q_norm_g, w_uq, kv_norm_g)
    if ckv_past is None:
        o_a = mla_prompt(q_nope, q_rope, ckv, kr, pos, w_uk, w_uv)
    else:
        o_a = mla_sample(q_nope, q_rope, ckv, kr, pos, ckv_past, kr_past, w_uk, w_uv)
    o_b, S_new, conv_new = gdn_branch(qkv, a, bb, z, S0, conv0, conv_w, a_log, dt_bias, gdn_norm_g)
    merged = jax.nn.sigmoid(gate_a) * o_a + jax.nn.sigmoid(gate_b) * o_b
    x = x + merged @ w_out
    x = x + moe_ffn(rmsnorm(x, ffn_norm_g), w_group, b_group, w_router, b_router, w_gate, w_up, w_down)
    return x, ckv, kr, S_new, conv_new


def setup_inputs(seed: int = 0) -> dict:
    key = jax.random.key(seed)
    ks = jax.random.split(key, 32)
    f32 = jnp.float32
    L = DEPTH

    def nrm(k, shape, scale):
        return jax.random.normal(k, shape, f32) * scale

    def gain(k, shape):
        return 1.0 + 0.01 * jax.random.normal(k, shape, f32)

    dt = jnp.exp(jax.random.uniform(ks[14], (L, GDN_HEADS), f32, np.log(1e-3), np.log(1e-1)))
    return {
        'x_prompt': nrm(ks[0], (BATCH, SEQ, D_MODEL), 1.0),
        'x_sample': nrm(ks[1], (DEC_BATCH, DEC_SEQ, D_MODEL), 1.0),
        'cache_ckv': nrm(ks[2], (L, DEC_BATCH, PAST_LEN, KV_LORA), 1.0),
        'cache_k_rope': nrm(ks[3], (L, DEC_BATCH, PAST_LEN, ROPE_DIM), 1.0),
        'state_gdn': nrm(ks[4], (L, DEC_BATCH, GDN_HEADS, GDN_DK, GDN_DV), 0.1),
        'state_conv': nrm(ks[5], (L, DEC_BATCH, CONV_W - 1, GDN_QKV), 1.0),
        'attn_norm_g': gain(ks[6], (L, D_MODEL)),
        'w_in': nrm(ks[7], (L, D_MODEL, IN_TOTAL), D_MODEL ** -0.5),
        'q_norm_g': gain(ks[8], (L, Q_LORA)),
        'w_uq': nrm(ks[9], (L, Q_LORA, MLA_HEADS, NOPE_DIM + ROPE_DIM), Q_LORA ** -0.5),
        'kv_norm_g': gain(ks[10], (L, KV_LORA)),
        'w_uk': nrm(ks[11], (L, KV_LORA, MLA_HEADS, NOPE_DIM), KV_LORA ** -0.5),
        'w_uv': nrm(ks[12], (L, KV_LORA, MLA_HEADS, V_DIM), KV_LORA ** -0.5),
        'conv_w': nrm(ks[13], (L, CONV_W, GDN_QKV), CONV_W ** -0.5),
        'a_log': jnp.log(jax.random.uniform(ks[15], (L, GDN_HEADS), f32, 1.0, 16.0)),
        'dt_bias': dt + jnp.log(-jnp.expm1(-dt)),
        'gdn_norm_g': gain(ks[16], (L, GDN_DV)),
        'w_out': nrm(ks[17], (L, D_MODEL, D_MODEL), D_MODEL ** -0.5),
        'ffn_norm_g': gain(ks[18], (L, D_MODEL)),
        'w_group': nrm(ks[19], (L, D_MODEL, N_GROUPS), D_MODEL ** -0.5),
        'b_group': nrm(ks[20], (L, N_GROUPS), 0.01),
        'w_router': nrm(ks[21], (L, N_GROUPS, D_MODEL, EXPERTS_PER_GROUP), D_MODEL ** -0.5),
        'b_router': nrm(ks[22], (L, N_GROUPS, EXPERTS_PER_GROUP), 0.01),
        'w_gate': nrm(ks[23], (L, N_EXPERTS, D_MODEL, D_EXPERT), D_MODEL ** -0.5),
        'w_up': nrm(ks[24], (L, N_EXPERTS, D_MODEL, D_EXPERT), D_MODEL ** -0.5),
        'w_down': nrm(ks[25], (L, N_EXPERTS, D_EXPERT, D_MODEL), D_EXPERT ** -0.5),
        'final_norm_g': gain(ks[26], (D_MODEL,)),
    }


def reference(x_prompt, x_sample, cache_ckv, cache_k_rope, state_gdn, state_conv,
              attn_norm_g, w_in, q_norm_g, w_uq, kv_norm_g, w_uk, w_uv, conv_w, a_log, dt_bias, gdn_norm_g, w_out,
              ffn_norm_g, w_group, b_group, w_router, b_router, w_gate, w_up, w_down, final_norm_g):
    b_p = x_prompt.shape[0]
    pos_p = jnp.arange(x_prompt.shape[1], dtype=jnp.int32)
    pos_s = PAST_LEN + jnp.arange(x_sample.shape[1], dtype=jnp.int32)
    xp, xs = x_prompt, x_sample
    ckv_p, kr_p, sg_p, sc_p = [], [], [], []
    ckv_s, kr_s, sg_s, sc_s = [], [], [], []
    for l in range(DEPTH):
        lw = (attn_norm_g[l], w_in[l], q_norm_g[l], w_uq[l], kv_norm_g[l], w_uk[l], w_uv[l], conv_w[l], a_log[l],
              dt_bias[l], gdn_norm_g[l], w_out[l], ffn_norm_g[l], w_group[l], b_group[l], w_router[l], b_router[l],
              w_gate[l], w_up[l], w_down[l])
        s0 = jnp.zeros((b_p, GDN_HEADS, GDN_DK, GDN_DV), jnp.float32)
        c0 = jnp.zeros((b_p, CONV_W - 1, GDN_QKV), x_prompt.dtype)
        xp, ckv, kr, sg, sc = layer(xp, pos_p, None, None, s0, c0, lw)
        ckv_p.append(ckv)
        kr_p.append(kr)
        sg_p.append(sg.astype(x_prompt.dtype))
        sc_p.append(sc)
        xs, ckv, kr, sg, sc = layer(xs, pos_s, cache_ckv[l], cache_k_rope[l], state_gdn[l], state_conv[l], lw)
        ckv_s.append(ckv)
        kr_s.append(kr)
        sg_s.append(sg.astype(x_sample.dtype))
        sc_s.append(sc)
    y_prompt = rmsnorm(xp, final_norm_g)
    y_sample = rmsnorm(xs, final_norm_g)
    new_ckv_prompt = jnp.stack(ckv_p)
    new_k_rope_prompt = jnp.stack(kr_p)
    new_gdn_prompt = jnp.stack(sg_p)
    new_conv_prompt = jnp.stack(sc_p)
    new_ckv_sample = jnp.stack(ckv_s)
    new_k_rope_sample = jnp.stack(kr_s)
    new_gdn_sample = jnp.stack(sg_s)
    new_conv_sample = jnp.stack(sc_s)
    return (y_prompt, y_sample, new_ckv_prompt, new_k_rope_prompt, new_gdn_prompt, new_conv_prompt,
            new_ckv_sample, new_k_rope_sample, new_gdn_sample, new_conv_sample)
```

```python
import functools

import numpy as np
import jax
import jax.numpy as jnp
from jax import lax
from jax.experimental import pallas as pl
from jax.experimental.pallas import tpu as pltpu

F32 = jnp.float32
BF16 = jnp.bfloat16

EPS = 1e-6
CHUNK = 64
ROPE_BASE = 10000.0
HEAD_DIM = 128
ROPE_DIM = 64
CONV_W = 4
TOP_K = 2
N_GROUPS = 8
EXPERTS_PER_GROUP = 8

LANES = 128
SUBLANES = 8
VMEM_LIMIT = 56 * 1024 * 1024


def _cparams(sem):
    return pltpu.CompilerParams(dimension_semantics=sem, vmem_limit_bytes=VMEM_LIMIT)


def _dot(a, b):
    return jnp.dot(a, b, preferred_element_type=F32)


def _dot_nt(a, b):
    return lax.dot_general(a, b, (((1,), (1,)), ((), ())), preferred_element_type=F32)


def _dot_tn(a, b):
    return lax.dot_general(a, b, (((0,), (0,)), ((), ())), preferred_element_type=F32)


def _sigmoid(x):
    return 1.0 / (1.0 + jnp.exp(-x))


def _split_bf16(a):
    hi = a.astype(BF16)
    lo = (a - hi.astype(F32)).astype(BF16)
    return hi, lo


def _dot3(a_hi, a_lo, b_hi, b_lo):
    return _dot(a_hi, b_hi) + (_dot(a_hi, b_lo) + _dot(a_lo, b_hi))


def _rms_mm_body(x_ref, g_ref, w_ref, o_ref, h_ref):
    @pl.when(pl.program_id(1) == 0)
    def _():
        x = x_ref[...]
        ms = jnp.mean(x * x, axis=-1, keepdims=True)
        h_ref[...] = ((x * lax.rsqrt(ms + EPS)) * g_ref[...]).astype(BF16)

    o_ref[...] = _dot(h_ref[...], w_ref[...])


def rms_matmul(x, g, w, tm, tn):
    n, k = x.shape
    m = w.shape[1]
    return pl.pallas_call(
        _rms_mm_body,
        grid=(n // tm, m // tn),
        in_specs=[pl.BlockSpec((tm, k), lambda i, j: (i, 0)),
                  pl.BlockSpec((1, k), lambda i, j: (0, 0)),
                  pl.BlockSpec((k, tn), lambda i, j: (0, j))],
        out_specs=pl.BlockSpec((tm, tn), lambda i, j: (i, j)),
        out_shape=jax.ShapeDtypeStruct((n, m), F32),
        scratch_shapes=[pltpu.VMEM((tm, k), BF16)],
        compiler_params=_cparams(("parallel", "arbitrary")),
        name="rms_matmul",
    )(x, g.reshape(1, k), w)


def _mla_prep_body(ql_ref, kvl_ref, sm_ref, cs_ref, gq_ref, gkv_ref, wq_ref, wkv_ref,
                   ckv_ref, kr_ref, q_ref, k_ref, v_ref, *, n_heads):
    hd = HEAD_DIM
    ql = ql_ref[...]
    qn = ((ql * lax.rsqrt(jnp.mean(ql * ql, axis=-1, keepdims=True) + EPS)) * gq_ref[...]).astype(BF16)
    q_all = _dot(qn, wq_ref[...])
    kvl = kvl_ref[...]
    ckv = (kvl * lax.rsqrt(jnp.mean(kvl * kvl, axis=-1, keepdims=True) + EPS)) * gkv_ref[...]
    ckv_ref[...] = ckv
    kv_all = _dot(ckv.astype(BF16), wkv_ref[...])

    cs = cs_ref[...]
    lane = lax.broadcasted_iota(jnp.int32, cs.shape, 1)

    def rope(t):
        t = t * cs
        return jnp.where(lane < ROPE_DIM, t + pltpu.roll(t, ROPE_DIM, axis=1), 0.0)

    kr = rope(sm_ref[:, 0:LANES])
    kr_ref[...] = kr
    kr_b = kr.astype(BF16)
    off = n_heads * hd
    for h in range(n_heads):
        q_ref[0, h, :, 0:hd] = q_all[:, h * hd:(h + 1) * hd].astype(BF16)
        q_ref[0, h, :, hd:2 * hd] = rope(q_all[:, off + h * hd:off + (h + 1) * hd]).astype(BF16)
        k_ref[0, h, :, 0:hd] = kv_all[:, h * hd:(h + 1) * hd].astype(BF16)
        k_ref[0, h, :, hd:2 * hd] = kr_b
        v_ref[0, h] = kv_all[:, off + h * hd:off + (h + 1) * hd].astype(BF16)


def mla_prep(u, cs, gq, gkv, wq, wkv, b, s, tm, col_ql, col_kvl, col_sm, n_heads):
    lora = wq.shape[0]
    nt = s // tm
    row = lambda bi, si: bi * nt + si
    hd = HEAD_DIM
    return pl.pallas_call(
        functools.partial(_mla_prep_body, n_heads=n_heads),
        grid=(b, nt),
        in_specs=[pl.BlockSpec((tm, lora), lambda bi, si: (row(bi, si), col_ql)),
                  pl.BlockSpec((tm, lora), lambda bi, si: (row(bi, si), col_kvl)),
                  pl.BlockSpec((tm, 512), lambda bi, si: (row(bi, si), col_sm)),
                  pl.BlockSpec((tm, LANES), lambda bi, si: (row(bi, si), 0)),
                  pl.BlockSpec((1, lora), lambda bi, si: (0, 0)),
                  pl.BlockSpec((1, lora), lambda bi, si: (0, 0)),
                  pl.BlockSpec(wq.shape, lambda bi, si: (0, 0)),
                  pl.BlockSpec(wkv.shape, lambda bi, si: (0, 0))],
        out_specs=[pl.BlockSpec((tm, lora), lambda bi, si: (row(bi, si), 0)),
                   pl.BlockSpec((tm, LANES), lambda bi, si: (row(bi, si), 0)),
                   pl.BlockSpec((1, n_heads, tm, 2 * hd), lambda bi, si: (bi, 0, si, 0)),
                   pl.BlockSpec((1, n_heads, tm, 2 * hd), lambda bi, si: (bi, 0, si, 0)),
                   pl.BlockSpec((1, n_heads, tm, hd), lambda bi, si: (bi, 0, si, 0))],
        out_shape=[jax.ShapeDtypeStruct((b * s, lora), F32),
                   jax.ShapeDtypeStruct((b * s, LANES), F32),
                   jax.ShapeDtypeStruct((b, n_heads, s, 2 * hd), BF16),
                   jax.ShapeDtypeStruct((b, n_heads, s, 2 * hd), BF16),
                   jax.ShapeDtypeStruct((b, n_heads, s, hd), BF16)],
        compiler_params=_cparams(("parallel", "parallel")),
        name="mla_prep",
    )(u, u, u, cs, gq.reshape(1, lora), gkv.reshape(1, lora), wq, wkv)


def _flash_body(qi_tab, ki_tab, fl_tab, q_ref, k_ref, v_ref, o_ref, m_scr, l_scr, acc_scr, *, scale, tq, tk):
    st = pl.program_id(2)
    qi = qi_tab[st]
    ki = ki_tab[st]
    fl = fl_tab[st]

    @pl.when((fl & 1) != 0)
    def _():
        m_scr[...] = jnp.full(m_scr.shape, -jnp.inf, F32)
        l_scr[...] = jnp.zeros(l_scr.shape, F32)
        acc_scr[...] = jnp.zeros(acc_scr.shape, F32)

    sc = _dot_nt(q_ref[0, 0], k_ref[0, 0]) * scale

    def update(sc):
        m_prev = m_scr[...]
        m_new = jnp.maximum(m_prev, jnp.max(sc, axis=-1, keepdims=True))
        alpha = jnp.exp(m_prev - m_new)
        p = jnp.exp(sc - m_new)
        l_scr[...] = alpha * l_scr[...] + jnp.sum(p, axis=-1, keepdims=True)
        acc_scr[...] = alpha * acc_scr[...] + _dot(p.astype(BF16), v_ref[0, 0])
        m_scr[...] = m_new

    @pl.when((fl & 4) != 0)
    def _():
        qpos = qi * tq + lax.broadcasted_iota(jnp.int32, sc.shape, 0)
        kpos = ki * tk + lax.broadcasted_iota(jnp.int32, sc.shape, 1)
        update(jnp.where(kpos // CHUNK <= qpos // CHUNK, sc, -jnp.inf))

    @pl.when((fl & 4) == 0)
    def _():
        update(sc)

    @pl.when((fl & 2) != 0)
    def _():
        o_ref[0] = acc_scr[...] / l_scr[...]


def _flash_tables(s, tq, tk):
    qi_l, ki_l, fl_l = [], [], []
    for qi in range(s // tq):
        q_hi = (qi + 1) * tq
        vis_end = -(-q_hi // CHUNK) * CHUNK
        nk = -(-min(vis_end, s) // tk)
        for ki in range(nk):
            masked = (ki + 1) * tk > (qi * tq // CHUNK + 1) * CHUNK
            qi_l.append(qi)
            ki_l.append(ki)
            fl_l.append((1 if ki == 0 else 0) | (2 if ki == nk - 1 else 0) | (4 if masked else 0))
    return (jnp.asarray(qi_l, jnp.int32), jnp.asarray(ki_l, jnp.int32), jnp.asarray(fl_l, jnp.int32))


def flash_prompt(q, k, v, tq, tk, scale):
    b, nh, s, dk = q.shape
    hd = v.shape[-1]
    qi_tab, ki_tab, fl_tab = _flash_tables(s, tq, tk)
    nst = qi_tab.shape[0]
    grid_spec = pltpu.PrefetchScalarGridSpec(
        num_scalar_prefetch=3,
        grid=(b, nh, nst),
        in_specs=[pl.BlockSpec((1, 1, tq, dk), lambda bi, h, st, qt, kt, ft: (bi, h, qt[st], 0)),
                  pl.BlockSpec((1, 1, tk, dk), lambda bi, h, st, qt, kt, ft: (bi, h, kt[st], 0)),
                  pl.BlockSpec((1, 1, tk, hd), lambda bi, h, st, qt, kt, ft: (bi, h, kt[st], 0))],
        out_specs=pl.BlockSpec((1, tq, hd), lambda bi, h, st, qt, kt, ft: (bi, qt[st], h)),
        scratch_shapes=[pltpu.VMEM((tq, 1), F32), pltpu.VMEM((tq, 1), F32), pltpu.VMEM((tq, hd), F32)],
    )
    return pl.pallas_call(
        functools.partial(_flash_body, scale=scale, tq=tq, tk=tk),
        grid_spec=grid_spec,
        out_shape=jax.ShapeDtypeStruct((b, s, nh * hd), F32),
        compiler_params=_cparams(("parallel", "parallel", "arbitrary")),
        name="flash_prompt",
    )(qi_tab, ki_tab, fl_tab, q, k, v)


def _bmm_body(a_ref, b_ref, o_ref):
    o_ref[0] = _dot(a_ref[0].astype(BF16), b_ref[0])


def head_matmul(a, b):
    nh, m, k = a.shape
    n = b.shape[2]
    return pl.pallas_call(
        _bmm_body,
        grid=(nh,),
        in_specs=[pl.BlockSpec((1, m, k), lambda h: (h, 0, 0)),
                  pl.BlockSpec((1, k, n), lambda h: (h, 0, 0))],
        out_specs=pl.BlockSpec((1, m, n), lambda h: (h, 0, 0)),
        out_shape=jax.ShapeDtypeStruct((nh, m, n), F32),
        compiler_params=_cparams(("parallel",)),
        name="head_matmul",
    )(a, b)


def _attn_sample_body(qa_ref, qr_ref, kp_ref, krp_ref, kn_ref, krn_ref, o_ref, *, scale, past, s_new, n_heads):
    qa = qa_ref[0].astype(BF16)
    qr = qr_ref[0].astype(BF16)
    kp = kp_ref[0].astype(BF16)
    krp = krp_ref[0].astype(BF16)
    kn = kn_ref[0].astype(BF16)
    krn = krn_ref[0].astype(BF16)
    s_past = (_dot_nt(qa, kp) + _dot_nt(qr, krp)) * scale
    s_n = (_dot_nt(qa, kn) + _dot_nt(qr, krn)) * scale
    row = lax.broadcasted_iota(jnp.int32, s_n.shape, 0)
    col = lax.broadcasted_iota(jnp.int32, s_n.shape, 1)
    qpos = past + row // n_heads
    kpos = past + col
    valid = (col < s_new) & (kpos // CHUNK <= qpos // CHUNK)
    s_n = jnp.where(valid, s_n, -jnp.inf)
    m = jnp.maximum(jnp.max(s_past, axis=-1, keepdims=True), jnp.max(s_n, axis=-1, keepdims=True))
    pp = jnp.exp(s_past - m)
    pn = jnp.exp(s_n - m)
    l = jnp.sum(pp, axis=-1, keepdims=True) + jnp.sum(pn, axis=-1, keepdims=True)
    o_ref[0] = (_dot(pp.astype(BF16), kp) + _dot(pn.astype(BF16), kn)) / l


def attn_sample(qa, qr, ckv_past, kr_past, ckv_new, kr_new, scale, s_new, n_heads):
    b, r, lora = qa.shape
    past = ckv_past.shape[1]
    return pl.pallas_call(
        functools.partial(_attn_sample_body, scale=scale, past=past, s_new=s_new, n_heads=n_heads),
        grid=(b,),
        in_specs=[pl.BlockSpec((1, r, lora), lambda i: (i, 0, 0)),
                  pl.BlockSpec((1, r, LANES), lambda i: (i, 0, 0)),
                  pl.BlockSpec((1, past, lora), lambda i: (i, 0, 0)),
                  pl.BlockSpec((1, past, LANES), lambda i: (i, 0, 0)),
                  pl.BlockSpec((1, LANES, lora), lambda i: (i, 0, 0)),
                  pl.BlockSpec((1, LANES, LANES), lambda i: (i, 0, 0))],
        out_specs=pl.BlockSpec((1, r, lora), lambda i: (i, 0, 0)),
        out_shape=jax.ShapeDtypeStruct((b, r, lora), F32),
        compiler_params=_cparams(("parallel",)),
        name="attn_sample",
    )(qa, qr, ckv_past, kr_past, ckv_new, kr_new)


def _inv_unit_lower(n, size):
    row = lax.broadcasted_iota(jnp.int32, n.shape, 0)
    col = lax.broadcasted_iota(jnp.int32, n.shape, 1)
    t = jnp.where(row == col, 1.0, 0.0) - n
    n_hi, n_lo = _split_bf16(n)
    p = _dot3(n_hi, n_lo, n_hi, n_lo)
    power = 2
    while True:
        p_hi, p_lo = _split_bf16(p)
        t_hi, t_lo = _split_bf16(t)
        t = t + _dot3(t_hi, t_lo, p_hi, p_lo)
        power *= 2
        if power >= size:
            return t
        p = _dot3(p_hi, p_lo, p_hi, p_lo)


def _softplus(x):
    return jnp.maximum(x, 0.0) + jnp.log1p(jnp.exp(-jnp.abs(x)))


def _gdn_body(qkv_ref, z_ref, sm_ref, prev0_ref, s0_ref, convw_ref, alog_ref, dtb_ref, gn_ref,
              o_ref, sout_ref, s_scr, prev_scr, y_scr, *, L, n_heads):
    hd = HEAD_DIM
    c = pl.program_id(1)

    @pl.when(c == 0)
    def _():
        s_scr[...] = s0_ref[0]
        prev_scr[...] = prev0_ref[0]

    x = qkv_ref[...]
    xp = jnp.concatenate([prev_scr[...], x], axis=0)
    base = SUBLANES - (CONV_W - 1)
    y = xp[base:base + L] * convw_ref[0:1, :]
    for j in range(1, CONV_W):
        y = y + xp[base + j:base + j + L] * convw_ref[j:j + 1, :]
    y_scr[...] = y * _sigmoid(y)
    prev_scr[...] = x[L - SUBLANES:L]

    a = sm_ref[:, LANES:2 * LANES]
    bb = sm_ref[:, 2 * LANES:3 * LANES]
    g = -jnp.exp(alog_ref[...]) * _softplus(a + dtb_ref[...])
    beta = _sigmoid(bb)
    rowl = lax.broadcasted_iota(jnp.int32, g.shape, 0)
    cum = g
    d = 1
    while d < L:
        cum = cum + jnp.where(rowl >= d, pltpu.roll(cum, d, axis=0), 0.0)
        d *= 2
    cum_t = jnp.transpose(jnp.concatenate([cum, jnp.zeros((LANES - L, LANES), F32)], axis=0))

    row = lax.broadcasted_iota(jnp.int32, (L, L), 0)
    col = lax.broadcasted_iota(jnp.int32, (L, L), 1)
    tri = row >= col
    strict = row > col
    gn = gn_ref[...]

    for h in range(n_heads):
        sl = slice(h * hd, (h + 1) * hd)
        qh = y_scr[:, h * hd:(h + 1) * hd]
        kh = y_scr[:, (n_heads + h) * hd:(n_heads + h + 1) * hd]
        vh = y_scr[:, (2 * n_heads + h) * hd:(2 * n_heads + h + 1) * hd]
        qh = (qh * lax.rsqrt(jnp.sum(qh * qh, axis=-1, keepdims=True) + EPS)) * (hd ** -0.5)
        kh = kh * lax.rsqrt(jnp.sum(kh * kh, axis=-1, keepdims=True) + EPS)
        gcol = cum[:, h:h + 1]
        grow = cum_t[h:h + 1, 0:L]
        bcol = beta[:, h:h + 1]
        diff = gcol - grow
        decay = jnp.where(tri, jnp.exp(jnp.where(tri, diff, 0.0)), 0.0)
        kb = kh * bcol
        kh_b = kh.astype(BF16)
        lmat = jnp.where(strict, _dot_nt(kb.astype(BF16), kh_b) * decay, 0.0)
        t = _inv_unit_lower(lmat, L).astype(BF16)
        e_g = jnp.exp(gcol)
        u = _dot(t, (vh * bcol).astype(BF16))
        w = _dot(t, (kb * e_g).astype(BF16))
        s_old = s_scr[h]
        s_b = s_old.astype(BF16)
        v_new = u - _dot(w.astype(BF16), s_b)
        v_new_b = v_new.astype(BF16)
        qh_b = qh.astype(BF16)
        attn = jnp.where(tri, _dot_nt(qh_b, kh_b) * decay, 0.0)
        o = _dot((qh * e_g).astype(BF16), s_b) + _dot(attn.astype(BF16), v_new_b)
        g_last = cum[L - 1:L, h:h + 1]
        kd = kh * jnp.exp(g_last - gcol)
        s_scr[h] = s_old * jnp.exp(g_last) + _dot_tn(kd.astype(BF16), v_new_b)
        zh = z_ref[:, sl]
        on = (o * lax.rsqrt(jnp.mean(o * o, axis=-1, keepdims=True) + EPS)) * gn
        o_ref[:, sl] = on * (zh * _sigmoid(zh))

    @pl.when(c == pl.num_programs(1) - 1)
    def _():
        sout_ref[0] = s_scr[...]


def gdn(u, prev0, s0, conv_w, a_log, dt_bias, gn, b, s, col_z, col_sm, n_heads):
    hd = HEAD_DIM
    L = min(s, CHUNK)
    nc = s // L
    qkv_w = 3 * n_heads * hd
    row = lambda bi, c: bi * nc + c
    alog = jnp.zeros((1, LANES), F32).at[0, :n_heads].set(a_log)
    dtb = jnp.zeros((1, LANES), F32).at[0, :n_heads].set(dt_bias)
    return pl.pallas_call(
        functools.partial(_gdn_body, L=L, n_heads=n_heads),
        grid=(b, nc),
        in_specs=[pl.BlockSpec((L, qkv_w), lambda bi, c: (row(bi, c), 0)),
                  pl.BlockSpec((L, n_heads * hd), lambda bi, c: (row(bi, c), col_z)),
                  pl.BlockSpec((L, 512), lambda bi, c: (row(bi, c), col_sm)),
                  pl.BlockSpec((1, SUBLANES, qkv_w), lambda bi, c: (bi, 0, 0)),
                  pl.BlockSpec((1, n_heads, hd, hd), lambda bi, c: (bi, 0, 0, 0)),
                  pl.BlockSpec((CONV_W, qkv_w), lambda bi, c: (0, 0)),
                  pl.BlockSpec((1, LANES), lambda bi, c: (0, 0)),
                  pl.BlockSpec((1, LANES), lambda bi, c: (0, 0)),
                  pl.BlockSpec((1, hd), lambda bi, c: (0, 0))],
        out_specs=[pl.BlockSpec((L, n_heads * hd), lambda bi, c: (row(bi, c), 0)),
                   pl.BlockSpec((1, n_heads, hd, hd), lambda bi, c: (bi, 0, 0, 0))],
        out_shape=[jax.ShapeDtypeStruct((b * s, n_heads * hd), F32),
                   jax.ShapeDtypeStruct((b, n_heads, hd, hd), F32)],
        scratch_shapes=[pltpu.VMEM((n_heads, hd, hd), F32),
                        pltpu.VMEM((SUBLANES, qkv_w), F32),
                        pltpu.VMEM((L, qkv_w), F32)],
        compiler_params=_cparams(("parallel", "arbitrary")),
        name="gdn",
    )(u, u, u, prev0, s0, conv_w, alog, dtb, gn.reshape(1, hd))


def _merge_body(ga_ref, gb_ref, oa_ref, ob_ref, x_ref, w_ref, o_ref, m_scr):
    @pl.when(pl.program_id(1) == 0)
    def _():
        m_scr[...] = (_sigmoid(ga_ref[...]) * oa_ref[...] + _sigmoid(gb_ref[...]) * ob_ref[...]).astype(BF16)

    o_ref[...] = x_ref[...] + _dot(m_scr[...], w_ref[...])


def merge_out(u, oa, ob, x, w, tm, tn, col_ga, col_gb):
    n, d = x.shape
    return pl.pallas_call(
        _merge_body,
        grid=(n // tm, d // tn),
        in_specs=[pl.BlockSpec((tm, d), lambda i, j: (i, col_ga)),
                  pl.BlockSpec((tm, d), lambda i, j: (i, col_gb)),
                  pl.BlockSpec((tm, d), lambda i, j: (i, 0)),
                  pl.BlockSpec((tm, d), lambda i, j: (i, 0)),
                  pl.BlockSpec((tm, tn), lambda i, j: (i, j)),
                  pl.BlockSpec((d, tn), lambda i, j: (0, j))],
        out_specs=pl.BlockSpec((tm, tn), lambda i, j: (i, j)),
        out_shape=jax.ShapeDtypeStruct((n, d), F32),
        scratch_shapes=[pltpu.VMEM((tm, d), BF16)],
        compiler_params=_cparams(("parallel", "arbitrary")),
        name="merge_out",
    )(u, u, oa, ob, x, w)


def _route_body(x_ref, g_ref, whi_ref, wlo_ref, bias_ref, h_ref, eid_ref, wgt_ref):
    x = x_ref[...]
    hn = (x * lax.rsqrt(jnp.mean(x * x, axis=-1, keepdims=True) + EPS)) * g_ref[...]
    h_hi, h_lo = _split_bf16(hn)
    h_ref[...] = h_hi
    lg = _dot3(h_hi, h_lo, whi_ref[...], wlo_ref[...]) + bias_ref[...]
    lane = lax.broadcasted_iota(jnp.int32, lg.shape, 1)
    big = jnp.int32(LANES)
    neg = -jnp.inf
    glog = jnp.where(lane < N_GROUPS, lg, neg)
    gmax = jnp.max(glog, axis=-1, keepdims=True)
    pg_top = 1.0 / jnp.sum(jnp.exp(glog - gmax), axis=-1, keepdims=True)
    grp = jnp.min(jnp.where(glog == gmax, lane, big), axis=-1, keepdims=True)
    lo = N_GROUPS + grp * EXPERTS_PER_GROUP
    le = jnp.where((lane >= lo) & (lane < lo + EXPERTS_PER_GROUP), lg, neg)
    v1 = jnp.max(le, axis=-1, keepdims=True)
    i1 = jnp.min(jnp.where(le == v1, lane, big), axis=-1, keepdims=True)
    le2 = jnp.where(lane == i1, neg, le)
    v2 = jnp.max(le2, axis=-1, keepdims=True)
    i2 = jnp.min(jnp.where(le2 == v2, lane, big), axis=-1, keepdims=True)
    e2 = jnp.exp(v2 - v1)
    den = 1.0 + e2
    w1 = pg_top * (1.0 / den)
    w2 = pg_top * (e2 / den)
    eid_ref[...] = jnp.where(lane == 0, i1 - N_GROUPS, jnp.where(lane == 1, i2 - N_GROUPS, 0))
    wgt_ref[...] = jnp.where(lane == 0, w1, jnp.where(lane == 1, w2, 0.0))


def route(x, g, whi, wlo, bias, tm):
    n, d = x.shape
    return pl.pallas_call(
        _route_body,
        grid=(n // tm,),
        in_specs=[pl.BlockSpec((tm, d), lambda i: (i, 0)),
                  pl.BlockSpec((1, d), lambda i: (0, 0)),
                  pl.BlockSpec((d, LANES), lambda i: (0, 0)),
                  pl.BlockSpec((d, LANES), lambda i: (0, 0)),
                  pl.BlockSpec((1, LANES), lambda i: (0, 0))],
        out_specs=[pl.BlockSpec((tm, d), lambda i: (i, 0)),
                   pl.BlockSpec((tm, LANES), lambda i: (i, 0)),
                   pl.BlockSpec((tm, LANES), lambda i: (i, 0))],
        out_shape=[jax.ShapeDtypeStruct((n, d), BF16),
                   jax.ShapeDtypeStruct((n, LANES), jnp.int32),
                   jax.ShapeDtypeStruct((n, LANES), F32)],
        compiler_params=_cparams(("parallel",)),
        name="route",
    )(x, g.reshape(1, d), whi, wlo, bias)


def _expert_body(be_tab, first_tab, nused, x_ref, sw_ref, wg_ref, wu_ref, wd_ref, o_ref, wg_s, wu_s, wd_s):
    blk = pl.program_id(0)

    @pl.when(blk < nused[0])
    def _():
        @pl.when(first_tab[blk] == 1)
        def _():
            wg_s[...] = wg_ref[0].astype(BF16)
            wu_s[...] = wu_ref[0].astype(BF16)
            wd_s[...] = wd_ref[0].astype(BF16)

        x = x_ref[...]
        gate = _dot(x, wg_s[...])
        up = _dot(x, wu_s[...])
        hid = (gate * _sigmoid(gate)) * up
        o_ref[...] = _dot(hid.astype(BF16), wd_s[...]) * sw_ref[...]

    @pl.when(blk >= nused[0])
    def _():
        o_ref[...] = jnp.zeros(o_ref.shape, F32)


def experts(x_sorted, slot_w, block_e, first, nused, w_gate, w_up, w_down, bm):
    n_slots, d = x_sorted.shape
    de = w_gate.shape[2]
    n_blocks = n_slots // bm
    grid_spec = pltpu.PrefetchScalarGridSpec(
        num_scalar_prefetch=3,
        grid=(n_blocks,),
        in_specs=[pl.BlockSpec((bm, d), lambda i, be, fi, nu: (i, 0)),
                  pl.BlockSpec((bm, 1), lambda i, be, fi, nu: (i, 0)),
                  pl.BlockSpec((1, d, de), lambda i, be, fi, nu: (be[i], 0, 0)),
                  pl.BlockSpec((1, d, de), lambda i, be, fi, nu: (be[i], 0, 0)),
                  pl.BlockSpec((1, de, d), lambda i, be, fi, nu: (be[i], 0, 0))],
        out_specs=pl.BlockSpec((bm, d), lambda i, be, fi, nu: (i, 0)),
        scratch_shapes=[pltpu.VMEM((d, de), BF16), pltpu.VMEM((d, de), BF16), pltpu.VMEM((de, d), BF16)],
    )
    return pl.pallas_call(
        _expert_body,
        grid_spec=grid_spec,
        out_shape=jax.ShapeDtypeStruct((n_slots, d), F32),
        compiler_params=_cparams(("arbitrary",)),
        name="experts",
    )(block_e, first, nused, x_sorted, slot_w, w_gate, w_up, w_down)


def _final_body(x_ref, ma_ref, mb_ref, g_ref, o_ref):
    x = x_ref[...] + (ma_ref[...] + mb_ref[...])
    o_ref[...] = (x * lax.rsqrt(jnp.mean(x * x, axis=-1, keepdims=True) + EPS)) * g_ref[...]


def final_norm(x, ma, mb, g, tm):
    n, d = x.shape
    spec = pl.BlockSpec((tm, d), lambda i: (i, 0))
    return pl.pallas_call(
        _final_body,
        grid=(n // tm,),
        in_specs=[spec, spec, spec, pl.BlockSpec((1, d), lambda i: (0, 0))],
        out_specs=spec,
        out_shape=jax.ShapeDtypeStruct((n, d), F32),
        compiler_params=_cparams(("parallel",)),
        name="final_norm",
    )(x, ma, mb, g.reshape(1, d))


def _tile(n, pref):
    t = min(n, pref)
    while n % t:
        t //= 2
    return t


def _rope_table(pos, b):
    half = ROPE_DIM // 2
    inv = ROPE_BASE ** (-jnp.arange(half, dtype=F32) / half)
    ang = pos.astype(F32)[:, None] * inv[None, :]
    cos, sin = jnp.cos(ang), jnp.sin(ang)
    cs = jnp.concatenate([cos, cos, sin, sin], axis=-1)
    return jnp.tile(cs, (b, 1))


def _rot_cols(w):
    half = ROPE_DIM // 2
    return jnp.concatenate([-w[..., half:], w[..., :half]], axis=-1)


def _moe(h_bf, eid, wgt, w_gate, w_up, w_down, bm):
    n, d = h_bf.shape
    n_exp = w_gate.shape[0]
    a = n * TOP_K
    e_flat = eid.reshape(-1)
    w_flat = wgt.reshape(-1)
    tok = jnp.repeat(jnp.arange(n, dtype=jnp.int32), TOP_K)
    order = jnp.argsort(e_flat)
    e_s, tok_s, w_s = e_flat[order], tok[order], w_flat[order]
    counts = jnp.zeros((n_exp,), jnp.int32).at[e_flat].add(1)
    start = jnp.cumsum(counts) - counts
    padded = (counts + bm - 1) // bm * bm
    pend = jnp.cumsum(padded)
    pstart = pend - padded
    dest = pstart[e_s] + (jnp.arange(a, dtype=jnp.int32) - start[e_s])
    n_blocks = (a + n_exp * (bm - 1) + bm - 1) // bm
    n_slots = n_blocks * bm
    slot_tok = jnp.full((n_slots,), n, jnp.int32).at[dest].set(tok_s)
    slot_w = jnp.zeros((n_slots,), F32).at[dest].set(w_s)
    block_e = jnp.minimum(jnp.searchsorted(pend, jnp.arange(n_blocks, dtype=jnp.int32) * bm, side='right'),
                          n_exp - 1).astype(jnp.int32)
    first = jnp.concatenate([jnp.ones((1,), jnp.int32), (block_e[1:] != block_e[:-1]).astype(jnp.int32)])
    nused = (pend[-1] // bm).astype(jnp.int32).reshape(1)
    x_pad = jnp.concatenate([h_bf, jnp.zeros((1, d), h_bf.dtype)], axis=0)
    x_sorted = x_pad[slot_tok]
    out = experts(x_sorted, slot_w.reshape(n_slots, 1), block_e, first, nused, w_gate, w_up, w_down, bm)
    slot_of = jnp.zeros((a,), jnp.int32).at[order].set(dest).reshape(n, TOP_K)
    return out[slot_of[:, 0]], out[slot_of[:, 1]]


def _layer(x3, pos, ckv_past, kr_past, s0, conv0, wp, final_g):
    b, s, d = x3.shape
    n = b * s
    n_heads = wp['n_heads']
    hd = HEAD_DIM
    x = x3.reshape(n, d)
    cols = wp['cols']

    u = rms_matmul(x, wp['attn_norm_g'], wp['w_all'], _tile(n, 1024), 512)

    cs = _rope_table(pos, b)
    tm = _tile(s, 256)
    ckv, kr_pad, q, k, v = mla_prep(u, cs, wp['q_norm_g'], wp['kv_norm_g'], wp['wq_ext'], wp['wkv'], b, s, tm,
                                    cols['q_lat'] // 512, cols['kv_lat'] // 512, cols['small'] // 512, n_heads)
    kr = kr_pad[:, :ROPE_DIM]
    scale = (hd + ROPE_DIM) ** -0.5
    if ckv_past is None:
        t_att = _tile(s, 512)
        o_a = flash_prompt(q, k, v, t_att, t_att, scale).reshape(n, n_heads * hd)
    else:
        lora = ckv.shape[1]
        qn = q[..., :hd].transpose(1, 0, 2, 3).reshape(n_heads, n, hd)
        q_abs = head_matmul(qn, wp['w_uk_t'])
        q_abs = q_abs.reshape(n_heads, b, s, lora).transpose(1, 2, 0, 3).reshape(b, s * n_heads, lora)
        qr = q[..., hd:].transpose(0, 2, 1, 3).reshape(b, s * n_heads, hd)
        kr_past_pad = jnp.pad(kr_past, ((0, 0), (0, 0), (0, LANES - ROPE_DIM)))
        ckv_new = jnp.pad(ckv.reshape(b, s, lora), ((0, 0), (0, LANES - s), (0, 0)))
        kr_new = jnp.pad(kr_pad.reshape(b, s, LANES), ((0, 0), (0, LANES - s), (0, 0)))
        o_lat = attn_sample(q_abs, qr, ckv_past, kr_past_pad, ckv_new, kr_new, scale, s, n_heads)
        o_lat = o_lat.reshape(b, s, n_heads, lora).transpose(2, 0, 1, 3).reshape(n_heads, n, lora)
        o_h = head_matmul(o_lat, wp['w_uv_h'])
        o_a = o_h.reshape(n_heads, n, hd).transpose(1, 0, 2).reshape(n, n_heads * hd)

    qkv_w = 3 * n_heads * hd
    prev0 = jnp.pad(conv0, ((0, 0), (SUBLANES - (CONV_W - 1), 0), (0, 0)))
    o_b, s_new = gdn(u, prev0, s0, wp['conv_w'], wp['a_log'], wp['dt_bias'], wp['gdn_norm_g'], b, s,
                     cols['z'] // (n_heads * hd), cols['small'] // 512, n_heads)
    xp = jnp.concatenate([conv0, u[:, :qkv_w].reshape(b, s, qkv_w)], axis=1)
    conv_new = xp[:, s:]

    x1 = merge_out(u, o_a, o_b, x, wp['w_out'], _tile(n, 512), 512, cols['gate_a'] // d, cols['gate_b'] // d)

    h_bf, eid, wgt = route(x1, wp['ffn_norm_g'], wp['wr_hi'], wp['wr_lo'], wp['br'], _tile(n, 512))
    ma, mb = _moe(h_bf, eid[:, :TOP_K], wgt[:, :TOP_K], wp['w_gate'], wp['w_up'], wp['w_down'], wp['bm'])
    y = final_norm(x1, ma, mb, final_g, _tile(n, 512))
    lora = ckv.shape[1]
    return (y.reshape(b, s, d), ckv.reshape(b, s, lora), kr.reshape(b, s, ROPE_DIM), s_new, conv_new)


def _prep_weights(l, attn_norm_g, w_in, q_norm_g, w_uq, kv_norm_g, w_uk, w_uv, conv_w, a_log, dt_bias, gdn_norm_g,
                  w_out, ffn_norm_g, w_group, b_group, w_router, b_router, w_gate, w_up, w_down):
    d = w_in.shape[1]
    q_lora, n_heads, _ = w_uq.shape[1:]
    kv_lora = w_uk.shape[1]
    hd = HEAD_DIM
    qkv_w = 3 * n_heads * hd
    sizes = (q_lora, kv_lora, ROPE_DIM, qkv_w, n_heads, n_heads, n_heads * hd, d, d)
    offs = np.concatenate([[0], np.cumsum(sizes)])
    wi = w_in[l]
    part = lambda i: wi[:, offs[i]:offs[i + 1]]
    w_qlat, w_kvlat, w_kr, w_qkv, w_a, w_b, w_z, w_ga, w_gb = [part(i) for i in range(9)]
    zpad = lambda c: jnp.zeros((d, c), F32)
    small = jnp.concatenate([w_kr, _rot_cols(w_kr), w_a, zpad(LANES - n_heads), w_b, zpad(LANES - n_heads),
                             zpad(LANES)], axis=1)
    w_all = jnp.concatenate([w_qkv, w_z, w_ga, w_gb, w_qlat, w_kvlat, small], axis=1).astype(BF16)
    cols = {'qkv': 0, 'z': qkv_w, 'gate_a': qkv_w + n_heads * hd, 'gate_b': qkv_w + n_heads * hd + d,
            'q_lat': qkv_w + n_heads * hd + 2 * d}
    cols['kv_lat'] = cols['q_lat'] + q_lora
    cols['small'] = cols['kv_lat'] + kv_lora
    wq = w_uq[l]
    wq_nope = wq[:, :, :hd].reshape(q_lora, n_heads * hd)
    wq_r = wq[:, :, hd:]
    wq_rope = jnp.concatenate([wq_r, _rot_cols(wq_r)], axis=-1).reshape(q_lora, n_heads * hd)
    wq_ext = jnp.concatenate([wq_nope, wq_rope], axis=1).astype(BF16)
    wkv = jnp.concatenate([w_uk[l].reshape(kv_lora, n_heads * hd), w_uv[l].reshape(kv_lora, n_heads * hd)],
                          axis=1).astype(BF16)
    wr = jnp.concatenate([w_group[l], w_router[l].transpose(1, 0, 2).reshape(d, -1)], axis=1)
    wr = jnp.pad(wr, ((0, 0), (0, LANES - wr.shape[1])))
    wr_hi = wr.astype(BF16)
    wr_lo = (wr - wr_hi.astype(F32)).astype(BF16)
    br = jnp.concatenate([b_group[l], b_router[l].reshape(-1)])
    br = jnp.pad(br, (0, LANES - br.shape[0])).reshape(1, LANES)
    return {
        'n_heads': n_heads, 'cols': cols, 'w_all': w_all, 'attn_norm_g': attn_norm_g[l],
        'q_norm_g': q_norm_g[l], 'kv_norm_g': kv_norm_g[l], 'wq_ext': wq_ext, 'wkv': wkv,
        'w_uk_t': w_uk[l].transpose(1, 2, 0).astype(BF16),
        'w_uv_h': w_uv[l].transpose(1, 0, 2).astype(BF16),
        'conv_w': conv_w[l], 'a_log': a_log[l], 'dt_bias': dt_bias[l], 'gdn_norm_g': gdn_norm_g[l],
        'w_out': w_out[l].astype(BF16), 'ffn_norm_g': ffn_norm_g[l], 'wr_hi': wr_hi, 'wr_lo': wr_lo, 'br': br,
        'w_gate': w_gate[l], 'w_up': w_up[l], 'w_down': w_down[l], 'bm': 256,
    }


def kernel(x_prompt, x_sample, cache_ckv, cache_k_rope, state_gdn, state_conv, attn_norm_g, w_in, q_norm_g, w_uq, kv_norm_g, w_uk, w_uv, conv_w, a_log, dt_bias, gdn_norm_g, w_out, ffn_norm_g, w_group, b_group, w_router, b_router, w_gate, w_up, w_down, final_norm_g):
    depth = w_in.shape[0]
    assert depth == 1, "final RMSNorm is fused into the last layer; deeper trunks are not supported"
    b_p, s_p, _ = x_prompt.shape
    past = cache_ckv.shape[2]
    n_heads = w_uq.shape[2]
    qkv_w = 3 * n_heads * HEAD_DIM
    pos_p = jnp.arange(s_p, dtype=jnp.int32)
    pos_s = past + jnp.arange(x_sample.shape[1], dtype=jnp.int32)
    wp = _prep_weights(0, attn_norm_g, w_in, q_norm_g, w_uq, kv_norm_g, w_uk, w_uv, conv_w, a_log, dt_bias,
                       gdn_norm_g, w_out, ffn_norm_g, w_group, b_group, w_router, b_router, w_gate, w_up, w_down)
    s0 = jnp.zeros((b_p, n_heads, HEAD_DIM, HEAD_DIM), F32)
    c0 = jnp.zeros((b_p, CONV_W - 1, qkv_w), F32)
    yp, ckv_p, kr_p, sg_p, sc_p = _layer(x_prompt, pos_p, None, None, s0, c0, wp, final_norm_g)
    ys, ckv_s, kr_s, sg_s, sc_s = _layer(x_sample, pos_s, cache_ckv[0], cache_k_rope[0], state_gdn[0],
                                         state_conv[0], wp, final_norm_g)
    return (yp, ys, ckv_p[None], kr_p[None], sg_p[None], sc_p[None], ckv_s[None], kr_s[None], sg_s[None], sc_s[None])
```

```python
import functools

import numpy as np
import jax
import jax.numpy as jnp
from jax import lax
from jax.experimental import pallas as pl
from jax.experimental.pallas import tpu as pltpu

F32 = jnp.float32
BF16 = jnp.bfloat16

EPS = 1e-6
CHUNK = 64
ROPE_BASE = 10000.0
HEAD_DIM = 128
ROPE_DIM = 64
CONV_W = 4
TOP_K = 2
N_GROUPS = 8
EXPERTS_PER_GROUP = 8

LANES = 128
SUBLANES = 8
VMEM_LIMIT = 56 * 1024 * 1024
ATTN_TILE = 512
ATTN_HEADS_PER_STEP = 2
MOE_BLOCK = 256


def _cparams(sem):
    return pltpu.CompilerParams(dimension_semantics=sem, vmem_limit_bytes=VMEM_LIMIT)


def _dot(a, b):
    return jnp.dot(a, b, preferred_element_type=F32)


def _dot_nt(a, b):
    return lax.dot_general(a, b, (((1,), (1,)), ((), ())), preferred_element_type=F32)


def _dot_tn(a, b):
    return lax.dot_general(a, b, (((0,), (0,)), ((), ())), preferred_element_type=F32)


def _sigmoid(x):
    return 1.0 / (1.0 + jnp.exp(-x))


def _split_bf16(a):
    hi = a.astype(BF16)
    lo = (a - hi.astype(F32)).astype(BF16)
    return hi, lo


def _dot3(a_hi, a_lo, b_hi, b_lo):
    return _dot(a_hi, b_hi) + (_dot(a_hi, b_lo) + _dot(a_lo, b_hi))


def _rms_mm_body(x_ref, g_ref, w_ref, o_ref, h_ref):
    @pl.when(pl.program_id(1) == 0)
    def _():
        x = x_ref[...]
        ms = jnp.mean(x * x, axis=-1, keepdims=True)
        h_ref[...] = ((x * lax.rsqrt(ms + EPS)) * g_ref[...]).astype(BF16)

    o_ref[...] = _dot(h_ref[...], w_ref[...])


def rms_matmul(x, g, w, tm, tn):
    n, k = x.shape
    m = w.shape[1]
    return pl.pallas_call(
        _rms_mm_body,
        grid=(n // tm, m // tn),
        in_specs=[pl.BlockSpec((tm, k), lambda i, j: (i, 0)),
                  pl.BlockSpec((1, k), lambda i, j: (0, 0)),
                  pl.BlockSpec((k, tn), lambda i, j: (0, j))],
        out_specs=pl.BlockSpec((tm, tn), lambda i, j: (i, j)),
        out_shape=jax.ShapeDtypeStruct((n, m), F32),
        scratch_shapes=[pltpu.VMEM((tm, k), BF16)],
        compiler_params=_cparams(("parallel", "arbitrary")),
        name="rms_matmul",
    )(x, g.reshape(1, k), w)


def _mla_prep_body(ql_ref, kvl_ref, sm_ref, cs_ref, gq_ref, gkv_ref, wq_ref, wkv_ref,
                   ckv_ref, kr_ref, q_ref, k_ref, v_ref, *, n_heads, scale):
    hd = HEAD_DIM
    ql = ql_ref[...]
    qn = ((ql * lax.rsqrt(jnp.mean(ql * ql, axis=-1, keepdims=True) + EPS)) * gq_ref[...]).astype(BF16)
    q_all = _dot(qn, wq_ref[...])
    kvl = kvl_ref[...]
    ckv = (kvl * lax.rsqrt(jnp.mean(kvl * kvl, axis=-1, keepdims=True) + EPS)) * gkv_ref[...]
    ckv_ref[...] = ckv
    kv_all = _dot(ckv.astype(BF16), wkv_ref[...])

    cs = cs_ref[...]
    lane = lax.broadcasted_iota(jnp.int32, cs.shape, 1)

    def rope(t):
        t = t * cs
        return jnp.where(lane < ROPE_DIM, t + pltpu.roll(t, ROPE_DIM, axis=1), 0.0)

    kr = rope(sm_ref[:, 0:LANES])
    kr_ref[...] = kr
    kr_b = kr.astype(BF16)
    off = n_heads * hd
    for h in range(n_heads):
        q_ref[0, h, :, 0:hd] = (q_all[:, h * hd:(h + 1) * hd] * scale).astype(BF16)
        q_ref[0, h, :, hd:2 * hd] = (rope(q_all[:, off + h * hd:off + (h + 1) * hd]) * scale).astype(BF16)
        k_ref[0, h, :, 0:hd] = kv_all[:, h * hd:(h + 1) * hd].astype(BF16)
        k_ref[0, h, :, hd:2 * hd] = kr_b
        v_ref[0, h] = kv_all[:, off + h * hd:off + (h + 1) * hd].astype(BF16)


def mla_prep(u, cs, gq, gkv, wq, wkv, b, s, tm, col_ql, col_kvl, col_sm, n_heads, scale):
    lora = wq.shape[0]
    nt = s // tm
    row = lambda bi, si: bi * nt + si
    hd = HEAD_DIM
    return pl.pallas_call(
        functools.partial(_mla_prep_body, n_heads=n_heads, scale=scale),
        grid=(b, nt),
        in_specs=[pl.BlockSpec((tm, lora), lambda bi, si: (row(bi, si), col_ql)),
                  pl.BlockSpec((tm, lora), lambda bi, si: (row(bi, si), col_kvl)),
                  pl.BlockSpec((tm, 512), lambda bi, si: (row(bi, si), col_sm)),
                  pl.BlockSpec((tm, LANES), lambda bi, si: (row(bi, si), 0)),
                  pl.BlockSpec((1, lora), lambda bi, si: (0, 0)),
                  pl.BlockSpec((1, lora), lambda bi, si: (0, 0)),
                  pl.BlockSpec(wq.shape, lambda bi, si: (0, 0)),
                  pl.BlockSpec(wkv.shape, lambda bi, si: (0, 0))],
        out_specs=[pl.BlockSpec((tm, lora), lambda bi, si: (row(bi, si), 0)),
                   pl.BlockSpec((tm, LANES), lambda bi, si: (row(bi, si), 0)),
                   pl.BlockSpec((1, n_heads, tm, 2 * hd), lambda bi, si: (bi, 0, si, 0)),
                   pl.BlockSpec((1, n_heads, tm, 2 * hd), lambda bi, si: (bi, 0, si, 0)),
                   pl.BlockSpec((1, n_heads, tm, hd), lambda bi, si: (bi, 0, si, 0))],
        out_shape=[jax.ShapeDtypeStruct((b * s, lora), F32),
                   jax.ShapeDtypeStruct((b * s, LANES), F32),
                   jax.ShapeDtypeStruct((b, n_heads, s, 2 * hd), BF16),
                   jax.ShapeDtypeStruct((b, n_heads, s, 2 * hd), BF16),
                   jax.ShapeDtypeStruct((b, n_heads, s, hd), BF16)],
        compiler_params=_cparams(("parallel", "parallel")),
        name="mla_prep",
    )(u, u, u, cs, gq.reshape(1, lora), gkv.reshape(1, lora), wq, wkv)


def _flash_body(q_ref, k_ref, v_ref, o_ref, m_scr, l_scr, acc_scr, *, t, gh):
    qi = pl.program_id(2)
    hd = HEAD_DIM
    m_scr[...] = jnp.full(m_scr.shape, -jnp.inf, F32)
    l_scr[...] = jnp.zeros(l_scr.shape, F32)
    acc_scr[...] = jnp.zeros(acc_scr.shape, F32)

    def block(j, masked):
        start = pl.multiple_of(j * t, t)
        for g in range(gh):
            k = k_ref[0, g, pl.ds(start, t), :]
            v = v_ref[0, g, pl.ds(start, t), :]
            sc = _dot_nt(q_ref[0, g], k)
            if masked:
                row = lax.broadcasted_iota(jnp.int32, sc.shape, 0)
                col = lax.broadcasted_iota(jnp.int32, sc.shape, 1)
                sc = jnp.where(col // CHUNK <= row // CHUNK, sc, -jnp.inf)
            m_prev = m_scr[g]
            m_new = jnp.maximum(m_prev, jnp.max(sc, axis=-1, keepdims=True))
            alpha = jnp.exp(m_prev - m_new)
            p = jnp.exp(sc - m_new)
            l_scr[g] = alpha * l_scr[g] + jnp.sum(p, axis=-1, keepdims=True)
            acc_scr[g] = alpha * acc_scr[g] + _dot(p.astype(BF16), v)
            m_scr[g] = m_new

    def full_block(j, carry):
        block(j, False)
        return carry

    lax.fori_loop(0, qi, full_block, 0)
    block(qi, True)
    for g in range(gh):
        o_ref[0, :, g * hd:(g + 1) * hd] = acc_scr[g] / l_scr[g]


def flash_prompt(q, k, v, t, gh):
    b, nh, s, dk = q.shape
    hd = v.shape[-1]
    assert t % CHUNK == 0 and s % t == 0 and nh % gh == 0
    return pl.pallas_call(
        functools.partial(_flash_body, t=t, gh=gh),
        grid=(b, nh // gh, s // t),
        in_specs=[pl.BlockSpec((1, gh, t, dk), lambda bi, hp, qi: (bi, hp, qi, 0)),
                  pl.BlockSpec((1, gh, s, dk), lambda bi, hp, qi: (bi, hp, 0, 0)),
                  pl.BlockSpec((1, gh, s, hd), lambda bi, hp, qi: (bi, hp, 0, 0))],
        out_specs=pl.BlockSpec((1, t, gh * hd), lambda bi, hp, qi: (bi, qi, hp)),
        out_shape=jax.ShapeDtypeStruct((b, s, nh * hd), F32),
        scratch_shapes=[pltpu.VMEM((gh, t, 1), F32), pltpu.VMEM((gh, t, 1), F32), pltpu.VMEM((gh, t, hd), F32)],
        compiler_params=_cparams(("parallel", "parallel", "arbitrary")),
        name="flash_prompt",
    )(q, k, v)


def _bmm_body(a_ref, b_ref, o_ref):
    o_ref[0] = _dot(a_ref[0].astype(BF16), b_ref[0])


def head_matmul(a, b):
    nh, m, k = a.shape
    n = b.shape[2]
    return pl.pallas_call(
        _bmm_body,
        grid=(nh,),
        in_specs=[pl.BlockSpec((1, m, k), lambda h: (h, 0, 0)),
                  pl.BlockSpec((1, k, n), lambda h: (h, 0, 0))],
        out_specs=pl.BlockSpec((1, m, n), lambda h: (h, 0, 0)),
        out_shape=jax.ShapeDtypeStruct((nh, m, n), F32),
        compiler_params=_cparams(("parallel",)),
        name="head_matmul",
    )(a, b)


def _attn_sample_body(qa_ref, qr_ref, kp_ref, krp_ref, kn_ref, krn_ref, o_ref, *, past, s_new, n_heads):
    qa = qa_ref[0].astype(BF16)
    qr = qr_ref[0].astype(BF16)
    kp = kp_ref[0].astype(BF16)
    krp = krp_ref[0].astype(BF16)
    kn = kn_ref[0].astype(BF16)
    krn = krn_ref[0].astype(BF16)
    s_past = _dot_nt(qa, kp) + _dot_nt(qr, krp)
    s_n = _dot_nt(qa, kn) + _dot_nt(qr, krn)
    row = lax.broadcasted_iota(jnp.int32, s_n.shape, 0)
    col = lax.broadcasted_iota(jnp.int32, s_n.shape, 1)
    qpos = past + row // n_heads
    kpos = past + col
    valid = (col < s_new) & (kpos // CHUNK <= qpos // CHUNK)
    s_n = jnp.where(valid, s_n, -jnp.inf)
    m = jnp.maximum(jnp.max(s_past, axis=-1, keepdims=True), jnp.max(s_n, axis=-1, keepdims=True))
    pp = jnp.exp(s_past - m)
    pn = jnp.exp(s_n - m)
    l = jnp.sum(pp, axis=-1, keepdims=True) + jnp.sum(pn, axis=-1, keepdims=True)
    o_ref[0] = (_dot(pp.astype(BF16), kp) + _dot(pn.astype(BF16), kn)) / l


def attn_sample(qa, qr, ckv_past, kr_past, ckv_new, kr_new, s_new, n_heads):
    b, r, lora = qa.shape
    past = ckv_past.shape[1]
    return pl.pallas_call(
        functools.partial(_attn_sample_body, past=past, s_new=s_new, n_heads=n_heads),
        grid=(b,),
        in_specs=[pl.BlockSpec((1, r, lora), lambda i: (i, 0, 0)),
                  pl.BlockSpec((1, r, LANES), lambda i: (i, 0, 0)),
                  pl.BlockSpec((1, past, lora), lambda i: (i, 0, 0)),
                  pl.BlockSpec((1, past, LANES), lambda i: (i, 0, 0)),
                  pl.BlockSpec((1, LANES, lora), lambda i: (i, 0, 0)),
                  pl.BlockSpec((1, LANES, LANES), lambda i: (i, 0, 0))],
        out_specs=pl.BlockSpec((1, r, lora), lambda i: (i, 0, 0)),
        out_shape=jax.ShapeDtypeStruct((b, r, lora), F32),
        compiler_params=_cparams(("parallel",)),
        name="attn_sample",
    )(qa, qr, ckv_past, kr_past, ckv_new, kr_new)


def _softplus(x):
    return jnp.maximum(x, 0.0) + jnp.log1p(jnp.exp(-jnp.abs(x)))


def _gdn_body(qkv_ref, z_ref, sm_ref, prev0_ref, s0_ref, convw_ref, alog_ref, dtb_ref, gn_ref,
              o_ref, sout_ref, s_scr, prev_scr, y_scr, *, L, n_heads):
    hd = HEAD_DIM
    heads = range(n_heads)
    c = pl.program_id(1)

    @pl.when(c == 0)
    def _():
        s_scr[...] = s0_ref[0]
        prev_scr[...] = prev0_ref[0]

    x = qkv_ref[...]
    xp = jnp.concatenate([prev_scr[...], x], axis=0)
    base = SUBLANES - (CONV_W - 1)
    y = xp[base:base + L] * convw_ref[0:1, :]
    for j in range(1, CONV_W):
        y = y + xp[base + j:base + j + L] * convw_ref[j:j + 1, :]
    y_scr[...] = y * _sigmoid(y)
    prev_scr[...] = x[L - SUBLANES:L]

    a = sm_ref[:, LANES:2 * LANES]
    bb = sm_ref[:, 2 * LANES:3 * LANES]
    g = -jnp.exp(alog_ref[...]) * _softplus(a + dtb_ref[...])
    beta = _sigmoid(bb)
    rowl = lax.broadcasted_iota(jnp.int32, g.shape, 0)
    cum = g
    d = 1
    while d < L:
        cum = cum + jnp.where(rowl >= d, pltpu.roll(cum, d, axis=0), 0.0)
        d *= 2
    cum_t = jnp.transpose(jnp.concatenate([cum, jnp.zeros((LANES - L, LANES), F32)], axis=0))

    row = lax.broadcasted_iota(jnp.int32, (L, L), 0)
    col = lax.broadcasted_iota(jnp.int32, (L, L), 1)
    tri = row >= col
    strict = row > col
    eye = jnp.where(row == col, 1.0, 0.0)
    gn = gn_ref[...]

    q_b, k_b, kb_b, rhs_b, qg_b, kd_b = [], [], [], [], [], []
    for h in heads:
        qh = y_scr[:, h * hd:(h + 1) * hd]
        kh = y_scr[:, (n_heads + h) * hd:(n_heads + h + 1) * hd]
        vh = y_scr[:, (2 * n_heads + h) * hd:(2 * n_heads + h + 1) * hd]
        qh = (qh * lax.rsqrt(jnp.sum(qh * qh, axis=-1, keepdims=True) + EPS)) * (hd ** -0.5)
        kh = kh * lax.rsqrt(jnp.sum(kh * kh, axis=-1, keepdims=True) + EPS)
        gcol = cum[:, h:h + 1]
        bcol = beta[:, h:h + 1]
        e_g = jnp.exp(gcol)
        kb = kh * bcol
        q_b.append(qh.astype(BF16))
        k_b.append(kh.astype(BF16))
        kb_b.append(kb.astype(BF16))
        rhs_b.append(jnp.concatenate([vh * bcol, kb * e_g], axis=1).astype(BF16))
        qg_b.append((qh * e_g).astype(BF16))
        kd_b.append((kh * jnp.exp(cum[L - 1:L, h:h + 1] - gcol)).astype(BF16))

    kk = [_dot_nt(kb_b[h], k_b[h]) for h in heads]
    qk = [_dot_nt(q_b[h], k_b[h]) for h in heads]

    nmat, attn_b = [], []
    for h in heads:
        diff = cum[:, h:h + 1] - cum_t[h:h + 1, 0:L]
        decay = jnp.where(tri, jnp.exp(jnp.where(tri, diff, 0.0)), 0.0)
        nmat.append(jnp.where(strict, kk[h] * decay, 0.0))
        attn_b.append(jnp.where(tri, qk[h] * decay, 0.0).astype(BF16))

    tmat = [eye - nmat[h] for h in heads]
    p_sp = [_split_bf16(nmat[h]) for h in heads]
    pmat = [_dot3(*p_sp[h], *p_sp[h]) for h in heads]
    power = 2
    while True:
        p_sp = [_split_bf16(pmat[h]) for h in heads]
        t_sp = [_split_bf16(tmat[h]) for h in heads]
        tmat = [tmat[h] + _dot3(*t_sp[h], *p_sp[h]) for h in heads]
        power *= 2
        if power >= L:
            break
        pmat = [_dot3(*p_sp[h], *p_sp[h]) for h in heads]

    uw = [_dot(tmat[h].astype(BF16), rhs_b[h]) for h in heads]
    s_old = [s_scr[h] for h in heads]
    s_b = [s_old[h].astype(BF16) for h in heads]
    ws = [_dot(jnp.concatenate([uw[h][:, hd:].astype(BF16), qg_b[h]], axis=0), s_b[h]) for h in heads]
    vn_b = [(uw[h][:, :hd] - ws[h][:L]).astype(BF16) for h in heads]
    av = [_dot(attn_b[h], vn_b[h]) for h in heads]
    ks = [_dot_tn(kd_b[h], vn_b[h]) for h in heads]
    for h in heads:
        sl = slice(h * hd, (h + 1) * hd)
        o = ws[h][L:] + av[h]
        s_scr[h] = s_old[h] * jnp.exp(cum[L - 1:L, h:h + 1]) + ks[h]
        zh = z_ref[:, sl]
        on = (o * lax.rsqrt(jnp.mean(o * o, axis=-1, keepdims=True) + EPS)) * gn
        o_ref[:, sl] = on * (zh * _sigmoid(zh))

    @pl.when(c == pl.num_programs(1) - 1)
    def _():
        sout_ref[0] = s_scr[...]


def gdn(u, prev0, s0, conv_w, a_log, dt_bias, gn, b, s, col_z, col_sm, n_heads):
    hd = HEAD_DIM
    L = min(s, CHUNK)
    nc = s // L
    qkv_w = 3 * n_heads * hd
    row = lambda bi, c: bi * nc + c
    alog = jnp.zeros((1, LANES), F32).at[0, :n_heads].set(a_log)
    dtb = jnp.zeros((1, LANES), F32).at[0, :n_heads].set(dt_bias)
    return pl.pallas_call(
        functools.partial(_gdn_body, L=L, n_heads=n_heads),
        grid=(b, nc),
        in_specs=[pl.BlockSpec((L, qkv_w), lambda bi, c: (row(bi, c), 0)),
                  pl.BlockSpec((L, n_heads * hd), lambda bi, c: (row(bi, c), col_z)),
                  pl.BlockSpec((L, 512), lambda bi, c: (row(bi, c), col_sm)),
                  pl.BlockSpec((1, SUBLANES, qkv_w), lambda bi, c: (bi, 0, 0)),
                  pl.BlockSpec((1, n_heads, hd, hd), lambda bi, c: (bi, 0, 0, 0)),
                  pl.BlockSpec((CONV_W, qkv_w), lambda bi, c: (0, 0)),
                  pl.BlockSpec((1, LANES), lambda bi, c: (0, 0)),
                  pl.BlockSpec((1, LANES), lambda bi, c: (0, 0)),
                  pl.BlockSpec((1, hd), lambda bi, c: (0, 0))],
        out_specs=[pl.BlockSpec((L, n_heads * hd), lambda bi, c: (row(bi, c), 0)),
                   pl.BlockSpec((1, n_heads, hd, hd), lambda bi, c: (bi, 0, 0, 0))],
        out_shape=[jax.ShapeDtypeStruct((b * s, n_heads * hd), F32),
                   jax.ShapeDtypeStruct((b, n_heads, hd, hd), F32)],
        scratch_shapes=[pltpu.VMEM((n_heads, hd, hd), F32),
                        pltpu.VMEM((SUBLANES, qkv_w), F32),
                        pltpu.VMEM((L, qkv_w), F32)],
        compiler_params=_cparams(("parallel", "arbitrary")),
        name="gdn",
    )(u, u, u, prev0, s0, conv_w, alog, dtb, gn.reshape(1, hd))


def _merge_body(ga_ref, gb_ref, oa_ref, ob_ref, x_ref, w_ref, o_ref, m_scr):
    @pl.when(pl.program_id(1) == 0)
    def _():
        m_scr[...] = (_sigmoid(ga_ref[...]) * oa_ref[...] + _sigmoid(gb_ref[...]) * ob_ref[...]).astype(BF16)

    o_ref[...] = x_ref[...] + _dot(m_scr[...], w_ref[...])


def merge_out(u, oa, ob, x, w, tm, tn, col_ga, col_gb):
    n, d = x.shape
    return pl.pallas_call(
        _merge_body,
        grid=(n // tm, d // tn),
        in_specs=[pl.BlockSpec((tm, d), lambda i, j: (i, col_ga)),
                  pl.BlockSpec((tm, d), lambda i, j: (i, col_gb)),
                  pl.BlockSpec((tm, d), lambda i, j: (i, 0)),
                  pl.BlockSpec((tm, d), lambda i, j: (i, 0)),
                  pl.BlockSpec((tm, tn), lambda i, j: (i, j)),
                  pl.BlockSpec((d, tn), lambda i, j: (0, j))],
        out_specs=pl.BlockSpec((tm, tn), lambda i, j: (i, j)),
        out_shape=jax.ShapeDtypeStruct((n, d), F32),
        scratch_shapes=[pltpu.VMEM((tm, d), BF16)],
        compiler_params=_cparams(("parallel", "arbitrary")),
        name="merge_out",
    )(u, u, oa, ob, x, w)


def _route_body(x_ref, g_ref, whi_ref, wlo_ref, bias_ref, h_ref, eid_ref, wgt_ref):
    x = x_ref[...]
    hn = (x * lax.rsqrt(jnp.mean(x * x, axis=-1, keepdims=True) + EPS)) * g_ref[...]
    h_hi, h_lo = _split_bf16(hn)
    h_ref[...] = h_hi
    lg = _dot3(h_hi, h_lo, whi_ref[...], wlo_ref[...]) + bias_ref[...]
    lane = lax.broadcasted_iota(jnp.int32, lg.shape, 1)
    big = jnp.int32(LANES)
    neg = -jnp.inf
    glog = jnp.where(lane < N_GROUPS, lg, neg)
    gmax = jnp.max(glog, axis=-1, keepdims=True)
    pg_top = 1.0 / jnp.sum(jnp.exp(glog - gmax), axis=-1, keepdims=True)
    grp = jnp.min(jnp.where(glog == gmax, lane, big), axis=-1, keepdims=True)
    lo = N_GROUPS + grp * EXPERTS_PER_GROUP
    le = jnp.where((lane >= lo) & (lane < lo + EXPERTS_PER_GROUP), lg, neg)
    v1 = jnp.max(le, axis=-1, keepdims=True)
    i1 = jnp.min(jnp.where(le == v1, lane, big), axis=-1, keepdims=True)
    le2 = jnp.where(lane == i1, neg, le)
    v2 = jnp.max(le2, axis=-1, keepdims=True)
    i2 = jnp.min(jnp.where(le2 == v2, lane, big), axis=-1, keepdims=True)
    e2 = jnp.exp(v2 - v1)
    den = 1.0 + e2
    w1 = pg_top * (1.0 / den)
    w2 = pg_top * (e2 / den)
    eid_ref[...] = jnp.where(lane == 0, i1 - N_GROUPS, jnp.where(lane == 1, i2 - N_GROUPS, 0))
    wgt_ref[...] = jnp.where(lane == 0, w1, jnp.where(lane == 1, w2, 0.0))


def route(x, g, whi, wlo, bias, tm):
    n, d = x.shape
    return pl.pallas_call(
        _route_body,
        grid=(n // tm,),
        in_specs=[pl.BlockSpec((tm, d), lambda i: (i, 0)),
                  pl.BlockSpec((1, d), lambda i: (0, 0)),
                  pl.BlockSpec((d, LANES), lambda i: (0, 0)),
                  pl.BlockSpec((d, LANES), lambda i: (0, 0)),
                  pl.BlockSpec((1, LANES), lambda i: (0, 0))],
        out_specs=[pl.BlockSpec((tm, d), lambda i: (i, 0)),
                   pl.BlockSpec((tm, LANES), lambda i: (i, 0)),
                   pl.BlockSpec((tm, LANES), lambda i: (i, 0))],
        out_shape=[jax.ShapeDtypeStruct((n, d), BF16),
                   jax.ShapeDtypeStruct((n, LANES), jnp.int32),
                   jax.ShapeDtypeStruct((n, LANES), F32)],
        compiler_params=_cparams(("parallel",)),
        name="route",
    )(x, g.reshape(1, d), whi, wlo, bias)


def _expert_body(be_tab, first_tab, nused, x_ref, sw_ref, wg_ref, wu_ref, wd_ref, o_ref, wg_s, wu_s, wd_s):
    blk = pl.program_id(0)

    @pl.when(blk < nused[0])
    def _():
        @pl.when(first_tab[blk] == 1)
        def _():
            wg_s[...] = wg_ref[0].astype(BF16)
            wu_s[...] = wu_ref[0].astype(BF16)
            wd_s[...] = wd_ref[0].astype(BF16)

        x = x_ref[...]
        gate = _dot(x, wg_s[...])
        up = _dot(x, wu_s[...])
        hid = (gate * _sigmoid(gate)) * up
        o_ref[...] = _dot(hid.astype(BF16), wd_s[...]) * sw_ref[...]

    @pl.when(blk >= nused[0])
    def _():
        o_ref[...] = jnp.zeros(o_ref.shape, F32)


def experts(x_sorted, slot_w, block_e, first, nused, w_gate, w_up, w_down, bm):
    n_slots, d = x_sorted.shape
    de = w_gate.shape[2]
    n_blocks = n_slots // bm
    grid_spec = pltpu.PrefetchScalarGridSpec(
        num_scalar_prefetch=3,
        grid=(n_blocks,),
        in_specs=[pl.BlockSpec((bm, d), lambda i, be, fi, nu: (i, 0)),
                  pl.BlockSpec((bm, 1), lambda i, be, fi, nu: (i, 0)),
                  pl.BlockSpec((1, d, de), lambda i, be, fi, nu: (be[i], 0, 0)),
                  pl.BlockSpec((1, d, de), lambda i, be, fi, nu: (be[i], 0, 0)),
                  pl.BlockSpec((1, de, d), lambda i, be, fi, nu: (be[i], 0, 0))],
        out_specs=pl.BlockSpec((bm, d), lambda i, be, fi, nu: (i, 0)),
        scratch_shapes=[pltpu.VMEM((d, de), BF16), pltpu.VMEM((d, de), BF16), pltpu.VMEM((de, d), BF16)],
    )
    return pl.pallas_call(
        _expert_body,
        grid_spec=grid_spec,
        out_shape=jax.ShapeDtypeStruct((n_slots, d), F32),
        compiler_params=_cparams(("arbitrary",)),
        name="experts",
    )(block_e, first, nused, x_sorted, slot_w, w_gate, w_up, w_down)


def _final_body(x_ref, ma_ref, mb_ref, g_ref, o_ref):
    x = x_ref[...] + (ma_ref[...] + mb_ref[...])
    o_ref[...] = (x * lax.rsqrt(jnp.mean(x * x, axis=-1, keepdims=True) + EPS)) * g_ref[...]


def final_norm(x, ma, mb, g, tm, row0):
    n, d = x.shape
    spec = pl.BlockSpec((tm, d), lambda i: (i, 0))
    mspec = pl.BlockSpec((tm, d), lambda i: (i + row0 // tm, 0))
    return pl.pallas_call(
        _final_body,
        grid=(n // tm,),
        in_specs=[spec, mspec, mspec, pl.BlockSpec((1, d), lambda i: (0, 0))],
        out_specs=spec,
        out_shape=jax.ShapeDtypeStruct((n, d), F32),
        compiler_params=_cparams(("parallel",)),
        name="final_norm",
    )(x, ma, mb, g.reshape(1, d))


def _tile(n, pref):
    t = min(n, pref)
    while n % t:
        t //= 2
    return t


def _rope_table(pos, b):
    half = ROPE_DIM // 2
    inv = ROPE_BASE ** (-jnp.arange(half, dtype=F32) / half)
    ang = pos.astype(F32)[:, None] * inv[None, :]
    cos, sin = jnp.cos(ang), jnp.sin(ang)
    cs = jnp.concatenate([cos, cos, sin, sin], axis=-1)
    return jnp.tile(cs, (b, 1))


def _rot_cols(w):
    half = ROPE_DIM // 2
    return jnp.concatenate([-w[..., half:], w[..., :half]], axis=-1)


def _moe(h_bf, eid, wgt, w_gate, w_up, w_down, bm):
    n, d = h_bf.shape
    n_exp = w_gate.shape[0]
    a = n * TOP_K
    e_flat = eid.reshape(-1)
    w_flat = wgt.reshape(-1)
    tok = jnp.repeat(jnp.arange(n, dtype=jnp.int32), TOP_K)
    order = jnp.argsort(e_flat)
    e_s, tok_s, w_s = e_flat[order], tok[order], w_flat[order]
    counts = jnp.zeros((n_exp,), jnp.int32).at[e_flat].add(1)
    start = jnp.cumsum(counts) - counts
    padded = (counts + bm - 1) // bm * bm
    pend = jnp.cumsum(padded)
    pstart = pend - padded
    dest = pstart[e_s] + (jnp.arange(a, dtype=jnp.int32) - start[e_s])
    n_blocks = (a + n_exp * (bm - 1) + bm - 1) // bm
    n_slots = n_blocks * bm
    slot_tok = jnp.full((n_slots,), n, jnp.int32).at[dest].set(tok_s)
    slot_w = jnp.zeros((n_slots,), F32).at[dest].set(w_s)
    block_e = jnp.minimum(jnp.searchsorted(pend, jnp.arange(n_blocks, dtype=jnp.int32) * bm, side='right'),
                          n_exp - 1).astype(jnp.int32)
    first = jnp.concatenate([jnp.ones((1,), jnp.int32), (block_e[1:] != block_e[:-1]).astype(jnp.int32)])
    nused = (pend[-1] // bm).astype(jnp.int32).reshape(1)
    x_pad = jnp.concatenate([h_bf, jnp.zeros((1, d), h_bf.dtype)], axis=0)
    x_sorted = x_pad[slot_tok]
    out = experts(x_sorted, slot_w.reshape(n_slots, 1), block_e, first, nused, w_gate, w_up, w_down, bm)
    slot_of = jnp.zeros((a,), jnp.int32).at[order].set(dest).reshape(n, TOP_K)
    return out[slot_of[:, 0]], out[slot_of[:, 1]]


def _mixer(x3, pos, ckv_past, kr_past, s0, conv0, wp):
    b, s, d = x3.shape
    n = b * s
    n_heads = wp['n_heads']
    hd = HEAD_DIM
    x = x3.reshape(n, d)
    cols = wp['cols']

    u = rms_matmul(x, wp['attn_norm_g'], wp['w_all'], _tile(n, 1024), 512)

    cs = _rope_table(pos, b)
    scale = (hd + ROPE_DIM) ** -0.5
    ckv, kr_pad, q, k, v = mla_prep(u, cs, wp['q_norm_g'], wp['kv_norm_g'], wp['wq_ext'], wp['wkv'], b, s,
                                    _tile(s, 256), cols['q_lat'] // 512, cols['kv_lat'] // 512,
                                    cols['small'] // 512, n_heads, scale)
    kr = kr_pad[:, :ROPE_DIM]
    lora = ckv.shape[1]
    if ckv_past is None:
        o_a = flash_prompt(q, k, v, _tile(s, ATTN_TILE), ATTN_HEADS_PER_STEP).reshape(n, n_heads * hd)
    else:
        qn = q[..., :hd].transpose(1, 0, 2, 3).reshape(n_heads, n, hd)
        q_abs = head_matmul(qn, wp['w_uk_t'])
        q_abs = q_abs.reshape(n_heads, b, s, lora).transpose(1, 2, 0, 3).reshape(b, s * n_heads, lora)
        qr = q[..., hd:].transpose(0, 2, 1, 3).reshape(b, s * n_heads, hd)
        kr_past_pad = jnp.pad(kr_past, ((0, 0), (0, 0), (0, LANES - ROPE_DIM)))
        ckv_new = jnp.pad(ckv.reshape(b, s, lora), ((0, 0), (0, LANES - s), (0, 0)))
        kr_new = jnp.pad(kr_pad.reshape(b, s, LANES), ((0, 0), (0, LANES - s), (0, 0)))
        o_lat = attn_sample(q_abs, qr, ckv_past, kr_past_pad, ckv_new, kr_new, s, n_heads)
        o_lat = o_lat.reshape(b, s, n_heads, lora).transpose(2, 0, 1, 3).reshape(n_heads, n, lora)
        o_h = head_matmul(o_lat, wp['w_uv_h'])
        o_a = o_h.reshape(n_heads, n, hd).transpose(1, 0, 2).reshape(n, n_heads * hd)

    qkv_w = 3 * n_heads * hd
    prev0 = jnp.pad(conv0, ((0, 0), (SUBLANES - (CONV_W - 1), 0), (0, 0)))
    o_b, s_new = gdn(u, prev0, s0, wp['conv_w'], wp['a_log'], wp['dt_bias'], wp['gdn_norm_g'], b, s,
                     cols['z'] // (n_heads * hd), cols['small'] // 512, n_heads)
    tail = u.reshape(b, s, -1)[:, max(s - (CONV_W - 1), 0):, :qkv_w]
    conv_new = jnp.concatenate([conv0, tail], axis=1)[:, -(CONV_W - 1):]

    x1 = merge_out(u, o_a, o_b, x, wp['w_out'], _tile(n, 512), 512, cols['gate_a'] // d, cols['gate_b'] // d)
    return x1, ckv.reshape(b, s, lora), kr.reshape(b, s, ROPE_DIM), s_new, conv_new


def _prep_weights(l, attn_norm_g, w_in, q_norm_g, w_uq, kv_norm_g, w_uk, w_uv, conv_w, a_log, dt_bias, gdn_norm_g,
                  w_out, ffn_norm_g, w_group, b_group, w_router, b_router, w_gate, w_up, w_down):
    d = w_in.shape[1]
    q_lora, n_heads, _ = w_uq.shape[1:]
    kv_lora = w_uk.shape[1]
    hd = HEAD_DIM
    qkv_w = 3 * n_heads * hd
    sizes = (q_lora, kv_lora, ROPE_DIM, qkv_w, n_heads, n_heads, n_heads * hd, d, d)
    offs = np.concatenate([[0], np.cumsum(sizes)])
    wi = w_in[l]
    part = lambda i: wi[:, offs[i]:offs[i + 1]]
    w_qlat, w_kvlat, w_kr, w_qkv, w_a, w_b, w_z, w_ga, w_gb = [part(i) for i in range(9)]
    zpad = lambda c: jnp.zeros((d, c), F32)
    small = jnp.concatenate([w_kr, _rot_cols(w_kr), w_a, zpad(LANES - n_heads), w_b, zpad(LANES - n_heads),
                             zpad(LANES)], axis=1)
    w_all = jnp.concatenate([w_qkv, w_z, w_ga, w_gb, w_qlat, w_kvlat, small], axis=1).astype(BF16)
    cols = {'qkv': 0, 'z': qkv_w, 'gate_a': qkv_w + n_heads * hd, 'gate_b': qkv_w + n_heads * hd + d,
            'q_lat': qkv_w + n_heads * hd + 2 * d}
    cols['kv_lat'] = cols['q_lat'] + q_lora
    cols['small'] = cols['kv_lat'] + kv_lora
    wq = w_uq[l]
    wq_nope = wq[:, :, :hd].reshape(q_lora, n_heads * hd)
    wq_r = wq[:, :, hd:]
    wq_rope = jnp.concatenate([wq_r, _rot_cols(wq_r)], axis=-1).reshape(q_lora, n_heads * hd)
    wq_ext = jnp.concatenate([wq_nope, wq_rope], axis=1).astype(BF16)
    wkv = jnp.concatenate([w_uk[l].reshape(kv_lora, n_heads * hd), w_uv[l].reshape(kv_lora, n_heads * hd)],
                          axis=1).astype(BF16)
    wr = jnp.concatenate([w_group[l], w_router[l].transpose(1, 0, 2).reshape(d, -1)], axis=1)
    wr = jnp.pad(wr, ((0, 0), (0, LANES - wr.shape[1])))
    wr_hi = wr.astype(BF16)
    wr_lo = (wr - wr_hi.astype(F32)).astype(BF16)
    br = jnp.concatenate([b_group[l], b_router[l].reshape(-1)])
    br = jnp.pad(br, (0, LANES - br.shape[0])).reshape(1, LANES)
    return {
        'n_heads': n_heads, 'cols': cols, 'w_all': w_all, 'attn_norm_g': attn_norm_g[l],
        'q_norm_g': q_norm_g[l], 'kv_norm_g': kv_norm_g[l], 'wq_ext': wq_ext, 'wkv': wkv,
        'w_uk_t': w_uk[l].transpose(1, 2, 0).astype(BF16),
        'w_uv_h': w_uv[l].transpose(1, 0, 2).astype(BF16),
        'conv_w': conv_w[l], 'a_log': a_log[l], 'dt_bias': dt_bias[l], 'gdn_norm_g': gdn_norm_g[l],
        'w_out': w_out[l].astype(BF16), 'ffn_norm_g': ffn_norm_g[l], 'wr_hi': wr_hi, 'wr_lo': wr_lo, 'br': br,
        'w_gate': w_gate[l], 'w_up': w_up[l], 'w_down': w_down[l],
    }


def kernel(x_prompt, x_sample, cache_ckv, cache_k_rope, state_gdn, state_conv, attn_norm_g, w_in, q_norm_g, w_uq, kv_norm_g, w_uk, w_uv, conv_w, a_log, dt_bias, gdn_norm_g, w_out, ffn_norm_g, w_group, b_group, w_router, b_router, w_gate, w_up, w_down, final_norm_g):
    depth = w_in.shape[0]
    assert depth == 1, "final RMSNorm is fused into the last layer; deeper trunks are not supported"
    b_p, s_p, d = x_prompt.shape
    b_s, s_s, _ = x_sample.shape
    past = cache_ckv.shape[2]
    n_heads = w_uq.shape[2]
    qkv_w = 3 * n_heads * HEAD_DIM
    pos_p = jnp.arange(s_p, dtype=jnp.int32)
    pos_s = past + jnp.arange(s_s, dtype=jnp.int32)
    wp = _prep_weights(0, attn_norm_g, w_in, q_norm_g, w_uq, kv_norm_g, w_uk, w_uv, conv_w, a_log, dt_bias,
                       gdn_norm_g, w_out, ffn_norm_g, w_group, b_group, w_router, b_router, w_gate, w_up, w_down)
    s0 = jnp.zeros((b_p, n_heads, HEAD_DIM, HEAD_DIM), F32)
    c0 = jnp.zeros((b_p, CONV_W - 1, qkv_w), F32)
    x1_p, ckv_p, kr_p, sg_p, sc_p = _mixer(x_prompt, pos_p, None, None, s0, c0, wp)
    x1_s, ckv_s, kr_s, sg_s, sc_s = _mixer(x_sample, pos_s, cache_ckv[0], cache_k_rope[0], state_gdn[0],
                                           state_conv[0], wp)

    n_p, n_s = b_p * s_p, b_s * s_s
    routed = [route(x1, wp['ffn_norm_g'], wp['wr_hi'], wp['wr_lo'], wp['br'], _tile(x1.shape[0], 512))
              for x1 in (x1_p, x1_s)]
    h_bf, eid, wgt = [jnp.concatenate([routed[0][i], routed[1][i]], axis=0) for i in range(3)]
    ma, mb = _moe(h_bf, eid[:, :TOP_K], wgt[:, :TOP_K], wp['w_gate'], wp['w_up'], wp['w_down'], MOE_BLOCK)
    tm_s = _tile(int(np.gcd(n_p, n_s)), 512)
    yp = final_norm(x1_p, ma, mb, final_norm_g, _tile(n_p, 512), 0).reshape(b_p, s_p, d)
    ys = final_norm(x1_s, ma, mb, final_norm_g, tm_s, n_p).reshape(b_s, s_s, d)
    return (yp, ys, ckv_p[None], kr_p[None], sg_p[None], sc_p[None], ckv_s[None], kr_s[None], sg_s[None], sc_s[None])
```

```python
import functools

import numpy as np
import jax
import jax.numpy as jnp
from jax import lax
from jax.experimental import pallas as pl
from jax.experimental.pallas import tpu as pltpu

F32 = jnp.float32
BF16 = jnp.bfloat16

EPS = 1e-6
CHUNK = 64
ROPE_BASE = 10000.0
HEAD_DIM = 128
ROPE_DIM = 64
CONV_W = 4
TOP_K = 2
N_GROUPS = 8
EXPERTS_PER_GROUP = 8

LANES = 128
SUBLANES = 8
VMEM_LIMIT = 56 * 1024 * 1024
ATTN_TILE = 512
ATTN_HEADS_PER_STEP = 2
MOE_BLOCK = 256


def _cparams(sem):
    return pltpu.CompilerParams(dimension_semantics=sem, vmem_limit_bytes=VMEM_LIMIT)


def _dot(a, b):
    return jnp.dot(a, b, preferred_element_type=F32)


def _dot_nt(a, b):
    return lax.dot_general(a, b, (((1,), (1,)), ((), ())), preferred_element_type=F32)


def _dot_tn(a, b):
    return lax.dot_general(a, b, (((0,), (0,)), ((), ())), preferred_element_type=F32)


def _sigmoid(x):
    return 1.0 / (1.0 + jnp.exp(-x))


def _split_bf16(a):
    hi = a.astype(BF16)
    lo = (a - hi.astype(F32)).astype(BF16)
    return hi, lo


def _dot3(a_hi, a_lo, b_hi, b_lo):
    return _dot(a_hi, b_hi) + (_dot(a_hi, b_lo) + _dot(a_lo, b_hi))


def _rms_mm_body(x_ref, g_ref, w_ref, o_ref, h_ref):
    @pl.when(pl.program_id(1) == 0)
    def _():
        x = x_ref[...]
        ms = jnp.mean(x * x, axis=-1, keepdims=True)
        h_ref[...] = ((x * lax.rsqrt(ms + EPS)) * g_ref[...]).astype(BF16)

    o_ref[...] = _dot(h_ref[...], w_ref[...])


def rms_matmul(x, g, w, tm, tn):
    n, k = x.shape
    m = w.shape[1]
    return pl.pallas_call(
        _rms_mm_body,
        grid=(n // tm, m // tn),
        in_specs=[pl.BlockSpec((tm, k), lambda i, j: (i, 0)),
                  pl.BlockSpec((1, k), lambda i, j: (0, 0)),
                  pl.BlockSpec((k, tn), lambda i, j: (0, j))],
        out_specs=pl.BlockSpec((tm, tn), lambda i, j: (i, j)),
        out_shape=jax.ShapeDtypeStruct((n, m), F32),
        scratch_shapes=[pltpu.VMEM((tm, k), BF16)],
        compiler_params=_cparams(("parallel", "arbitrary")),
        name="rms_matmul",
    )(x, g.reshape(1, k), w)


def _mla_prep_body(ql_ref, kvl_ref, sm_ref, cs_ref, gq_ref, gkv_ref, wq_ref, *rest, n_heads, scale, with_kv):
    if with_kv:
        wk_ref, wvt_ref, ckv_ref, kr_ref, q_ref, k_ref, vt_ref = rest
    else:
        ckv_ref, kr_ref, q_ref = rest
    hd = HEAD_DIM
    ql = ql_ref[...]
    qn = ((ql * lax.rsqrt(jnp.mean(ql * ql, axis=-1, keepdims=True) + EPS)) * gq_ref[...]).astype(BF16)
    q_all = _dot(qn, wq_ref[...])
    kvl = kvl_ref[...]
    ckv = (kvl * lax.rsqrt(jnp.mean(kvl * kvl, axis=-1, keepdims=True) + EPS)) * gkv_ref[...]
    ckv_ref[...] = ckv
    if with_kv:
        ckv_b = ckv.astype(BF16)
        k_all = _dot(ckv_b, wk_ref[...])
        vt_all = _dot_nt(wvt_ref[...], ckv_b)

    cs = cs_ref[...]
    lane = lax.broadcasted_iota(jnp.int32, cs.shape, 1)

    def rope(t):
        t = t * cs
        return jnp.where(lane < ROPE_DIM, t + pltpu.roll(t, ROPE_DIM, axis=1), 0.0)

    kr = rope(sm_ref[:, 0:LANES])
    kr_ref[...] = kr
    kr_b = kr.astype(BF16)
    off = n_heads * hd
    for h in range(n_heads):
        q_ref[0, h, :, 0:hd] = (q_all[:, h * hd:(h + 1) * hd] * scale).astype(BF16)
        q_ref[0, h, :, hd:2 * hd] = (rope(q_all[:, off + h * hd:off + (h + 1) * hd]) * scale).astype(BF16)
        if with_kv:
            k_ref[0, h, :, 0:hd] = k_all[:, h * hd:(h + 1) * hd].astype(BF16)
            k_ref[0, h, :, hd:2 * hd] = kr_b
            vt_ref[0, h, 0] = vt_all[h * hd:(h + 1) * hd, :].astype(BF16)


def mla_prep(u, cs, gq, gkv, wq, wk, wvt, b, s, tm, t_att, col_ql, col_kvl, col_sm, n_heads, scale):
    lora = wq.shape[0]
    nt = s // tm
    row = lambda bi, si: bi * nt + si
    hd = HEAD_DIM
    with_kv = wk is not None
    full = lambda bi, si: (0, 0)
    in_specs = [pl.BlockSpec((tm, lora), lambda bi, si: (row(bi, si), col_ql)),
                pl.BlockSpec((tm, lora), lambda bi, si: (row(bi, si), col_kvl)),
                pl.BlockSpec((tm, 512), lambda bi, si: (row(bi, si), col_sm)),
                pl.BlockSpec((tm, LANES), lambda bi, si: (row(bi, si), 0)),
                pl.BlockSpec((1, lora), full),
                pl.BlockSpec((1, lora), full),
                pl.BlockSpec(wq.shape, full)]
    out_specs = [pl.BlockSpec((tm, lora), lambda bi, si: (row(bi, si), 0)),
                 pl.BlockSpec((tm, LANES), lambda bi, si: (row(bi, si), 0)),
                 pl.BlockSpec((1, n_heads, tm, 2 * hd), lambda bi, si: (bi, 0, si, 0))]
    out_shape = [jax.ShapeDtypeStruct((b * s, lora), F32),
                 jax.ShapeDtypeStruct((b * s, LANES), F32),
                 jax.ShapeDtypeStruct((b, n_heads, s, 2 * hd), BF16)]
    args = [u, u, u, cs, gq.reshape(1, lora), gkv.reshape(1, lora), wq]
    if with_kv:
        r = t_att // tm
        in_specs += [pl.BlockSpec(wk.shape, full), pl.BlockSpec(wvt.shape, full)]
        out_specs += [pl.BlockSpec((1, n_heads, tm, 2 * hd), lambda bi, si: (bi, 0, si, 0)),
                      pl.BlockSpec((1, n_heads, 1, hd, tm), lambda bi, si: (bi, 0, si // r, 0, si % r))]
        out_shape += [jax.ShapeDtypeStruct((b, n_heads, s, 2 * hd), BF16),
                      jax.ShapeDtypeStruct((b, n_heads, s // t_att, hd, t_att), BF16)]
        args += [wk, wvt]
    return pl.pallas_call(
        functools.partial(_mla_prep_body, n_heads=n_heads, scale=scale, with_kv=with_kv),
        grid=(b, nt),
        in_specs=in_specs,
        out_specs=out_specs,
        out_shape=out_shape,
        compiler_params=_cparams(("parallel", "parallel")),
        name="mla_prep",
    )(*args)


def _flash_body(q_ref, k_ref, vt_ref, o_ref, m_scr, l_scr, acc_scr, *, t, gh):
    qi = pl.program_id(2)
    hd = HEAD_DIM
    m_scr[...] = jnp.full(m_scr.shape, -jnp.inf, F32)
    l_scr[...] = jnp.zeros(l_scr.shape, F32)
    acc_scr[...] = jnp.zeros(acc_scr.shape, F32)

    def block(j, masked):
        start = pl.multiple_of(j * t, t)
        for g in range(gh):
            k = k_ref[0, g, pl.ds(start, t), :]
            sc = _dot_nt(k, q_ref[0, g])
            if masked:
                krow = lax.broadcasted_iota(jnp.int32, sc.shape, 0)
                qcol = lax.broadcasted_iota(jnp.int32, sc.shape, 1)
                sc = jnp.where(krow // CHUNK <= qcol // CHUNK, sc, -jnp.inf)
            m_prev = m_scr[g]
            m_new = jnp.maximum(m_prev, jnp.max(sc, axis=0, keepdims=True))
            alpha = jnp.exp(m_prev - m_new)
            p = jnp.exp(sc - m_new)
            l_scr[g] = alpha * l_scr[g] + jnp.sum(p, axis=0, keepdims=True)
            acc_scr[g] = alpha * acc_scr[g] + _dot(vt_ref[0, g, j], p.astype(BF16))
            m_scr[g] = m_new

    def full_block(j, carry):
        block(j, False)
        return carry

    lax.fori_loop(0, qi, full_block, 0)
    block(qi, True)
    for g in range(gh):
        o_ref[0, :, g * hd:(g + 1) * hd] = jnp.transpose(acc_scr[g] / l_scr[g])


def flash_prompt(q, k, vt, t, gh):
    b, nh, s, dk = q.shape
    hd = vt.shape[-2]
    assert t % CHUNK == 0 and s % t == 0 and nh % gh == 0 and vt.shape[-1] == t
    return pl.pallas_call(
        functools.partial(_flash_body, t=t, gh=gh),
        grid=(b, nh // gh, s // t),
        in_specs=[pl.BlockSpec((1, gh, t, dk), lambda bi, hp, qi: (bi, hp, qi, 0)),
                  pl.BlockSpec((1, gh, s, dk), lambda bi, hp, qi: (bi, hp, 0, 0)),
                  pl.BlockSpec((1, gh, s // t, hd, t), lambda bi, hp, qi: (bi, hp, 0, 0, 0))],
        out_specs=pl.BlockSpec((1, t, gh * hd), lambda bi, hp, qi: (bi, qi, hp)),
        out_shape=jax.ShapeDtypeStruct((b, s, nh * hd), F32),
        scratch_shapes=[pltpu.VMEM((gh, 1, t), F32), pltpu.VMEM((gh, 1, t), F32), pltpu.VMEM((gh, hd, t), F32)],
        compiler_params=_cparams(("parallel", "parallel", "arbitrary")),
        name="flash_prompt",
    )(q, k, vt)


def _bmm_body(a_ref, b_ref, o_ref):
    o_ref[0] = _dot(a_ref[0].astype(BF16), b_ref[0])


def head_matmul(a, b):
    nh, m, k = a.shape
    n = b.shape[2]
    return pl.pallas_call(
        _bmm_body,
        grid=(nh,),
        in_specs=[pl.BlockSpec((1, m, k), lambda h: (h, 0, 0)),
                  pl.BlockSpec((1, k, n), lambda h: (h, 0, 0))],
        out_specs=pl.BlockSpec((1, m, n), lambda h: (h, 0, 0)),
        out_shape=jax.ShapeDtypeStruct((nh, m, n), F32),
        compiler_params=_cparams(("parallel",)),
        name="head_matmul",
    )(a, b)


def _attn_sample_body(qa_ref, qr_ref, kp_ref, krp_ref, kn_ref, krn_ref, o_ref, *, past, s_new, n_heads):
    qa = qa_ref[0].astype(BF16)
    qr = qr_ref[0].astype(BF16)
    kp = kp_ref[0].astype(BF16)
    krp = krp_ref[0].astype(BF16)
    kn = kn_ref[0].astype(BF16)
    krn = krn_ref[0].astype(BF16)
    s_past = _dot_nt(qa, kp) + _dot_nt(qr, krp)
    s_n = _dot_nt(qa, kn) + _dot_nt(qr, krn)
    row = lax.broadcasted_iota(jnp.int32, s_n.shape, 0)
    col = lax.broadcasted_iota(jnp.int32, s_n.shape, 1)
    qpos = past + row // n_heads
    kpos = past + col
    valid = (col < s_new) & (kpos // CHUNK <= qpos // CHUNK)
    s_n = jnp.where(valid, s_n, -jnp.inf)
    m = jnp.maximum(jnp.max(s_past, axis=-1, keepdims=True), jnp.max(s_n, axis=-1, keepdims=True))
    pp = jnp.exp(s_past - m)
    pn = jnp.exp(s_n - m)
    l = jnp.sum(pp, axis=-1, keepdims=True) + jnp.sum(pn, axis=-1, keepdims=True)
    o_ref[0] = (_dot(pp.astype(BF16), kp) + _dot(pn.astype(BF16), kn)) / l


def attn_sample(qa, qr, ckv_past, kr_past, ckv_new, kr_new, s_new, n_heads):
    b, r, lora = qa.shape
    past = ckv_past.shape[1]
    return pl.pallas_call(
        functools.partial(_attn_sample_body, past=past, s_new=s_new, n_heads=n_heads),
        grid=(b,),
        in_specs=[pl.BlockSpec((1, r, lora), lambda i: (i, 0, 0)),
                  pl.BlockSpec((1, r, LANES), lambda i: (i, 0, 0)),
                  pl.BlockSpec((1, past, lora), lambda i: (i, 0, 0)),
                  pl.BlockSpec((1, past, LANES), lambda i: (i, 0, 0)),
                  pl.BlockSpec((1, LANES, lora), lambda i: (i, 0, 0)),
                  pl.BlockSpec((1, LANES, LANES), lambda i: (i, 0, 0))],
        out_specs=pl.BlockSpec((1, r, lora), lambda i: (i, 0, 0)),
        out_shape=jax.ShapeDtypeStruct((b, r, lora), F32),
        compiler_params=_cparams(("parallel",)),
        name="attn_sample",
    )(qa, qr, ckv_past, kr_past, ckv_new, kr_new)


def _softplus(x):
    return jnp.maximum(x, 0.0) + jnp.log1p(jnp.exp(-jnp.abs(x)))


def _gdn_body(qkv_ref, z_ref, sm_ref, prev0_ref, s0_ref, convw_ref, alog_ref, dtb_ref, gn_ref,
              o_ref, sout_ref, s_scr, prev_scr, y_scr, *, L, n_heads):
    hd = HEAD_DIM
    heads = range(n_heads)
    c = pl.program_id(1)

    @pl.when(c == 0)
    def _():
        s_scr[...] = s0_ref[0]
        prev_scr[...] = prev0_ref[0]

    x = qkv_ref[...]
    xp = jnp.concatenate([prev_scr[...], x], axis=0)
    base = SUBLANES - (CONV_W - 1)
    y = xp[base:base + L] * convw_ref[0:1, :]
    for j in range(1, CONV_W):
        y = y + xp[base + j:base + j + L] * convw_ref[j:j + 1, :]
    y_scr[...] = y * _sigmoid(y)
    prev_scr[...] = x[L - SUBLANES:L]

    a = sm_ref[:, LANES:2 * LANES]
    bb = sm_ref[:, 2 * LANES:3 * LANES]
    g = -jnp.exp(alog_ref[...]) * _softplus(a + dtb_ref[...])
    beta = _sigmoid(bb)
    rowl = lax.broadcasted_iota(jnp.int32, g.shape, 0)
    cum = g
    d = 1
    while d < L:
        cum = cum + jnp.where(rowl >= d, pltpu.roll(cum, d, axis=0), 0.0)
        d *= 2
    cum_t = jnp.transpose(jnp.concatenate([cum, jnp.zeros((LANES - L, LANES), F32)], axis=0))

    row = lax.broadcasted_iota(jnp.int32, (L, L), 0)
    col = lax.broadcasted_iota(jnp.int32, (L, L), 1)
    tri = row >= col
    strict = row > col
    eye = jnp.where(row == col, 1.0, 0.0)
    gn = gn_ref[...]

    q_b, k_b, kb_b, rhs_b, qg_b, kd_b = [], [], [], [], [], []
    for h in heads:
        qh = y_scr[:, h * hd:(h + 1) * hd]
        kh = y_scr[:, (n_heads + h) * hd:(n_heads + h + 1) * hd]
        vh = y_scr[:, (2 * n_heads + h) * hd:(2 * n_heads + h + 1) * hd]
        qh = (qh * lax.rsqrt(jnp.sum(qh * qh, axis=-1, keepdims=True) + EPS)) * (hd ** -0.5)
        kh = kh * lax.rsqrt(jnp.sum(kh * kh, axis=-1, keepdims=True) + EPS)
        gcol = cum[:, h:h + 1]
        bcol = beta[:, h:h + 1]
        e_g = jnp.exp(gcol)
        kb = kh * bcol
        q_b.append(qh.astype(BF16))
        k_b.append(kh.astype(BF16))
        kb_b.append(kb.astype(BF16))
        rhs_b.append(jnp.concatenate([vh * bcol, kb * e_g], axis=1).astype(BF16))
        qg_b.append((qh * e_g).astype(BF16))
        kd_b.append((kh * jnp.exp(cum[L - 1:L, h:h + 1] - gcol)).astype(BF16))

    kk = [_dot_nt(kb_b[h], k_b[h]) for h in heads]
    qk = [_dot_nt(q_b[h], k_b[h]) for h in heads]

    nmat, attn_b = [], []
    for h in heads:
        diff = cum[:, h:h + 1] - cum_t[h:h + 1, 0:L]
        decay = jnp.where(tri, jnp.exp(jnp.where(tri, diff, 0.0)), 0.0)
        nmat.append(jnp.where(strict, kk[h] * decay, 0.0))
        attn_b.append(jnp.where(tri, qk[h] * decay, 0.0).astype(BF16))

    tmat = [eye - nmat[h] for h in heads]
    p_sp = [_split_bf16(nmat[h]) for h in heads]
    pmat = [_dot3(*p_sp[h], *p_sp[h]) for h in heads]
    power = 2
    while True:
        p_sp = [_split_bf16(pmat[h]) for h in heads]
        t_sp = [_split_bf16(tmat[h]) for h in heads]
        tmat = [tmat[h] + _dot3(*t_sp[h], *p_sp[h]) for h in heads]
        power *= 2
        if power >= L:
            break
        pmat = [_dot3(*p_sp[h], *p_sp[h]) for h in heads]

    uw = [_dot(tmat[h].astype(BF16), rhs_b[h]) for h in heads]
    s_old = [s_scr[h] for h in heads]
    s_b = [s_old[h].astype(BF16) for h in heads]
    ws = [_dot(jnp.concatenate([uw[h][:, hd:].astype(BF16), qg_b[h]], axis=0), s_b[h]) for h in heads]
    vn_b = [(uw[h][:, :hd] - ws[h][:L]).astype(BF16) for h in heads]
    av = [_dot(attn_b[h], vn_b[h]) for h in heads]
    ks = [_dot_tn(kd_b[h], vn_b[h]) for h in heads]
    for h in heads:
        sl = slice(h * hd, (h + 1) * hd)
        o = ws[h][L:] + av[h]
        s_scr[h] = s_old[h] * jnp.exp(cum[L - 1:L, h:h + 1]) + ks[h]
        zh = z_ref[:, sl]
        on = (o * lax.rsqrt(jnp.mean(o * o, axis=-1, keepdims=True) + EPS)) * gn
        o_ref[:, sl] = on * (zh * _sigmoid(zh))

    @pl.when(c == pl.num_programs(1) - 1)
    def _():
        sout_ref[0] = s_scr[...]


def gdn(u, prev0, s0, conv_w, a_log, dt_bias, gn, b, s, col_z, col_sm, n_heads):
    hd = HEAD_DIM
    L = min(s, CHUNK)
    nc = s // L
    qkv_w = 3 * n_heads * hd
    row = lambda bi, c: bi * nc + c
    alog = jnp.zeros((1, LANES), F32).at[0, :n_heads].set(a_log)
    dtb = jnp.zeros((1, LANES), F32).at[0, :n_heads].set(dt_bias)
    return pl.pallas_call(
        functools.partial(_gdn_body, L=L, n_heads=n_heads),
        grid=(b, nc),
        in_specs=[pl.BlockSpec((L, qkv_w), lambda bi, c: (row(bi, c), 0)),
                  pl.BlockSpec((L, n_heads * hd), lambda bi, c: (row(bi, c), col_z)),
                  pl.BlockSpec((L, 512), lambda bi, c: (row(bi, c), col_sm)),
                  pl.BlockSpec((1, SUBLANES, qkv_w), lambda bi, c: (bi, 0, 0)),
                  pl.BlockSpec((1, n_heads, hd, hd), lambda bi, c: (bi, 0, 0, 0)),
                  pl.BlockSpec((CONV_W, qkv_w), lambda bi, c: (0, 0)),
                  pl.BlockSpec((1, LANES), lambda bi, c: (0, 0)),
                  pl.BlockSpec((1, LANES), lambda bi, c: (0, 0)),
                  pl.BlockSpec((1, hd), lambda bi, c: (0, 0))],
        out_specs=[pl.BlockSpec((L, n_heads * hd), lambda bi, c: (row(bi, c), 0)),
                   pl.BlockSpec((1, n_heads, hd, hd), lambda bi, c: (bi, 0, 0, 0))],
        out_shape=[jax.ShapeDtypeStruct((b * s, n_heads * hd), F32),
                   jax.ShapeDtypeStruct((b, n_heads, hd, hd), F32)],
        scratch_shapes=[pltpu.VMEM((n_heads, hd, hd), F32),
                        pltpu.VMEM((SUBLANES, qkv_w), F32),
                        pltpu.VMEM((L, qkv_w), F32)],
        compiler_params=_cparams(("parallel", "arbitrary")),
        name="gdn",
    )(u, u, u, prev0, s0, conv_w, alog, dtb, gn.reshape(1, hd))


def _merge_body(ga_ref, gb_ref, oa_ref, ob_ref, x_ref, w_ref, o_ref, m_scr):
    @pl.when(pl.program_id(1) == 0)
    def _():
        m_scr[...] = (_sigmoid(ga_ref[...]) * oa_ref[...] + _sigmoid(gb_ref[...]) * ob_ref[...]).astype(BF16)

    o_ref[...] = x_ref[...] + _dot(m_scr[...], w_ref[...])


def merge_out(u, oa, ob, x, w, tm, tn, col_ga, col_gb):
    n, d = x.shape
    return pl.pallas_call(
        _merge_body,
        grid=(n // tm, d // tn),
        in_specs=[pl.BlockSpec((tm, d), lambda i, j: (i, col_ga)),
                  pl.BlockSpec((tm, d), lambda i, j: (i, col_gb)),
                  pl.BlockSpec((tm, d), lambda i, j: (i, 0)),
                  pl.BlockSpec((tm, d), lambda i, j: (i, 0)),
                  pl.BlockSpec((tm, tn), lambda i, j: (i, j)),
                  pl.BlockSpec((d, tn), lambda i, j: (0, j))],
        out_specs=pl.BlockSpec((tm, tn), lambda i, j: (i, j)),
        out_shape=jax.ShapeDtypeStruct((n, d), F32),
        scratch_shapes=[pltpu.VMEM((tm, d), BF16)],
        compiler_params=_cparams(("parallel", "arbitrary")),
        name="merge_out",
    )(u, u, oa, ob, x, w)


def _route_body(x_ref, g_ref, whi_ref, wlo_ref, bias_ref, h_ref, eid_ref, wgt_ref):
    x = x_ref[...]
    hn = (x * lax.rsqrt(jnp.mean(x * x, axis=-1, keepdims=True) + EPS)) * g_ref[...]
    h_hi, h_lo = _split_bf16(hn)
    h_ref[...] = h_hi
    lg = _dot3(h_hi, h_lo, whi_ref[...], wlo_ref[...]) + bias_ref[...]
    lane = lax.broadcasted_iota(jnp.int32, lg.shape, 1)
    big = jnp.int32(LANES)
    neg = -jnp.inf
    glog = jnp.where(lane < N_GROUPS, lg, neg)
    gmax = jnp.max(glog, axis=-1, keepdims=True)
    pg_top = 1.0 / jnp.sum(jnp.exp(glog - gmax), axis=-1, keepdims=True)
    grp = jnp.min(jnp.where(glog == gmax, lane, big), axis=-1, keepdims=True)
    lo = N_GROUPS + grp * EXPERTS_PER_GROUP
    le = jnp.where((lane >= lo) & (lane < lo + EXPERTS_PER_GROUP), lg, neg)
    v1 = jnp.max(le, axis=-1, keepdims=True)
    i1 = jnp.min(jnp.where(le == v1, lane, big), axis=-1, keepdims=True)
    le2 = jnp.where(lane == i1, neg, le)
    v2 = jnp.max(le2, axis=-1, keepdims=True)
    i2 = jnp.min(jnp.where(le2 == v2, lane, big), axis=-1, keepdims=True)
    e2 = jnp.exp(v2 - v1)
    den = 1.0 + e2
    w1 = pg_top * (1.0 / den)
    w2 = pg_top * (e2 / den)
    eid_ref[...] = jnp.where(lane == 0, i1 - N_GROUPS, jnp.where(lane == 1, i2 - N_GROUPS, 0))
    wgt_ref[...] = jnp.where(lane == 0, w1, jnp.where(lane == 1, w2, 0.0))


def route(x, g, whi, wlo, bias, tm):
    n, d = x.shape
    return pl.pallas_call(
        _route_body,
        grid=(n // tm,),
        in_specs=[pl.BlockSpec((tm, d), lambda i: (i, 0)),
                  pl.BlockSpec((1, d), lambda i: (0, 0)),
                  pl.BlockSpec((d, LANES), lambda i: (0, 0)),
                  pl.BlockSpec((d, LANES), lambda i: (0, 0)),
                  pl.BlockSpec((1, LANES), lambda i: (0, 0))],
        out_specs=[pl.BlockSpec((tm, d), lambda i: (i, 0)),
                   pl.BlockSpec((tm, LANES), lambda i: (i, 0)),
                   pl.BlockSpec((tm, LANES), lambda i: (i, 0))],
        out_shape=[jax.ShapeDtypeStruct((n, d), BF16),
                   jax.ShapeDtypeStruct((n, LANES), jnp.int32),
                   jax.ShapeDtypeStruct((n, LANES), F32)],
        compiler_params=_cparams(("parallel",)),
        name="route",
    )(x, g.reshape(1, d), whi, wlo, bias)


def _expert_body(be_tab, first_tab, nused, x_ref, sw_ref, wg_ref, wu_ref, wd_ref, o_ref, wg_s, wu_s, wd_s):
    blk = pl.program_id(0)

    @pl.when(blk < nused[0])
    def _():
        @pl.when(first_tab[blk] == 1)
        def _():
            wg_s[...] = wg_ref[0].astype(BF16)
            wu_s[...] = wu_ref[0].astype(BF16)
            wd_s[...] = wd_ref[0].astype(BF16)

        x = x_ref[...]
        gate = _dot(x, wg_s[...])
        up = _dot(x, wu_s[...])
        hid = (gate * _sigmoid(gate)) * up
        o_ref[...] = _dot(hid.astype(BF16), wd_s[...]) * sw_ref[...]

    @pl.when(blk >= nused[0])
    def _():
        o_ref[...] = jnp.zeros(o_ref.shape, F32)


def experts(x_sorted, slot_w, block_e, first, nused, w_gate, w_up, w_down, bm):
    n_slots, d = x_sorted.shape
    de = w_gate.shape[2]
    n_blocks = n_slots // bm
    grid_spec = pltpu.PrefetchScalarGridSpec(
        num_scalar_prefetch=3,
        grid=(n_blocks,),
        in_specs=[pl.BlockSpec((bm, d), lambda i, be, fi, nu: (i, 0)),
                  pl.BlockSpec((bm, 1), lambda i, be, fi, nu: (i, 0)),
                  pl.BlockSpec((1, d, de), lambda i, be, fi, nu: (be[i], 0, 0)),
                  pl.BlockSpec((1, d, de), lambda i, be, fi, nu: (be[i], 0, 0)),
                  pl.BlockSpec((1, de, d), lambda i, be, fi, nu: (be[i], 0, 0))],
        out_specs=pl.BlockSpec((bm, d), lambda i, be, fi, nu: (i, 0)),
        scratch_shapes=[pltpu.VMEM((d, de), BF16), pltpu.VMEM((d, de), BF16), pltpu.VMEM((de, d), BF16)],
    )
    return pl.pallas_call(
        _expert_body,
        grid_spec=grid_spec,
        out_shape=jax.ShapeDtypeStruct((n_slots, d), F32),
        compiler_params=_cparams(("arbitrary",)),
        name="experts",
    )(block_e, first, nused, x_sorted, slot_w, w_gate, w_up, w_down)


def _final_body(x_ref, ma_ref, mb_ref, g_ref, o_ref):
    x = x_ref[...] + (ma_ref[...] + mb_ref[...])
    o_ref[...] = (x * lax.rsqrt(jnp.mean(x * x, axis=-1, keepdims=True) + EPS)) * g_ref[...]


def final_norm(x, ma, mb, g, tm, row0):
    n, d = x.shape
    spec = pl.BlockSpec((tm, d), lambda i: (i, 0))
    mspec = pl.BlockSpec((tm, d), lambda i: (i + row0 // tm, 0))
    return pl.pallas_call(
        _final_body,
        grid=(n // tm,),
        in_specs=[spec, mspec, mspec, pl.BlockSpec((1, d), lambda i: (0, 0))],
        out_specs=spec,
        out_shape=jax.ShapeDtypeStruct((n, d), F32),
        compiler_params=_cparams(("parallel",)),
        name="final_norm",
    )(x, ma, mb, g.reshape(1, d))


def _tile(n, pref):
    t = min(n, pref)
    while n % t:
        t //= 2
    return t


def _rope_table(pos, b):
    half = ROPE_DIM // 2
    inv = ROPE_BASE ** (-jnp.arange(half, dtype=F32) / half)
    ang = pos.astype(F32)[:, None] * inv[None, :]
    cos, sin = jnp.cos(ang), jnp.sin(ang)
    cs = jnp.concatenate([cos, cos, sin, sin], axis=-1)
    return jnp.tile(cs, (b, 1))


def _rot_cols(w):
    half = ROPE_DIM // 2
    return jnp.concatenate([-w[..., half:], w[..., :half]], axis=-1)


def _moe(h_bf, eid, wgt, w_gate, w_up, w_down, bm):
    n, d = h_bf.shape
    n_exp = w_gate.shape[0]
    a = n * TOP_K
    e_flat = eid.reshape(-1)
    w_flat = wgt.reshape(-1)
    order = jnp.argsort(e_flat).astype(jnp.int32)
    e_s, w_s = e_flat[order], w_flat[order]
    tok_s = order // TOP_K
    start = jnp.searchsorted(e_s, jnp.arange(n_exp, dtype=jnp.int32), side='left').astype(jnp.int32)
    counts = jnp.concatenate([start[1:], jnp.full((1,), a, jnp.int32)]) - start
    padded = (counts + bm - 1) // bm * bm
    pend = jnp.cumsum(padded)
    pstart = pend - padded
    n_blocks = (a + n_exp * (bm - 1) + bm - 1) // bm
    n_slots = n_blocks * bm
    block_e = jnp.minimum(jnp.searchsorted(pend, jnp.arange(n_blocks, dtype=jnp.int32) * bm, side='right'),
                          n_exp - 1).astype(jnp.int32)
    first = jnp.concatenate([jnp.ones((1,), jnp.int32), (block_e[1:] != block_e[:-1]).astype(jnp.int32)])
    nused = (pend[-1] // bm).astype(jnp.int32).reshape(1)
    slot = jnp.arange(n_slots, dtype=jnp.int32)
    e_slot = jnp.repeat(block_e, bm)
    off = slot - pstart[e_slot]
    valid = (off >= 0) & (off < counts[e_slot])
    src = jnp.clip(start[e_slot] + off, 0, a - 1)
    slot_tok = jnp.where(valid, tok_s[src], slot % n)
    slot_w = jnp.where(valid, w_s[src], 0.0)
    x_sorted = h_bf[slot_tok]
    out = experts(x_sorted, slot_w.reshape(n_slots, 1), block_e, first, nused, w_gate, w_up, w_down, bm)
    rank = jnp.argsort(order).astype(jnp.int32)
    slot_of = (pstart[e_flat] + (rank - start[e_flat])).reshape(n, TOP_K)
    return out[slot_of[:, 0]], out[slot_of[:, 1]]


def _mixer(x3, pos, ckv_past, kr_past, s0, conv0, wp):
    b, s, d = x3.shape
    n = b * s
    n_heads = wp['n_heads']
    hd = HEAD_DIM
    x = x3.reshape(n, d)
    cols = wp['cols']

    u = rms_matmul(x, wp['attn_norm_g'], wp['w_all'], _tile(n, 1024), 512)

    cs = _rope_table(pos, b)
    scale = (hd + ROPE_DIM) ** -0.5
    prompt = ckv_past is None
    t_att = _tile(s, ATTN_TILE)
    tm = min(_tile(s, 256), t_att)
    res = mla_prep(u, cs, wp['q_norm_g'], wp['kv_norm_g'], wp['wq_ext'], wp['wk'] if prompt else None,
                   wp['wvt'] if prompt else None, b, s, tm, t_att, cols['q_lat'] // 512, cols['kv_lat'] // 512,
                   cols['small'] // 512, n_heads, scale)
    ckv, kr_pad, q = res[:3]
    kr = kr_pad[:, :ROPE_DIM]
    lora = ckv.shape[1]
    if prompt:
        o_a = flash_prompt(q, res[3], res[4], t_att, ATTN_HEADS_PER_STEP).reshape(n, n_heads * hd)
    else:
        qn = q[..., :hd].transpose(1, 0, 2, 3).reshape(n_heads, n, hd)
        q_abs = head_matmul(qn, wp['w_uk_t'])
        q_abs = q_abs.reshape(n_heads, b, s, lora).transpose(1, 2, 0, 3).reshape(b, s * n_heads, lora)
        qr = q[..., hd:].transpose(0, 2, 1, 3).reshape(b, s * n_heads, hd)
        kr_past_pad = jnp.pad(kr_past, ((0, 0), (0, 0), (0, LANES - ROPE_DIM)))
        ckv_new = jnp.pad(ckv.reshape(b, s, lora), ((0, 0), (0, LANES - s), (0, 0)))
        kr_new = jnp.pad(kr_pad.reshape(b, s, LANES), ((0, 0), (0, LANES - s), (0, 0)))
        o_lat = attn_sample(q_abs, qr, ckv_past, kr_past_pad, ckv_new, kr_new, s, n_heads)
        o_lat = o_lat.reshape(b, s, n_heads, lora).transpose(2, 0, 1, 3).reshape(n_heads, n, lora)
        o_h = head_matmul(o_lat, wp['w_uv_h'])
        o_a = o_h.reshape(n_heads, n, hd).transpose(1, 0, 2).reshape(n, n_heads * hd)

    qkv_w = 3 * n_heads * hd
    prev0 = jnp.pad(conv0, ((0, 0), (SUBLANES - (CONV_W - 1), 0), (0, 0)))
    o_b, s_new = gdn(u, prev0, s0, wp['conv_w'], wp['a_log'], wp['dt_bias'], wp['gdn_norm_g'], b, s,
                     cols['z'] // (n_heads * hd), cols['small'] // 512, n_heads)
    tail = u.reshape(b, s, -1)[:, max(s - (CONV_W - 1), 0):, :qkv_w]
    conv_new = jnp.concatenate([conv0, tail], axis=1)[:, -(CONV_W - 1):]

    x1 = merge_out(u, o_a, o_b, x, wp['w_out'], _tile(n, 512), 512, cols['gate_a'] // d, cols['gate_b'] // d)
    return x1, ckv.reshape(b, s, lora), kr.reshape(b, s, ROPE_DIM), s_new, conv_new


def _prep_weights(l, attn_norm_g, w_in, q_norm_g, w_uq, kv_norm_g, w_uk, w_uv, conv_w, a_log, dt_bias, gdn_norm_g,
                  w_out, ffn_norm_g, w_group, b_group, w_router, b_router, w_gate, w_up, w_down):
    d = w_in.shape[1]
    q_lora, n_heads, _ = w_uq.shape[1:]
    kv_lora = w_uk.shape[1]
    hd = HEAD_DIM
    qkv_w = 3 * n_heads * hd
    sizes = (q_lora, kv_lora, ROPE_DIM, qkv_w, n_heads, n_heads, n_heads * hd, d, d)
    offs = np.concatenate([[0], np.cumsum(sizes)])
    wi = w_in[l]
    part = lambda i: wi[:, offs[i]:offs[i + 1]]
    w_qlat, w_kvlat, w_kr, w_qkv, w_a, w_b, w_z, w_ga, w_gb = [part(i) for i in range(9)]
    zpad = lambda c: jnp.zeros((d, c), F32)
    small = jnp.concatenate([w_kr, _rot_cols(w_kr), w_a, zpad(LANES - n_heads), w_b, zpad(LANES - n_heads),
                             zpad(LANES)], axis=1)
    w_all = jnp.concatenate([w_qkv, w_z, w_ga, w_gb, w_qlat, w_kvlat, small], axis=1).astype(BF16)
    cols = {'qkv': 0, 'z': qkv_w, 'gate_a': qkv_w + n_heads * hd, 'gate_b': qkv_w + n_heads * hd + d,
            'q_lat': qkv_w + n_heads * hd + 2 * d}
    cols['kv_lat'] = cols['q_lat'] + q_lora
    cols['small'] = cols['kv_lat'] + kv_lora
    wq = w_uq[l]
    wq_nope = wq[:, :, :hd].reshape(q_lora, n_heads * hd)
    wq_r = wq[:, :, hd:]
    wq_rope = jnp.concatenate([wq_r, _rot_cols(wq_r)], axis=-1).reshape(q_lora, n_heads * hd)
    wq_ext = jnp.concatenate([wq_nope, wq_rope], axis=1).astype(BF16)
    wk = w_uk[l].reshape(kv_lora, n_heads * hd).astype(BF16)
    wvt = w_uv[l].reshape(kv_lora, n_heads * hd).T.astype(BF16)
    wr = jnp.concatenate([w_group[l], w_router[l].transpose(1, 0, 2).reshape(d, -1)], axis=1)
    wr = jnp.pad(wr, ((0, 0), (0, LANES - wr.shape[1])))
    wr_hi = wr.astype(BF16)
    wr_lo = (wr - wr_hi.astype(F32)).astype(BF16)
    br = jnp.concatenate([b_group[l], b_router[l].reshape(-1)])
    br = jnp.pad(br, (0, LANES - br.shape[0])).reshape(1, LANES)
    return {
        'n_heads': n_heads, 'cols': cols, 'w_all': w_all, 'attn_norm_g': attn_norm_g[l],
        'q_norm_g': q_norm_g[l], 'kv_norm_g': kv_norm_g[l], 'wq_ext': wq_ext, 'wk': wk, 'wvt': wvt,
        'w_uk_t': w_uk[l].transpose(1, 2, 0).astype(BF16),
        'w_uv_h': w_uv[l].transpose(1, 0, 2).astype(BF16),
        'conv_w': conv_w[l], 'a_log': a_log[l], 'dt_bias': dt_bias[l], 'gdn_norm_g': gdn_norm_g[l],
        'w_out': w_out[l].astype(BF16), 'ffn_norm_g': ffn_norm_g[l], 'wr_hi': wr_hi, 'wr_lo': wr_lo, 'br': br,
        'w_gate': w_gate[l], 'w_up': w_up[l], 'w_down': w_down[l],
    }


def kernel(x_prompt, x_sample, cache_ckv, cache_k_rope, state_gdn, state_conv, attn_norm_g, w_in, q_norm_g, w_uq, kv_norm_g, w_uk, w_uv, conv_w, a_log, dt_bias, gdn_norm_g, w_out, ffn_norm_g, w_group, b_group, w_router, b_router, w_gate, w_up, w_down, final_norm_g):
    depth = w_in.shape[0]
    assert depth == 1, "final RMSNorm is fused into the last layer; deeper trunks are not supported"
    b_p, s_p, d = x_prompt.shape
    b_s, s_s, _ = x_sample.shape
    past = cache_ckv.shape[2]
    n_heads = w_uq.shape[2]
    qkv_w = 3 * n_heads * HEAD_DIM
    pos_p = jnp.arange(s_p, dtype=jnp.int32)
    pos_s = past + jnp.arange(s_s, dtype=jnp.int32)
    wp = _prep_weights(0, attn_norm_g, w_in, q_norm_g, w_uq, kv_norm_g, w_uk, w_uv, conv_w, a_log, dt_bias,
                       gdn_norm_g, w_out, ffn_norm_g, w_group, b_group, w_router, b_router, w_gate, w_up, w_down)
    s0 = jnp.zeros((b_p, n_heads, HEAD_DIM, HEAD_DIM), F32)
    c0 = jnp.zeros((b_p, CONV_W - 1, qkv_w), F32)
    x1_p, ckv_p, kr_p, sg_p, sc_p = _mixer(x_prompt, pos_p, None, None, s0, c0, wp)
    x1_s, ckv_s, kr_s, sg_s, sc_s = _mixer(x_sample, pos_s, cache_ckv[0], cache_k_rope[0], state_gdn[0],
                                           state_conv[0], wp)

    n_p, n_s = b_p * s_p, b_s * s_s
    routed = [route(x1, wp['ffn_norm_g'], wp['wr_hi'], wp['wr_lo'], wp['br'], _tile(x1.shape[0], 512))
              for x1 in (x1_p, x1_s)]
    h_bf, eid, wgt = [jnp.concatenate([routed[0][i], routed[1][i]], axis=0) for i in range(3)]
    ma, mb = _moe(h_bf, eid[:, :TOP_K], wgt[:, :TOP_K], wp['w_gate'], wp['w_up'], wp['w_down'], MOE_BLOCK)
    tm_s = _tile(int(np.gcd(n_p, n_s)), 512)
    yp = final_norm(x1_p, ma, mb, final_norm_g, _tile(n_p, 512), 0).reshape(b_p, s_p, d)
    ys = final_norm(x1_s, ma, mb, final_norm_g, tm_s, n_p).reshape(b_s, s_s, d)
    return (yp, ys, ckv_p[None], kr_p[None], sg_p[None], sc_p[None], ckv_s[None], kr_s[None], sg_s[None], sc_s[None])
```

```python
import functools

import numpy as np
import jax
import jax.numpy as jnp
from jax import lax
from jax.experimental import pallas as pl
from jax.experimental.pallas import tpu as pltpu

F32 = jnp.float32
BF16 = jnp.bfloat16

EPS = 1e-6
CHUNK = 64
ROPE_BASE = 10000.0
HEAD_DIM = 128
ROPE_DIM = 64
CONV_W = 4
TOP_K = 2
N_GROUPS = 8
EXPERTS_PER_GROUP = 8

LANES = 128
SUBLANES = 8
BF16_ROWS = 16
VMEM_LIMIT = 56 * 1024 * 1024
ATTN_TILE = 512
ATTN_HEADS_PER_STEP = 4
MOE_BLOCK = 256


def _cparams(sem):
    return pltpu.CompilerParams(dimension_semantics=sem, vmem_limit_bytes=VMEM_LIMIT)


def _dot(a, b):
    return jnp.dot(a, b, preferred_element_type=F32)


def _dot_nt(a, b):
    return lax.dot_general(a, b, (((1,), (1,)), ((), ())), preferred_element_type=F32)


def _dot_tn(a, b):
    return lax.dot_general(a, b, (((0,), (0,)), ((), ())), preferred_element_type=F32)


def _sigmoid(x):
    return 1.0 / (1.0 + jnp.exp(-x))


def _split_bf16(a):
    hi = a.astype(BF16)
    lo = (a - hi.astype(F32)).astype(BF16)
    return hi, lo


def _dot3(a_hi, a_lo, b_hi, b_lo):
    return _dot(a_hi, b_hi) + (_dot(a_hi, b_lo) + _dot(a_lo, b_hi))


def _rms_mm_body(x_ref, g_ref, w_ref, o_ref, h_ref):
    @pl.when(pl.program_id(1) == 0)
    def _():
        x = x_ref[...]
        ms = jnp.mean(x * x, axis=-1, keepdims=True)
        h_ref[...] = ((x * lax.rsqrt(ms + EPS)) * g_ref[...]).astype(BF16)

    o_ref[...] = _dot(h_ref[...], w_ref[...])


def rms_matmul(x, g, w, tm, tn):
    n, k = x.shape
    m = w.shape[1]
    return pl.pallas_call(
        _rms_mm_body,
        grid=(n // tm, m // tn),
        in_specs=[pl.BlockSpec((tm, k), lambda i, j: (i, 0)),
                  pl.BlockSpec((1, k), lambda i, j: (0, 0)),
                  pl.BlockSpec((k, tn), lambda i, j: (0, j))],
        out_specs=pl.BlockSpec((tm, tn), lambda i, j: (i, j)),
        out_shape=jax.ShapeDtypeStruct((n, m), F32),
        scratch_shapes=[pltpu.VMEM((tm, k), BF16)],
        compiler_params=_cparams(("parallel", "arbitrary")),
        name="rms_matmul",
    )(x, g.reshape(1, k), w)


def _mla_prep_body(ql_ref, kvl_ref, sm_ref, cs_ref, gq_ref, gkv_ref, wq_ref, *rest, n_heads, scale, with_kv):
    if with_kv:
        wk_ref, wvt_ref, ckv_ref, kr_ref, q_ref, k_ref, vt_ref = rest
    else:
        ckv_ref, kr_ref, q_ref = rest
    hd = HEAD_DIM
    ql = ql_ref[...]
    qn = ((ql * lax.rsqrt(jnp.mean(ql * ql, axis=-1, keepdims=True) + EPS)) * gq_ref[...]).astype(BF16)
    q_all = _dot(qn, wq_ref[...])
    kvl = kvl_ref[...]
    ckv = (kvl * lax.rsqrt(jnp.mean(kvl * kvl, axis=-1, keepdims=True) + EPS)) * gkv_ref[...]
    ckv_ref[...] = ckv
    if with_kv:
        ckv_b = ckv.astype(BF16)
        k_all = _dot(ckv_b, wk_ref[...])
        vt_all = _dot_nt(wvt_ref[...], ckv_b)

    cs = cs_ref[...]
    lane = lax.broadcasted_iota(jnp.int32, cs.shape, 1)

    def rope(t):
        t = t * cs
        return jnp.where(lane < ROPE_DIM, t + pltpu.roll(t, ROPE_DIM, axis=1), 0.0)

    kr = rope(sm_ref[:, 0:LANES])
    kr_ref[...] = kr
    kr_b = kr.astype(BF16)
    off = n_heads * hd
    for h in range(n_heads):
        q_ref[0, h, :, 0:hd] = (q_all[:, h * hd:(h + 1) * hd] * scale).astype(BF16)
        q_ref[0, h, :, hd:2 * hd] = (rope(q_all[:, off + h * hd:off + (h + 1) * hd]) * scale).astype(BF16)
        if with_kv:
            k_ref[0, h, :, 0:hd] = k_all[:, h * hd:(h + 1) * hd].astype(BF16)
            k_ref[0, h, :, hd:2 * hd] = kr_b
            vt_ref[0, h, 0, 0:hd, :] = vt_all[h * hd:(h + 1) * hd, :].astype(BF16)
            vt_ref[0, h, 0, hd:hd + BF16_ROWS, :] = jnp.ones((BF16_ROWS, vt_all.shape[1]), BF16)


def mla_prep(u, cs, gq, gkv, wq, wk, wvt, b, s, tm, t_att, col_ql, col_kvl, col_sm, n_heads, scale):
    lora = wq.shape[0]
    nt = s // tm
    row = lambda bi, si: bi * nt + si
    hd = HEAD_DIM
    with_kv = wk is not None
    full = lambda bi, si: (0, 0)
    in_specs = [pl.BlockSpec((tm, lora), lambda bi, si: (row(bi, si), col_ql)),
                pl.BlockSpec((tm, lora), lambda bi, si: (row(bi, si), col_kvl)),
                pl.BlockSpec((tm, 512), lambda bi, si: (row(bi, si), col_sm)),
                pl.BlockSpec((tm, LANES), lambda bi, si: (row(bi, si), 0)),
                pl.BlockSpec((1, lora), full),
                pl.BlockSpec((1, lora), full),
                pl.BlockSpec(wq.shape, full)]
    out_specs = [pl.BlockSpec((tm, lora), lambda bi, si: (row(bi, si), 0)),
                 pl.BlockSpec((tm, LANES), lambda bi, si: (row(bi, si), 0)),
                 pl.BlockSpec((1, n_heads, tm, 2 * hd), lambda bi, si: (bi, 0, si, 0))]
    out_shape = [jax.ShapeDtypeStruct((b * s, lora), F32),
                 jax.ShapeDtypeStruct((b * s, LANES), F32),
                 jax.ShapeDtypeStruct((b, n_heads, s, 2 * hd), BF16)]
    args = [u, u, u, cs, gq.reshape(1, lora), gkv.reshape(1, lora), wq]
    if with_kv:
        r = t_att // tm
        in_specs += [pl.BlockSpec(wk.shape, full), pl.BlockSpec(wvt.shape, full)]
        out_specs += [pl.BlockSpec((1, n_heads, tm, 2 * hd), lambda bi, si: (bi, 0, si, 0)),
                      pl.BlockSpec((1, n_heads, 1, hd + BF16_ROWS, tm),
                                   lambda bi, si: (bi, 0, si // r, 0, si % r))]
        out_shape += [jax.ShapeDtypeStruct((b, n_heads, s, 2 * hd), BF16),
                      jax.ShapeDtypeStruct((b, n_heads, s // t_att, hd + BF16_ROWS, t_att), BF16)]
        args += [wk, wvt]
    return pl.pallas_call(
        functools.partial(_mla_prep_body, n_heads=n_heads, scale=scale, with_kv=with_kv),
        grid=(b, nt),
        in_specs=in_specs,
        out_specs=out_specs,
        out_shape=out_shape,
        compiler_params=_cparams(("parallel", "parallel")),
        name="mla_prep",
    )(*args)


def _flash_body(q_ref, k_ref, vt_ref, o_ref, m_scr, acc_scr, *, t, gh):
    qi = pl.program_id(2)
    hd = HEAD_DIM
    m_scr[...] = jnp.full(m_scr.shape, -jnp.inf, F32)
    acc_scr[...] = jnp.zeros(acc_scr.shape, F32)

    def block(j, masked):
        start = pl.multiple_of(j * t, t)
        scs = [_dot_nt(k_ref[0, g, pl.ds(start, t), :], q_ref[0, g]) for g in range(gh)]
        for g in range(gh):
            sc = scs[g]
            if masked:
                krow = lax.broadcasted_iota(jnp.int32, sc.shape, 0)
                qcol = lax.broadcasted_iota(jnp.int32, sc.shape, 1)
                sc = jnp.where(krow // CHUNK <= qcol // CHUNK, sc, -jnp.inf)
            m_prev = m_scr[g]
            m_new = jnp.maximum(m_prev, jnp.max(sc, axis=0, keepdims=True))
            alpha = jnp.exp2(m_prev - m_new)
            p = jnp.exp2(sc - m_new)
            acc_scr[g] = alpha * acc_scr[g] + _dot(vt_ref[0, g, j], p.astype(BF16))
            m_scr[g] = m_new

    def full_block(j, carry):
        block(j, False)
        return carry

    lax.fori_loop(0, qi, full_block, 0)
    block(qi, True)
    for g in range(gh):
        acc = acc_scr[g]
        o_ref[0, :, g * hd:(g + 1) * hd] = jnp.transpose(acc[:hd] / acc[hd:hd + 1])


def flash_prompt(q, k, vt, t, gh):
    b, nh, s, dk = q.shape
    hd = HEAD_DIM
    hv = vt.shape[-2]
    assert t % CHUNK == 0 and s % t == 0 and nh % gh == 0 and vt.shape[-1] == t and hv > hd
    return pl.pallas_call(
        functools.partial(_flash_body, t=t, gh=gh),
        grid=(b, nh // gh, s // t),
        in_specs=[pl.BlockSpec((1, gh, t, dk), lambda bi, hp, qi: (bi, hp, qi, 0)),
                  pl.BlockSpec((1, gh, s, dk), lambda bi, hp, qi: (bi, hp, 0, 0)),
                  pl.BlockSpec((1, gh, s // t, hv, t), lambda bi, hp, qi: (bi, hp, 0, 0, 0))],
        out_specs=pl.BlockSpec((1, t, gh * hd), lambda bi, hp, qi: (bi, qi, hp)),
        out_shape=jax.ShapeDtypeStruct((b, s, nh * hd), F32),
        scratch_shapes=[pltpu.VMEM((gh, 1, t), F32), pltpu.VMEM((gh, hv, t), F32)],
        compiler_params=_cparams(("parallel", "parallel", "arbitrary")),
        name="flash_prompt",
    )(q, k, vt)


def _bmm_body(a_ref, b_ref, o_ref):
    o_ref[0] = _dot(a_ref[0].astype(BF16), b_ref[0])


def head_matmul(a, b):
    nh, m, k = a.shape
    n = b.shape[2]
    return pl.pallas_call(
        _bmm_body,
        grid=(nh,),
        in_specs=[pl.BlockSpec((1, m, k), lambda h: (h, 0, 0)),
                  pl.BlockSpec((1, k, n), lambda h: (h, 0, 0))],
        out_specs=pl.BlockSpec((1, m, n), lambda h: (h, 0, 0)),
        out_shape=jax.ShapeDtypeStruct((nh, m, n), F32),
        compiler_params=_cparams(("parallel",)),
        name="head_matmul",
    )(a, b)


def _attn_sample_body(qa_ref, qr_ref, kp_ref, krp_ref, kn_ref, krn_ref, o_ref, *, past, s_new, n_heads):
    qa = qa_ref[0].astype(BF16)
    qr = qr_ref[0].astype(BF16)
    kp = kp_ref[0].astype(BF16)
    krp = krp_ref[0].astype(BF16)
    kn = kn_ref[0].astype(BF16)
    krn = krn_ref[0].astype(BF16)
    s_past = _dot_nt(qa, kp) + _dot_nt(qr, krp)
    s_n = _dot_nt(qa, kn) + _dot_nt(qr, krn)
    row = lax.broadcasted_iota(jnp.int32, s_n.shape, 0)
    col = lax.broadcasted_iota(jnp.int32, s_n.shape, 1)
    qpos = past + row // n_heads
    kpos = past + col
    valid = (col < s_new) & (kpos // CHUNK <= qpos // CHUNK)
    s_n = jnp.where(valid, s_n, -jnp.inf)
    m = jnp.maximum(jnp.max(s_past, axis=-1, keepdims=True), jnp.max(s_n, axis=-1, keepdims=True))
    pp = jnp.exp2(s_past - m)
    pn = jnp.exp2(s_n - m)
    l = jnp.sum(pp, axis=-1, keepdims=True) + jnp.sum(pn, axis=-1, keepdims=True)
    o_ref[0] = (_dot(pp.astype(BF16), kp) + _dot(pn.astype(BF16), kn)) / l


def attn_sample(qa, qr, ckv_past, kr_past, ckv_new, kr_new, s_new, n_heads):
    b, r, lora = qa.shape
    past = ckv_past.shape[1]
    return pl.pallas_call(
        functools.partial(_attn_sample_body, past=past, s_new=s_new, n_heads=n_heads),
        grid=(b,),
        in_specs=[pl.BlockSpec((1, r, lora), lambda i: (i, 0, 0)),
                  pl.BlockSpec((1, r, LANES), lambda i: (i, 0, 0)),
                  pl.BlockSpec((1, past, lora), lambda i: (i, 0, 0)),
                  pl.BlockSpec((1, past, LANES), lambda i: (i, 0, 0)),
                  pl.BlockSpec((1, LANES, lora), lambda i: (i, 0, 0)),
                  pl.BlockSpec((1, LANES, LANES), lambda i: (i, 0, 0))],
        out_specs=pl.BlockSpec((1, r, lora), lambda i: (i, 0, 0)),
        out_shape=jax.ShapeDtypeStruct((b, r, lora), F32),
        compiler_params=_cparams(("parallel",)),
        name="attn_sample",
    )(qa, qr, ckv_past, kr_past, ckv_new, kr_new)


def _softplus(x):
    return jnp.maximum(x, 0.0) + jnp.log1p(jnp.exp(-jnp.abs(x)))


def _gdn_body(qkv_ref, z_ref, sm_ref, ga_ref, gb_ref, oa_ref, prev0_ref, s0_ref, convw_ref, alog_ref, dtb_ref,
              gn_ref, m_ref, sout_ref, s_scr, prev_scr, y_scr, *, L, n_heads):
    hd = HEAD_DIM
    heads = range(n_heads)
    c = pl.program_id(1)

    @pl.when(c == 0)
    def _():
        s_scr[...] = s0_ref[0]
        prev_scr[...] = prev0_ref[0]

    x = qkv_ref[...]
    xp = jnp.concatenate([prev_scr[...], x], axis=0)
    base = SUBLANES - (CONV_W - 1)
    y = xp[base:base + L] * convw_ref[0:1, :]
    for j in range(1, CONV_W):
        y = y + xp[base + j:base + j + L] * convw_ref[j:j + 1, :]
    y_scr[...] = y * _sigmoid(y)
    prev_scr[...] = x[L - SUBLANES:L]

    a = sm_ref[:, LANES:2 * LANES]
    bb = sm_ref[:, 2 * LANES:3 * LANES]
    g = -jnp.exp(alog_ref[...]) * _softplus(a + dtb_ref[...])
    beta = _sigmoid(bb)
    rowl = lax.broadcasted_iota(jnp.int32, g.shape, 0)
    cum = g
    d = 1
    while d < L:
        cum = cum + jnp.where(rowl >= d, pltpu.roll(cum, d, axis=0), 0.0)
        d *= 2
    cum_t = jnp.transpose(jnp.concatenate([cum, jnp.zeros((LANES - L, LANES), F32)], axis=0))

    row = lax.broadcasted_iota(jnp.int32, (L, L), 0)
    col = lax.broadcasted_iota(jnp.int32, (L, L), 1)
    tri = row >= col
    strict = row > col
    eye = jnp.where(row == col, 1.0, 0.0)
    gn = gn_ref[...]

    q_b, k_b, kb_b, rhs_b, qg_b, kd_b = [], [], [], [], [], []
    for h in heads:
        qh = y_scr[:, h * hd:(h + 1) * hd]
        kh = y_scr[:, (n_heads + h) * hd:(n_heads + h + 1) * hd]
        vh = y_scr[:, (2 * n_heads + h) * hd:(2 * n_heads + h + 1) * hd]
        qh = (qh * lax.rsqrt(jnp.sum(qh * qh, axis=-1, keepdims=True) + EPS)) * (hd ** -0.5)
        kh = kh * lax.rsqrt(jnp.sum(kh * kh, axis=-1, keepdims=True) + EPS)
        gcol = cum[:, h:h + 1]
        bcol = beta[:, h:h + 1]
        e_g = jnp.exp(gcol)
        kb = kh * bcol
        q_b.append(qh.astype(BF16))
        k_b.append(kh.astype(BF16))
        kb_b.append(kb.astype(BF16))
        rhs_b.append(jnp.concatenate([vh * bcol, kb * e_g], axis=1).astype(BF16))
        qg_b.append((qh * e_g).astype(BF16))
        kd_b.append((kh * jnp.exp(cum[L - 1:L, h:h + 1] - gcol)).astype(BF16))

    kk = [_dot_nt(kb_b[h], k_b[h]) for h in heads]
    qk = [_dot_nt(q_b[h], k_b[h]) for h in heads]

    nmat, attn_b = [], []
    for h in heads:
        diff = cum[:, h:h + 1] - cum_t[h:h + 1, 0:L]
        decay = jnp.where(tri, jnp.exp(jnp.where(tri, diff, 0.0)), 0.0)
        nmat.append(jnp.where(strict, kk[h] * decay, 0.0))
        attn_b.append(jnp.where(tri, qk[h] * decay, 0.0).astype(BF16))

    tmat = [eye - nmat[h] for h in heads]
    p_b = [nmat[h].astype(BF16) for h in heads]
    pmat = [_dot(p_b[h], p_b[h]) for h in heads]
    power = 2
    while True:
        p_b = [pmat[h].astype(BF16) for h in heads]
        tmat = [tmat[h] + _dot(tmat[h].astype(BF16), p_b[h]) for h in heads]
        power *= 2
        if power >= L:
            break
        pmat = [_dot(p_b[h], p_b[h]) for h in heads]

    uw = [_dot(tmat[h].astype(BF16), rhs_b[h]) for h in heads]
    s_old = [s_scr[h] for h in heads]
    s_b = [s_old[h].astype(BF16) for h in heads]
    ws = [_dot(jnp.concatenate([uw[h][:, hd:].astype(BF16), qg_b[h]], axis=0), s_b[h]) for h in heads]
    vn_b = [(uw[h][:, :hd] - ws[h][:L]).astype(BF16) for h in heads]
    av = [_dot(attn_b[h], vn_b[h]) for h in heads]
    ks = [_dot_tn(kd_b[h], vn_b[h]) for h in heads]
    for h in heads:
        sl = slice(h * hd, (h + 1) * hd)
        o = ws[h][L:] + av[h]
        s_scr[h] = s_old[h] * jnp.exp(cum[L - 1:L, h:h + 1]) + ks[h]
        zh = z_ref[:, sl]
        on = (o * lax.rsqrt(jnp.mean(o * o, axis=-1, keepdims=True) + EPS)) * gn
        ob = on * (zh * _sigmoid(zh))
        m_ref[:, sl] = (_sigmoid(ga_ref[:, sl]) * oa_ref[:, sl] + _sigmoid(gb_ref[:, sl]) * ob).astype(BF16)

    @pl.when(c == pl.num_programs(1) - 1)
    def _():
        sout_ref[0] = s_scr[...]


def gdn(u, o_a, prev0, s0, conv_w, a_log, dt_bias, gn, b, s, col_z, col_sm, col_ga, col_gb, n_heads):
    hd = HEAD_DIM
    L = min(s, CHUNK)
    nc = s // L
    qkv_w = 3 * n_heads * hd
    row = lambda bi, c: bi * nc + c
    alog = jnp.zeros((1, LANES), F32).at[0, :n_heads].set(a_log)
    dtb = jnp.zeros((1, LANES), F32).at[0, :n_heads].set(dt_bias)
    return pl.pallas_call(
        functools.partial(_gdn_body, L=L, n_heads=n_heads),
        grid=(b, nc),
        in_specs=[pl.BlockSpec((L, qkv_w), lambda bi, c: (row(bi, c), 0)),
                  pl.BlockSpec((L, n_heads * hd), lambda bi, c: (row(bi, c), col_z)),
                  pl.BlockSpec((L, 512), lambda bi, c: (row(bi, c), col_sm)),
                  pl.BlockSpec((L, n_heads * hd), lambda bi, c: (row(bi, c), col_ga)),
                  pl.BlockSpec((L, n_heads * hd), lambda bi, c: (row(bi, c), col_gb)),
                  pl.BlockSpec((L, n_heads * hd), lambda bi, c: (row(bi, c), 0)),
                  pl.BlockSpec((1, SUBLANES, qkv_w), lambda bi, c: (bi, 0, 0)),
                  pl.BlockSpec((1, n_heads, hd, hd), lambda bi, c: (bi, 0, 0, 0)),
                  pl.BlockSpec((CONV_W, qkv_w), lambda bi, c: (0, 0)),
                  pl.BlockSpec((1, LANES), lambda bi, c: (0, 0)),
                  pl.BlockSpec((1, LANES), lambda bi, c: (0, 0)),
                  pl.BlockSpec((1, hd), lambda bi, c: (0, 0))],
        out_specs=[pl.BlockSpec((L, n_heads * hd), lambda bi, c: (row(bi, c), 0)),
                   pl.BlockSpec((1, n_heads, hd, hd), lambda bi, c: (bi, 0, 0, 0))],
        out_shape=[jax.ShapeDtypeStruct((b * s, n_heads * hd), BF16),
                   jax.ShapeDtypeStruct((b, n_heads, hd, hd), F32)],
        scratch_shapes=[pltpu.VMEM((n_heads, hd, hd), F32),
                        pltpu.VMEM((SUBLANES, qkv_w), F32),
                        pltpu.VMEM((L, qkv_w), F32)],
        compiler_params=_cparams(("parallel", "arbitrary")),
        name="gdn",
    )(u, u, u, u, u, o_a, prev0, s0, conv_w, alog, dtb, gn.reshape(1, hd))


def _out_proj_body(m_ref, x_ref, w_ref, o_ref):
    o_ref[...] = x_ref[...] + _dot(m_ref[...], w_ref[...])


def out_proj(m, x, w, tm, tn):
    n, d = x.shape
    return pl.pallas_call(
        _out_proj_body,
        grid=(n // tm, d // tn),
        in_specs=[pl.BlockSpec((tm, d), lambda i, j: (i, 0)),
                  pl.BlockSpec((tm, tn), lambda i, j: (i, j)),
                  pl.BlockSpec((d, tn), lambda i, j: (0, j))],
        out_specs=pl.BlockSpec((tm, tn), lambda i, j: (i, j)),
        out_shape=jax.ShapeDtypeStruct((n, d), F32),
        compiler_params=_cparams(("parallel", "arbitrary")),
        name="out_proj",
    )(m, x, w)


def _route_body(x_ref, g_ref, whi_ref, wlo_ref, bias_ref, h_ref, eid_ref, wgt_ref, cnt_ref):
    x = x_ref[...]
    hn = (x * lax.rsqrt(jnp.mean(x * x, axis=-1, keepdims=True) + EPS)) * g_ref[...]
    h_hi, h_lo = _split_bf16(hn)
    h_ref[...] = h_hi
    lg = _dot3(h_hi, h_lo, whi_ref[...], wlo_ref[...]) + bias_ref[...]
    lane = lax.broadcasted_iota(jnp.int32, lg.shape, 1)
    big = jnp.int32(LANES)
    neg = -jnp.inf
    glog = jnp.where(lane < N_GROUPS, lg, neg)
    gmax = jnp.max(glog, axis=-1, keepdims=True)
    pg_top = 1.0 / jnp.sum(jnp.exp(glog - gmax), axis=-1, keepdims=True)
    grp = jnp.min(jnp.where(glog == gmax, lane, big), axis=-1, keepdims=True)
    lo = N_GROUPS + grp * EXPERTS_PER_GROUP
    le = jnp.where((lane >= lo) & (lane < lo + EXPERTS_PER_GROUP), lg, neg)
    v1 = jnp.max(le, axis=-1, keepdims=True)
    i1 = jnp.min(jnp.where(le == v1, lane, big), axis=-1, keepdims=True)
    le2 = jnp.where(lane == i1, neg, le)
    v2 = jnp.max(le2, axis=-1, keepdims=True)
    i2 = jnp.min(jnp.where(le2 == v2, lane, big), axis=-1, keepdims=True)
    e2 = jnp.exp(v2 - v1)
    den = 1.0 + e2
    w1 = pg_top * (1.0 / den)
    w2 = pg_top * (e2 / den)
    eid_ref[...] = jnp.where(lane == 0, i1 - N_GROUPS, jnp.where(lane == 1, i2 - N_GROUPS, 0))
    wgt_ref[...] = jnp.where(lane == 0, w1, jnp.where(lane == 1, w2, 0.0))
    hit = jnp.where((lane == i1 - N_GROUPS) | (lane == i2 - N_GROUPS), 1.0, 0.0)
    cnt_ref[0] = jnp.broadcast_to(jnp.sum(hit, axis=0, keepdims=True), cnt_ref.shape[1:]).astype(jnp.int32)


def route(x, g, whi, wlo, bias, tm):
    n, d = x.shape
    return pl.pallas_call(
        _route_body,
        grid=(n // tm,),
        in_specs=[pl.BlockSpec((tm, d), lambda i: (i, 0)),
                  pl.BlockSpec((1, d), lambda i: (0, 0)),
                  pl.BlockSpec((d, LANES), lambda i: (0, 0)),
                  pl.BlockSpec((d, LANES), lambda i: (0, 0)),
                  pl.BlockSpec((1, LANES), lambda i: (0, 0))],
        out_specs=[pl.BlockSpec((tm, d), lambda i: (i, 0)),
                   pl.BlockSpec((tm, LANES), lambda i: (i, 0)),
                   pl.BlockSpec((tm, LANES), lambda i: (i, 0)),
                   pl.BlockSpec((1, SUBLANES, LANES), lambda i: (i, 0, 0))],
        out_shape=[jax.ShapeDtypeStruct((n, d), BF16),
                   jax.ShapeDtypeStruct((n, LANES), jnp.int32),
                   jax.ShapeDtypeStruct((n, LANES), F32),
                   jax.ShapeDtypeStruct((n // tm, SUBLANES, LANES), jnp.int32)],
        compiler_params=_cparams(("parallel",)),
        name="route",
    )(x, g.reshape(1, d), whi, wlo, bias)


def _expert_body(be_tab, first_tab, nused, x_ref, sw_ref, wg_ref, wu_ref, wd_ref, o_ref, wg_s, wu_s, wd_s):
    blk = pl.program_id(0)

    @pl.when(blk < nused[0])
    def _():
        @pl.when(first_tab[blk] == 1)
        def _():
            wg_s[...] = wg_ref[0].astype(BF16)
            wu_s[...] = wu_ref[0].astype(BF16)
            wd_s[...] = wd_ref[0].astype(BF16)

        x = x_ref[...]
        gate = _dot(x, wg_s[...])
        up = _dot(x, wu_s[...])
        hid = (gate * _sigmoid(gate)) * up
        o_ref[...] = _dot(hid.astype(BF16), wd_s[...]) * sw_ref[...]

    @pl.when(blk >= nused[0])
    def _():
        o_ref[...] = jnp.zeros(o_ref.shape, F32)


def experts(x_sorted, slot_w, block_e, first, nused, w_gate, w_up, w_down, bm):
    n_slots, d = x_sorted.shape
    de = w_gate.shape[2]
    n_blocks = n_slots // bm
    grid_spec = pltpu.PrefetchScalarGridSpec(
        num_scalar_prefetch=3,
        grid=(n_blocks,),
        in_specs=[pl.BlockSpec((bm, d), lambda i, be, fi, nu: (i, 0)),
                  pl.BlockSpec((bm, 1), lambda i, be, fi, nu: (i, 0)),
                  pl.BlockSpec((1, d, de), lambda i, be, fi, nu: (be[i], 0, 0)),
                  pl.BlockSpec((1, d, de), lambda i, be, fi, nu: (be[i], 0, 0)),
                  pl.BlockSpec((1, de, d), lambda i, be, fi, nu: (be[i], 0, 0))],
        out_specs=pl.BlockSpec((bm, d), lambda i, be, fi, nu: (i, 0)),
        scratch_shapes=[pltpu.VMEM((d, de), BF16), pltpu.VMEM((d, de), BF16), pltpu.VMEM((de, d), BF16)],
    )
    return pl.pallas_call(
        _expert_body,
        grid_spec=grid_spec,
        out_shape=jax.ShapeDtypeStruct((n_slots, d), F32),
        compiler_params=_cparams(("arbitrary",)),
        name="experts",
    )(block_e, first, nused, x_sorted, slot_w, w_gate, w_up, w_down)


def _final_body(x_ref, ma_ref, mb_ref, g_ref, o_ref):
    x = x_ref[...] + (ma_ref[...] + mb_ref[...])
    o_ref[...] = (x * lax.rsqrt(jnp.mean(x * x, axis=-1, keepdims=True) + EPS)) * g_ref[...]


def final_norm(x, ma, mb, g, tm, row0):
    n, d = x.shape
    spec = pl.BlockSpec((tm, d), lambda i: (i, 0))
    mspec = pl.BlockSpec((tm, d), lambda i: (i + row0 // tm, 0))
    return pl.pallas_call(
        _final_body,
        grid=(n // tm,),
        in_specs=[spec, mspec, mspec, pl.BlockSpec((1, d), lambda i: (0, 0))],
        out_specs=spec,
        out_shape=jax.ShapeDtypeStruct((n, d), F32),
        compiler_params=_cparams(("parallel",)),
        name="final_norm",
    )(x, ma, mb, g.reshape(1, d))


def _tile(n, pref):
    t = min(n, pref)
    while n % t:
        t //= 2
    return t


def _rope_table(pos, b):
    half = ROPE_DIM // 2
    inv = ROPE_BASE ** (-jnp.arange(half, dtype=F32) / half)
    ang = pos.astype(F32)[:, None] * inv[None, :]
    cos, sin = jnp.cos(ang), jnp.sin(ang)
    cs = jnp.concatenate([cos, cos, sin, sin], axis=-1)
    return jnp.tile(cs, (b, 1))


def _rot_cols(w):
    half = ROPE_DIM // 2
    return jnp.concatenate([-w[..., half:], w[..., :half]], axis=-1)


def _moe(h_bf, eid, wgt, counts, w_gate, w_up, w_down, bm):
    n, d = h_bf.shape
    n_exp = w_gate.shape[0]
    a = n * TOP_K
    e_flat = eid.reshape(-1)
    w_flat = wgt.reshape(-1)
    order = jnp.argsort(e_flat).astype(jnp.int32)
    start = jnp.cumsum(counts) - counts
    padded = (counts + bm - 1) // bm * bm
    pend = jnp.cumsum(padded)
    pstart = pend - padded
    n_blocks = (a + n_exp * (bm - 1) + bm - 1) // bm
    n_slots = n_blocks * bm
    block_e = jnp.minimum(jnp.searchsorted(pend, jnp.arange(n_blocks, dtype=jnp.int32) * bm, side='right'),
                          n_exp - 1).astype(jnp.int32)
    first = jnp.concatenate([jnp.ones((1,), jnp.int32), (block_e[1:] != block_e[:-1]).astype(jnp.int32)])
    nused = (pend[-1] // bm).astype(jnp.int32).reshape(1)
    slot = jnp.arange(n_slots, dtype=jnp.int32).reshape(n_blocks, bm)
    off = slot - pstart[block_e][:, None]
    valid = ((off >= 0) & (off < counts[block_e][:, None])).reshape(-1)
    src = order[jnp.clip(start[block_e][:, None] + off, 0, a - 1).reshape(-1)]
    slot_tok = jnp.where(valid, src // TOP_K, slot.reshape(-1) % n)
    slot_w = jnp.where(valid, w_flat[src], 0.0)
    x_sorted = h_bf[slot_tok]
    out = experts(x_sorted, slot_w.reshape(n_slots, 1), block_e, first, nused, w_gate, w_up, w_down, bm)
    rank = jnp.argsort(order).astype(jnp.int32)
    slot_of = (pstart[e_flat] + (rank - start[e_flat])).reshape(n, TOP_K)
    return out[slot_of[:, 0]], out[slot_of[:, 1]]


def _mixer(x3, pos, ckv_past, kr_past, s0, conv0, wp):
    b, s, d = x3.shape
    n = b * s
    n_heads = wp['n_heads']
    hd = HEAD_DIM
    x = x3.reshape(n, d)
    cols = wp['cols']

    u = rms_matmul(x, wp['attn_norm_g'], wp['w_all'], _tile(n, 1024), 512)

    cs = _rope_table(pos, b)
    scale = (hd + ROPE_DIM) ** -0.5 * float(np.log2(np.e))
    prompt = ckv_past is None
    t_att = _tile(s, ATTN_TILE)
    tm = min(_tile(s, 256), t_att)
    res = mla_prep(u, cs, wp['q_norm_g'], wp['kv_norm_g'], wp['wq_ext'], wp['wk'] if prompt else None,
                   wp['wvt'] if prompt else None, b, s, tm, t_att, cols['q_lat'] // 512, cols['kv_lat'] // 512,
                   cols['small'] // 512, n_heads, scale)
    ckv, kr_pad, q = res[:3]
    kr = kr_pad[:, :ROPE_DIM]
    lora = ckv.shape[1]
    if prompt:
        o_a = flash_prompt(q, res[3], res[4], t_att, ATTN_HEADS_PER_STEP).reshape(n, n_heads * hd)
    else:
        qn = q[..., :hd].transpose(1, 0, 2, 3).reshape(n_heads, n, hd)
        q_abs = head_matmul(qn, wp['w_uk_t'])
        q_abs = q_abs.reshape(n_heads, b, s, lora).transpose(1, 2, 0, 3).reshape(b, s * n_heads, lora)
        qr = q[..., hd:].transpose(0, 2, 1, 3).reshape(b, s * n_heads, hd)
        kr_past_pad = jnp.pad(kr_past, ((0, 0), (0, 0), (0, LANES - ROPE_DIM)))
        ckv_new = jnp.pad(ckv.reshape(b, s, lora), ((0, 0), (0, LANES - s), (0, 0)))
        kr_new = jnp.pad(kr_pad.reshape(b, s, LANES), ((0, 0), (0, LANES - s), (0, 0)))
        o_lat = attn_sample(q_abs, qr, ckv_past, kr_past_pad, ckv_new, kr_new, s, n_heads)
        o_lat = o_lat.reshape(b, s, n_heads, lora).transpose(2, 0, 1, 3).reshape(n_heads, n, lora)
        o_h = head_matmul(o_lat, wp['w_uv_h'])
        o_a = o_h.reshape(n_heads, n, hd).transpose(1, 0, 2).reshape(n, n_heads * hd)

    qkv_w = 3 * n_heads * hd
    prev0 = jnp.pad(conv0, ((0, 0), (SUBLANES - (CONV_W - 1), 0), (0, 0)))
    merged, s_new = gdn(u, o_a, prev0, s0, wp['conv_w'], wp['a_log'], wp['dt_bias'], wp['gdn_norm_g'], b, s,
                        cols['z'] // d, cols['small'] // 512, cols['gate_a'] // d, cols['gate_b'] // d, n_heads)
    tail = u.reshape(b, s, -1)[:, max(s - (CONV_W - 1), 0):, :qkv_w]
    conv_new = jnp.concatenate([conv0, tail], axis=1)[:, -(CONV_W - 1):]

    x1 = out_proj(merged, x, wp['w_out'], _tile(n, 1024), 512)
    return x1, ckv.reshape(b, s, lora), kr.reshape(b, s, ROPE_DIM), s_new, conv_new


def _prep_weights(l, attn_norm_g, w_in, q_norm_g, w_uq, kv_norm_g, w_uk, w_uv, conv_w, a_log, dt_bias, gdn_norm_g,
                  w_out, ffn_norm_g, w_group, b_group, w_router, b_router, w_gate, w_up, w_down):
    d = w_in.shape[1]
    q_lora, n_heads, _ = w_uq.shape[1:]
    kv_lora = w_uk.shape[1]
    hd = HEAD_DIM
    qkv_w = 3 * n_heads * hd
    sizes = (q_lora, kv_lora, ROPE_DIM, qkv_w, n_heads, n_heads, n_heads * hd, d, d)
    offs = np.concatenate([[0], np.cumsum(sizes)])
    wi = w_in[l]
    part = lambda i: wi[:, offs[i]:offs[i + 1]]
    w_qlat, w_kvlat, w_kr, w_qkv, w_a, w_b, w_z, w_ga, w_gb = [part(i) for i in range(9)]
    zpad = lambda c: jnp.zeros((d, c), F32)
    small = jnp.concatenate([w_kr, _rot_cols(w_kr), w_a, zpad(LANES - n_heads), w_b, zpad(LANES - n_heads),
                             zpad(LANES)], axis=1)
    w_all = jnp.concatenate([w_qkv, w_z, w_ga, w_gb, w_qlat, w_kvlat, small], axis=1).astype(BF16)
    cols = {'qkv': 0, 'z': qkv_w, 'gate_a': qkv_w + n_heads * hd, 'gate_b': qkv_w + n_heads * hd + d,
            'q_lat': qkv_w + n_heads * hd + 2 * d}
    cols['kv_lat'] = cols['q_lat'] + q_lora
    cols['small'] = cols['kv_lat'] + kv_lora
    wq = w_uq[l]
    wq_nope = wq[:, :, :hd].reshape(q_lora, n_heads * hd)
    wq_r = wq[:, :, hd:]
    wq_rope = jnp.concatenate([wq_r, _rot_cols(wq_r)], axis=-1).reshape(q_lora, n_heads * hd)
    wq_ext = jnp.concatenate([wq_nope, wq_rope], axis=1).astype(BF16)
    wk = w_uk[l].reshape(kv_lora, n_heads * hd).astype(BF16)
    wvt = w_uv[l].reshape(kv_lora, n_heads * hd).T.astype(BF16)
    wr = jnp.concatenate([w_group[l], w_router[l].transpose(1, 0, 2).reshape(d, -1)], axis=1)
    wr = jnp.pad(wr, ((0, 0), (0, LANES - wr.shape[1])))
    wr_hi = wr.astype(BF16)
    wr_lo = (wr - wr_hi.astype(F32)).astype(BF16)
    br = jnp.concatenate([b_group[l], b_router[l].reshape(-1)])
    br = jnp.pad(br, (0, LANES - br.shape[0])).reshape(1, LANES)
    return {
        'n_heads': n_heads, 'cols': cols, 'w_all': w_all, 'attn_norm_g': attn_norm_g[l],
        'q_norm_g': q_norm_g[l], 'kv_norm_g': kv_norm_g[l], 'wq_ext': wq_ext, 'wk': wk, 'wvt': wvt,
        'w_uk_t': w_uk[l].transpose(1, 2, 0).astype(BF16),
        'w_uv_h': w_uv[l].transpose(1, 0, 2).astype(BF16),
        'conv_w': conv_w[l], 'a_log': a_log[l], 'dt_bias': dt_bias[l], 'gdn_norm_g': gdn_norm_g[l],
        'w_out': w_out[l].astype(BF16), 'ffn_norm_g': ffn_norm_g[l], 'wr_hi': wr_hi, 'wr_lo': wr_lo, 'br': br,
        'w_gate': w_gate[l], 'w_up': w_up[l], 'w_down': w_down[l],
    }


def kernel(x_prompt, x_sample, cache_ckv, cache_k_rope, state_gdn, state_conv, attn_norm_g, w_in, q_norm_g, w_uq, kv_norm_g, w_uk, w_uv, conv_w, a_log, dt_bias, gdn_norm_g, w_out, ffn_norm_g, w_group, b_group, w_router, b_router, w_gate, w_up, w_down, final_norm_g):
    depth = w_in.shape[0]
    assert depth == 1, "final RMSNorm is fused into the last layer; deeper trunks are not supported"
    b_p, s_p, d = x_prompt.shape
    b_s, s_s, _ = x_sample.shape
    past = cache_ckv.shape[2]
    n_heads = w_uq.shape[2]
    qkv_w = 3 * n_heads * HEAD_DIM
    pos_p = jnp.arange(s_p, dtype=jnp.int32)
    pos_s = past + jnp.arange(s_s, dtype=jnp.int32)
    wp = _prep_weights(0, attn_norm_g, w_in, q_norm_g, w_uq, kv_norm_g, w_uk, w_uv, conv_w, a_log, dt_bias,
                       gdn_norm_g, w_out, ffn_norm_g, w_group, b_group, w_router, b_router, w_gate, w_up, w_down)
    s0 = jnp.zeros((b_p, n_heads, HEAD_DIM, HEAD_DIM), F32)
    c0 = jnp.zeros((b_p, CONV_W - 1, qkv_w), F32)
    x1_p, ckv_p, kr_p, sg_p, sc_p = _mixer(x_prompt, pos_p, None, None, s0, c0, wp)
    x1_s, ckv_s, kr_s, sg_s, sc_s = _mixer(x_sample, pos_s, cache_ckv[0], cache_k_rope[0], state_gdn[0],
                                           state_conv[0], wp)

    n_p, n_s = b_p * s_p, b_s * s_s
    routed = [route(x1, wp['ffn_norm_g'], wp['wr_hi'], wp['wr_lo'], wp['br'], _tile(x1.shape[0], 512))
              for x1 in (x1_p, x1_s)]
    h_bf, eid, wgt = [jnp.concatenate([routed[0][i], routed[1][i]], axis=0) for i in range(3)]
    n_exp = wp['w_gate'].shape[0]
    counts = sum(r[3][:, 0, :n_exp].sum(axis=0) for r in routed)
    ma, mb = _moe(h_bf, eid[:, :TOP_K], wgt[:, :TOP_K], counts, wp['w_gate'], wp['w_up'], wp['w_down'], MOE_BLOCK)
    tm_s = _tile(int(np.gcd(n_p, n_s)), 512)
    yp = final_norm(x1_p, ma, mb, final_norm_g, _tile(n_p, 512), 0).reshape(b_p, s_p, d)
    ys = final_norm(x1_s, ma, mb, final_norm_g, tm_s, n_p).reshape(b_s, s_s, d)
    return (yp, ys, ckv_p[None], kr_p[None], sg_p[None], sc_p[None], ckv_s[None], kr_s[None], sg_s[None], sc_s[None])
```

```python
import functools

import numpy as np
import jax
import jax.numpy as jnp
from jax import lax
from jax.experimental import pallas as pl
from jax.experimental.pallas import tpu as pltpu

F32 = jnp.float32
BF16 = jnp.bfloat16

EPS = 1e-6
CHUNK = 64
ROPE_BASE = 10000.0
HEAD_DIM = 128
ROPE_DIM = 64
CONV_W = 4
TOP_K = 2
N_GROUPS = 8
EXPERTS_PER_GROUP = 8

LANES = 128
SUBLANES = 8
BF16_ROWS = 16
VMEM_LIMIT = 56 * 1024 * 1024
ATTN_TILE = 512
ATTN_HEADS_PER_STEP = 4
MOE_BLOCK = 256
IN_PROJ_TN = 768


def _cparams(sem):
    return pltpu.CompilerParams(dimension_semantics=sem, vmem_limit_bytes=VMEM_LIMIT)


def _dot(a, b):
    return jnp.dot(a, b, preferred_element_type=F32)


def _dot_nt(a, b):
    return lax.dot_general(a, b, (((1,), (1,)), ((), ())), preferred_element_type=F32)


def _dot_tn(a, b):
    return lax.dot_general(a, b, (((0,), (0,)), ((), ())), preferred_element_type=F32)


def _sigmoid(x):
    return 1.0 / (1.0 + jnp.exp(-x))


def _split_bf16(a):
    hi = a.astype(BF16)
    lo = (a - hi.astype(F32)).astype(BF16)
    return hi, lo


def _dot3(a_hi, a_lo, b_hi, b_lo):
    return _dot(a_hi, b_hi) + (_dot(a_hi, b_lo) + _dot(a_lo, b_hi))


def _rms_mm_body(x_ref, g_ref, w_ref, o_ref, h_ref):
    @pl.when(pl.program_id(1) == 0)
    def _():
        x = x_ref[...]
        ms = jnp.mean(x * x, axis=-1, keepdims=True)
        h_ref[...] = ((x * lax.rsqrt(ms + EPS)) * g_ref[...]).astype(BF16)

    o_ref[...] = _dot(h_ref[...], w_ref[...])


def rms_matmul(x, g, w, tm, tn):
    n, k = x.shape
    m = w.shape[1]
    return pl.pallas_call(
        _rms_mm_body,
        grid=(n // tm, m // tn),
        in_specs=[pl.BlockSpec((tm, k), lambda i, j: (i, 0)),
                  pl.BlockSpec((1, k), lambda i, j: (0, 0)),
                  pl.BlockSpec((k, tn), lambda i, j: (0, j))],
        out_specs=pl.BlockSpec((tm, tn), lambda i, j: (i, j)),
        out_shape=jax.ShapeDtypeStruct((n, m), F32),
        scratch_shapes=[pltpu.VMEM((tm, k), BF16)],
        compiler_params=_cparams(("parallel", "arbitrary")),
        name="rms_matmul",
    )(x, g.reshape(1, k), w)


def _mla_prep_body(ql_ref, kvl_ref, sm_ref, cs_ref, gq_ref, gkv_ref, wq_ref, *rest, n_heads, scale, with_kv):
    if with_kv:
        wk_ref, wvt_ref, ckv_ref, kr_ref, q_ref, k_ref, vt_ref = rest
    else:
        ckv_ref, kr_ref, q_ref = rest
    hd = HEAD_DIM
    ql = ql_ref[...]
    qn = ((ql * lax.rsqrt(jnp.mean(ql * ql, axis=-1, keepdims=True) + EPS)) * gq_ref[...]).astype(BF16)
    q_all = _dot(qn, wq_ref[...])
    kvl = kvl_ref[...]
    ckv = (kvl * lax.rsqrt(jnp.mean(kvl * kvl, axis=-1, keepdims=True) + EPS)) * gkv_ref[...]
    ckv_ref[...] = ckv
    if with_kv:
        ckv_b = ckv.astype(BF16)
        k_all = _dot(ckv_b, wk_ref[...])
        vt_all = _dot_nt(wvt_ref[...], ckv_b)

    cs = cs_ref[...]
    lane = lax.broadcasted_iota(jnp.int32, cs.shape, 1)

    def rope(t):
        t = t * cs
        return jnp.where(lane < ROPE_DIM, t + pltpu.roll(t, ROPE_DIM, axis=1), 0.0)

    kr = rope(sm_ref[:, 0:LANES])
    kr_ref[...] = kr
    kr_b = kr.astype(BF16)
    off = n_heads * hd
    for h in range(n_heads):
        q_ref[0, h, :, 0:hd] = (q_all[:, h * hd:(h + 1) * hd] * scale).astype(BF16)
        q_ref[0, h, :, hd:2 * hd] = (rope(q_all[:, off + h * hd:off + (h + 1) * hd]) * scale).astype(BF16)
        if with_kv:
            k_ref[0, h, :, 0:hd] = k_all[:, h * hd:(h + 1) * hd].astype(BF16)
            k_ref[0, h, :, hd:2 * hd] = kr_b
            vt_ref[0, h, 0, 0:hd, :] = vt_all[h * hd:(h + 1) * hd, :].astype(BF16)
            vt_ref[0, h, 0, hd:hd + BF16_ROWS, :] = jnp.ones((BF16_ROWS, vt_all.shape[1]), BF16)


def mla_prep(u, cs, gq, gkv, wq, wk, wvt, b, s, tm, t_att, col_ql, col_kvl, col_sm, n_heads, scale):
    lora = wq.shape[0]
    nt = s // tm
    row = lambda bi, si: bi * nt + si
    hd = HEAD_DIM
    with_kv = wk is not None
    full = lambda bi, si: (0, 0)
    in_specs = [pl.BlockSpec((tm, lora), lambda bi, si: (row(bi, si), col_ql)),
                pl.BlockSpec((tm, lora), lambda bi, si: (row(bi, si), col_kvl)),
                pl.BlockSpec((tm, 512), lambda bi, si: (row(bi, si), col_sm)),
                pl.BlockSpec((tm, LANES), lambda bi, si: (row(bi, si), 0)),
                pl.BlockSpec((1, lora), full),
                pl.BlockSpec((1, lora), full),
                pl.BlockSpec(wq.shape, full)]
    out_specs = [pl.BlockSpec((tm, lora), lambda bi, si: (row(bi, si), 0)),
                 pl.BlockSpec((tm, LANES), lambda bi, si: (row(bi, si), 0)),
                 pl.BlockSpec((1, n_heads, tm, 2 * hd), lambda bi, si: (bi, 0, si, 0))]
    out_shape = [jax.ShapeDtypeStruct((b * s, lora), F32),
                 jax.ShapeDtypeStruct((b * s, LANES), F32),
                 jax.ShapeDtypeStruct((b, n_heads, s, 2 * hd), BF16)]
    args = [u, u, u, cs, gq.reshape(1, lora), gkv.reshape(1, lora), wq]
    if with_kv:
        r = t_att // tm
        in_specs += [pl.BlockSpec(wk.shape, full), pl.BlockSpec(wvt.shape, full)]
        out_specs += [pl.BlockSpec((1, n_heads, tm, 2 * hd), lambda bi, si: (bi, 0, si, 0)),
                      pl.BlockSpec((1, n_heads, 1, hd + BF16_ROWS, tm),
                                   lambda bi, si: (bi, 0, si // r, 0, si % r))]
        out_shape += [jax.ShapeDtypeStruct((b, n_heads, s, 2 * hd), BF16),
                      jax.ShapeDtypeStruct((b, n_heads, s // t_att, hd + BF16_ROWS, t_att), BF16)]
        args += [wk, wvt]
    return pl.pallas_call(
        functools.partial(_mla_prep_body, n_heads=n_heads, scale=scale, with_kv=with_kv),
        grid=(b, nt),
        in_specs=in_specs,
        out_specs=out_specs,
        out_shape=out_shape,
        compiler_params=_cparams(("parallel", "parallel")),
        name="mla_prep",
    )(*args)


def _flash_body(q_ref, k_ref, vt_ref, o_ref, m_scr, acc_scr, *, t, gh):
    qi = pl.program_id(2)
    hd = HEAD_DIM
    m_scr[...] = jnp.full(m_scr.shape, -jnp.inf, F32)
    acc_scr[...] = jnp.zeros(acc_scr.shape, F32)

    def block(j, masked):
        start = pl.multiple_of(j * t, t)
        scs = [_dot_nt(k_ref[0, g, pl.ds(start, t), :], q_ref[0, g]) for g in range(gh)]
        for g in range(gh):
            sc = scs[g]
            if masked:
                krow = lax.broadcasted_iota(jnp.int32, sc.shape, 0)
                qcol = lax.broadcasted_iota(jnp.int32, sc.shape, 1)
                sc = jnp.where(krow // CHUNK <= qcol // CHUNK, sc, -jnp.inf)
            m_prev = m_scr[g]
            m_new = jnp.maximum(m_prev, jnp.max(sc, axis=0, keepdims=True))
            alpha = jnp.exp2(m_prev - m_new)
            p = jnp.exp2(sc - m_new)
            acc_scr[g] = alpha * acc_scr[g] + _dot(vt_ref[0, g, j], p.astype(BF16))
            m_scr[g] = m_new

    def full_block(j, carry):
        block(j, False)
        return carry

    lax.fori_loop(0, qi, full_block, 0)
    block(qi, True)
    for g in range(gh):
        acc = acc_scr[g]
        o_ref[0, :, g * hd:(g + 1) * hd] = jnp.transpose(acc[:hd] / acc[hd:hd + 1])


def flash_prompt(q, k, vt, t, gh):
    b, nh, s, dk = q.shape
    hd = HEAD_DIM
    hv = vt.shape[-2]
    assert t % CHUNK == 0 and s % t == 0 and nh % gh == 0 and vt.shape[-1] == t and hv > hd
    return pl.pallas_call(
        functools.partial(_flash_body, t=t, gh=gh),
        grid=(b, nh // gh, s // t),
        in_specs=[pl.BlockSpec((1, gh, t, dk), lambda bi, hp, qi: (bi, hp, qi, 0)),
                  pl.BlockSpec((1, gh, s, dk), lambda bi, hp, qi: (bi, hp, 0, 0)),
                  pl.BlockSpec((1, gh, s // t, hv, t), lambda bi, hp, qi: (bi, hp, 0, 0, 0))],
        out_specs=pl.BlockSpec((1, t, gh * hd), lambda bi, hp, qi: (bi, qi, hp)),
        out_shape=jax.ShapeDtypeStruct((b, s, nh * hd), F32),
        scratch_shapes=[pltpu.VMEM((gh, 1, t), F32), pltpu.VMEM((gh, hv, t), F32)],
        compiler_params=_cparams(("parallel", "parallel", "arbitrary")),
        name="flash_prompt",
    )(q, k, vt)


def _bmm_body(a_ref, b_ref, o_ref):
    o_ref[0] = _dot(a_ref[0].astype(BF16), b_ref[0])


def head_matmul(a, b):
    nh, m, k = a.shape
    n = b.shape[2]
    return pl.pallas_call(
        _bmm_body,
        grid=(nh,),
        in_specs=[pl.BlockSpec((1, m, k), lambda h: (h, 0, 0)),
                  pl.BlockSpec((1, k, n), lambda h: (h, 0, 0))],
        out_specs=pl.BlockSpec((1, m, n), lambda h: (h, 0, 0)),
        out_shape=jax.ShapeDtypeStruct((nh, m, n), F32),
        compiler_params=_cparams(("parallel",)),
        name="head_matmul",
    )(a, b)


def _attn_sample_body(qa_ref, qr_ref, kp_ref, krp_ref, kn_ref, krn_ref, o_ref, *, past, s_new, n_heads):
    qa = qa_ref[0].astype(BF16)
    qr = qr_ref[0].astype(BF16)
    kp = kp_ref[0].astype(BF16)
    krp = krp_ref[0].astype(BF16)
    kn = kn_ref[0].astype(BF16)
    krn = krn_ref[0].astype(BF16)
    s_past = _dot_nt(qa, kp) + _dot_nt(qr, krp)
    s_n = _dot_nt(qa, kn) + _dot_nt(qr, krn)
    row = lax.broadcasted_iota(jnp.int32, s_n.shape, 0)
    col = lax.broadcasted_iota(jnp.int32, s_n.shape, 1)
    qpos = past + row // n_heads
    kpos = past + col
    valid = (col < s_new) & (kpos // CHUNK <= qpos // CHUNK)
    s_n = jnp.where(valid, s_n, -jnp.inf)
    m = jnp.maximum(jnp.max(s_past, axis=-1, keepdims=True), jnp.max(s_n, axis=-1, keepdims=True))
    pp = jnp.exp2(s_past - m)
    pn = jnp.exp2(s_n - m)
    l = jnp.sum(pp, axis=-1, keepdims=True) + jnp.sum(pn, axis=-1, keepdims=True)
    o_ref[0] = (_dot(pp.astype(BF16), kp) + _dot(pn.astype(BF16), kn)) / l


def attn_sample(qa, qr, ckv_past, kr_past, ckv_new, kr_new, s_new, n_heads):
    b, r, lora = qa.shape
    past = ckv_past.shape[1]
    return pl.pallas_call(
        functools.partial(_attn_sample_body, past=past, s_new=s_new, n_heads=n_heads),
        grid=(b,),
        in_specs=[pl.BlockSpec((1, r, lora), lambda i: (i, 0, 0)),
                  pl.BlockSpec((1, r, LANES), lambda i: (i, 0, 0)),
                  pl.BlockSpec((1, past, lora), lambda i: (i, 0, 0)),
                  pl.BlockSpec((1, past, LANES), lambda i: (i, 0, 0)),
                  pl.BlockSpec((1, LANES, lora), lambda i: (i, 0, 0)),
                  pl.BlockSpec((1, LANES, LANES), lambda i: (i, 0, 0))],
        out_specs=pl.BlockSpec((1, r, lora), lambda i: (i, 0, 0)),
        out_shape=jax.ShapeDtypeStruct((b, r, lora), F32),
        compiler_params=_cparams(("parallel",)),
        name="attn_sample",
    )(qa, qr, ckv_past, kr_past, ckv_new, kr_new)


def _softplus(x):
    return jnp.maximum(x, 0.0) + jnp.log1p(jnp.exp(-jnp.abs(x)))


def _gdn_body(qkv_ref, z_ref, sm_ref, ga_ref, gb_ref, oa_ref, prev0_ref, s0_ref, convw_ref, alog_ref, dtb_ref,
              gn_ref, m_ref, sout_ref, s_scr, prev_scr, y_scr, *, L, n_heads):
    hd = HEAD_DIM
    heads = range(n_heads)
    c = pl.program_id(1)

    @pl.when(c == 0)
    def _():
        s_scr[...] = s0_ref[0]
        prev_scr[...] = prev0_ref[0]

    x = qkv_ref[...]
    xp = jnp.concatenate([prev_scr[...], x], axis=0)
    base = SUBLANES - (CONV_W - 1)
    y = xp[base:base + L] * convw_ref[0:1, :]
    for j in range(1, CONV_W):
        y = y + xp[base + j:base + j + L] * convw_ref[j:j + 1, :]
    y_scr[...] = y * _sigmoid(y)
    prev_scr[...] = x[L - SUBLANES:L]

    a = sm_ref[:, LANES:2 * LANES]
    bb = sm_ref[:, 2 * LANES:3 * LANES]
    g = -jnp.exp(alog_ref[...]) * _softplus(a + dtb_ref[...])
    beta = _sigmoid(bb)
    rowl = lax.broadcasted_iota(jnp.int32, g.shape, 0)
    cum = g
    d = 1
    while d < L:
        cum = cum + jnp.where(rowl >= d, pltpu.roll(cum, d, axis=0), 0.0)
        d *= 2
    cum_t = jnp.transpose(jnp.concatenate([cum, jnp.zeros((LANES - L, LANES), F32)], axis=0))

    row = lax.broadcasted_iota(jnp.int32, (L, L), 0)
    col = lax.broadcasted_iota(jnp.int32, (L, L), 1)
    tri = row >= col
    strict = row > col
    eye = jnp.where(row == col, 1.0, 0.0)
    gn = gn_ref[...]

    q_b, k_b, kb_b, rhs_b, qg_b, kd_b = [], [], [], [], [], []
    for h in heads:
        qh = y_scr[:, h * hd:(h + 1) * hd]
        kh = y_scr[:, (n_heads + h) * hd:(n_heads + h + 1) * hd]
        vh = y_scr[:, (2 * n_heads + h) * hd:(2 * n_heads + h + 1) * hd]
        qh = (qh * lax.rsqrt(jnp.sum(qh * qh, axis=-1, keepdims=True) + EPS)) * (hd ** -0.5)
        kh = kh * lax.rsqrt(jnp.sum(kh * kh, axis=-1, keepdims=True) + EPS)
        gcol = cum[:, h:h + 1]
        bcol = beta[:, h:h + 1]
        e_g = jnp.exp(gcol)
        kb = kh * bcol
        q_b.append(qh.astype(BF16))
        k_b.append(kh.astype(BF16))
        kb_b.append(kb.astype(BF16))
        rhs_b.append(jnp.concatenate([vh * bcol, kb * e_g], axis=1).astype(BF16))
        qg_b.append((qh * e_g).astype(BF16))
        kd_b.append((kh * jnp.exp(cum[L - 1:L, h:h + 1] - gcol)).astype(BF16))

    kk = [_dot_nt(kb_b[h], k_b[h]) for h in heads]
    qk = [_dot_nt(q_b[h], k_b[h]) for h in heads]

    nmat, attn_b = [], []
    for h in heads:
        diff = cum[:, h:h + 1] - cum_t[h:h + 1, 0:L]
        decay = jnp.where(tri, jnp.exp(jnp.where(tri, diff, 0.0)), 0.0)
        nmat.append(jnp.where(strict, kk[h] * decay, 0.0))
        attn_b.append(jnp.where(tri, qk[h] * decay, 0.0).astype(BF16))

    tmat = [eye - nmat[h] for h in heads]
    p_b = [nmat[h].astype(BF16) for h in heads]
    pmat = [_dot(p_b[h], p_b[h]) for h in heads]
    power = 2
    while True:
        p_b = [pmat[h].astype(BF16) for h in heads]
        tmat = [tmat[h] + _dot(tmat[h].astype(BF16), p_b[h]) for h in heads]
        power *= 2
        if power >= L:
            break
        pmat = [_dot(p_b[h], p_b[h]) for h in heads]

    uw = [_dot(tmat[h].astype(BF16), rhs_b[h]) for h in heads]
    s_old = [s_scr[h] for h in heads]
    s_b = [s_old[h].astype(BF16) for h in heads]
    ws = [_dot(jnp.concatenate([uw[h][:, hd:].astype(BF16), qg_b[h]], axis=0), s_b[h]) for h in heads]
    vn_b = [(uw[h][:, :hd] - ws[h][:L]).astype(BF16) for h in heads]
    av = [_dot(attn_b[h], vn_b[h]) for h in heads]
    ks = [_dot_tn(kd_b[h], vn_b[h]) for h in heads]
    for h in heads:
        sl = slice(h * hd, (h + 1) * hd)
        o = ws[h][L:] + av[h]
        s_scr[h] = s_old[h] * jnp.exp(cum[L - 1:L, h:h + 1]) + ks[h]
        zh = z_ref[:, sl]
        on = (o * lax.rsqrt(jnp.mean(o * o, axis=-1, keepdims=True) + EPS)) * gn
        ob = on * (zh * _sigmoid(zh))
        m_ref[:, sl] = (_sigmoid(ga_ref[:, sl]) * oa_ref[:, sl] + _sigmoid(gb_ref[:, sl]) * ob).astype(BF16)

    @pl.when(c == pl.num_programs(1) - 1)
    def _():
        sout_ref[0] = s_scr[...]


def gdn(u, o_a, prev0, s0, conv_w, a_log, dt_bias, gn, b, s, col_z, col_sm, col_ga, col_gb, n_heads):
    hd = HEAD_DIM
    L = min(s, CHUNK)
    nc = s // L
    qkv_w = 3 * n_heads * hd
    row = lambda bi, c: bi * nc + c
    alog = jnp.zeros((1, LANES), F32).at[0, :n_heads].set(a_log)
    dtb = jnp.zeros((1, LANES), F32).at[0, :n_heads].set(dt_bias)
    return pl.pallas_call(
        functools.partial(_gdn_body, L=L, n_heads=n_heads),
        grid=(b, nc),
        in_specs=[pl.BlockSpec((L, qkv_w), lambda bi, c: (row(bi, c), 0)),
                  pl.BlockSpec((L, n_heads * hd), lambda bi, c: (row(bi, c), col_z)),
                  pl.BlockSpec((L, 512), lambda bi, c: (row(bi, c), col_sm)),
                  pl.BlockSpec((L, n_heads * hd), lambda bi, c: (row(bi, c), col_ga)),
                  pl.BlockSpec((L, n_heads * hd), lambda bi, c: (row(bi, c), col_gb)),
                  pl.BlockSpec((L, n_heads * hd), lambda bi, c: (row(bi, c), 0)),
                  pl.BlockSpec((1, SUBLANES, qkv_w), lambda bi, c: (bi, 0, 0)),
                  pl.BlockSpec((1, n_heads, hd, hd), lambda bi, c: (bi, 0, 0, 0)),
                  pl.BlockSpec((CONV_W, qkv_w), lambda bi, c: (0, 0)),
                  pl.BlockSpec((1, LANES), lambda bi, c: (0, 0)),
                  pl.BlockSpec((1, LANES), lambda bi, c: (0, 0)),
                  pl.BlockSpec((1, hd), lambda bi, c: (0, 0))],
        out_specs=[pl.BlockSpec((L, n_heads * hd), lambda bi, c: (row(bi, c), 0)),
                   pl.BlockSpec((1, n_heads, hd, hd), lambda bi, c: (bi, 0, 0, 0))],
        out_shape=[jax.ShapeDtypeStruct((b * s, n_heads * hd), BF16),
                   jax.ShapeDtypeStruct((b, n_heads, hd, hd), F32)],
        scratch_shapes=[pltpu.VMEM((n_heads, hd, hd), F32),
                        pltpu.VMEM((SUBLANES, qkv_w), F32),
                        pltpu.VMEM((L, qkv_w), F32)],
        compiler_params=_cparams(("parallel", "arbitrary")),
        name="gdn",
    )(u, u, u, u, u, o_a, prev0, s0, conv_w, alog, dtb, gn.reshape(1, hd))


def _out_proj_body(m_ref, x_ref, w_ref, *rest):
    o_ref = rest[-1]
    o_ref[...] = x_ref[...] + _dot(m_ref[...], w_ref[...])


def out_proj(m, x, w, tm, tn, n_total, row0, dest=None):
    n, d = x.shape
    assert row0 % tm == 0
    in_specs = [pl.BlockSpec((tm, d), lambda i, j: (i, 0)),
                pl.BlockSpec((tm, tn), lambda i, j: (i, j)),
                pl.BlockSpec((d, tn), lambda i, j: (0, j))]
    args = [m, x, w]
    aliases = {}
    if dest is not None:
        in_specs.append(pl.BlockSpec(memory_space=pl.ANY))
        args.append(dest)
        aliases = {3: 0}
    return pl.pallas_call(
        _out_proj_body,
        grid=(n // tm, d // tn),
        in_specs=in_specs,
        out_specs=pl.BlockSpec((tm, tn), lambda i, j: (i + row0 // tm, j)),
        out_shape=jax.ShapeDtypeStruct((n_total, d), F32),
        input_output_aliases=aliases,
        compiler_params=_cparams(("parallel", "arbitrary")),
        name="out_proj",
    )(*args)


def _route_body(x_ref, g_ref, whi_ref, wlo_ref, bias_ref, h_ref, eid_ref, wgt_ref, cnt_ref):
    x = x_ref[...]
    hn = (x * lax.rsqrt(jnp.mean(x * x, axis=-1, keepdims=True) + EPS)) * g_ref[...]
    h_hi, h_lo = _split_bf16(hn)
    h_ref[...] = h_hi
    lg = _dot3(h_hi, h_lo, whi_ref[...], wlo_ref[...]) + bias_ref[...]
    lane = lax.broadcasted_iota(jnp.int32, lg.shape, 1)
    big = jnp.int32(LANES)
    neg = -jnp.inf
    glog = jnp.where(lane < N_GROUPS, lg, neg)
    gmax = jnp.max(glog, axis=-1, keepdims=True)
    pg_top = 1.0 / jnp.sum(jnp.exp(glog - gmax), axis=-1, keepdims=True)
    grp = jnp.min(jnp.where(glog == gmax, lane, big), axis=-1, keepdims=True)
    lo = N_GROUPS + grp * EXPERTS_PER_GROUP
    le = jnp.where((lane >= lo) & (lane < lo + EXPERTS_PER_GROUP), lg, neg)
    v1 = jnp.max(le, axis=-1, keepdims=True)
    i1 = jnp.min(jnp.where(le == v1, lane, big), axis=-1, keepdims=True)
    le2 = jnp.where(lane == i1, neg, le)
    v2 = jnp.max(le2, axis=-1, keepdims=True)
    i2 = jnp.min(jnp.where(le2 == v2, lane, big), axis=-1, keepdims=True)
    e2 = jnp.exp(v2 - v1)
    den = 1.0 + e2
    w1 = pg_top * (1.0 / den)
    w2 = pg_top * (e2 / den)
    eid_ref[...] = jnp.where(lane == 0, i1 - N_GROUPS, jnp.where(lane == 1, i2 - N_GROUPS, 0))
    wgt_ref[...] = jnp.where(lane == 0, w1, jnp.where(lane == 1, w2, 0.0))
    hit = jnp.where((lane == i1 - N_GROUPS) | (lane == i2 - N_GROUPS), 1.0, 0.0)
    cnt_ref[0] = jnp.broadcast_to(jnp.sum(hit, axis=0, keepdims=True), cnt_ref.shape[1:]).astype(jnp.int32)


def route(x, g, whi, wlo, bias, tm):
    n, d = x.shape
    return pl.pallas_call(
        _route_body,
        grid=(n // tm,),
        in_specs=[pl.BlockSpec((tm, d), lambda i: (i, 0)),
                  pl.BlockSpec((1, d), lambda i: (0, 0)),
                  pl.BlockSpec((d, LANES), lambda i: (0, 0)),
                  pl.BlockSpec((d, LANES), lambda i: (0, 0)),
                  pl.BlockSpec((1, LANES), lambda i: (0, 0))],
        out_specs=[pl.BlockSpec((tm, d), lambda i: (i, 0)),
                   pl.BlockSpec((tm, LANES), lambda i: (i, 0)),
                   pl.BlockSpec((tm, LANES), lambda i: (i, 0)),
                   pl.BlockSpec((1, SUBLANES, LANES), lambda i: (i, 0, 0))],
        out_shape=[jax.ShapeDtypeStruct((n, d), BF16),
                   jax.ShapeDtypeStruct((n, LANES), jnp.int32),
                   jax.ShapeDtypeStruct((n, LANES), F32),
                   jax.ShapeDtypeStruct((n // tm, SUBLANES, LANES), jnp.int32)],
        compiler_params=_cparams(("parallel",)),
        name="route",
    )(x, g.reshape(1, d), whi, wlo, bias)


def _expert_body(be_tab, first_tab, nused, x_ref, sw_ref, wg_ref, wu_ref, wd_ref, o_ref, wg_s, wu_s, wd_s):
    blk = pl.program_id(0)

    @pl.when(blk < nused[0])
    def _():
        @pl.when(first_tab[blk] == 1)
        def _():
            wg_s[...] = wg_ref[0].astype(BF16)
            wu_s[...] = wu_ref[0].astype(BF16)
            wd_s[...] = wd_ref[0].astype(BF16)

        x = x_ref[...]
        gate = _dot(x, wg_s[...])
        up = _dot(x, wu_s[...])
        hid = (gate * _sigmoid(gate)) * up
        o_ref[...] = _dot(hid.astype(BF16), wd_s[...]) * sw_ref[...]

    @pl.when(blk >= nused[0])
    def _():
        o_ref[...] = jnp.zeros(o_ref.shape, F32)


def experts(x_sorted, slot_w, block_e, first, nused, w_gate, w_up, w_down, bm):
    n_slots, d = x_sorted.shape
    de = w_gate.shape[2]
    n_blocks = n_slots // bm
    grid_spec = pltpu.PrefetchScalarGridSpec(
        num_scalar_prefetch=3,
        grid=(n_blocks,),
        in_specs=[pl.BlockSpec((bm, d), lambda i, be, fi, nu: (i, 0)),
                  pl.BlockSpec((bm, 1), lambda i, be, fi, nu: (i, 0)),
                  pl.BlockSpec((1, d, de), lambda i, be, fi, nu: (be[i], 0, 0)),
                  pl.BlockSpec((1, d, de), lambda i, be, fi, nu: (be[i], 0, 0)),
                  pl.BlockSpec((1, de, d), lambda i, be, fi, nu: (be[i], 0, 0))],
        out_specs=pl.BlockSpec((bm, d), lambda i, be, fi, nu: (i, 0)),
        scratch_shapes=[pltpu.VMEM((d, de), BF16), pltpu.VMEM((d, de), BF16), pltpu.VMEM((de, d), BF16)],
    )
    return pl.pallas_call(
        _expert_body,
        grid_spec=grid_spec,
        out_shape=jax.ShapeDtypeStruct((n_slots, d), F32),
        compiler_params=_cparams(("arbitrary",)),
        name="experts",
    )(block_e, first, nused, x_sorted, slot_w, w_gate, w_up, w_down)


def _final_body(x_ref, ma_ref, mb_ref, g_ref, o_ref):
    x = x_ref[...] + (ma_ref[...] + mb_ref[...])
    o_ref[...] = (x * lax.rsqrt(jnp.mean(x * x, axis=-1, keepdims=True) + EPS)) * g_ref[...]


def final_norm(x, ma, mb, g, tm, n, row0):
    d = x.shape[1]
    assert row0 % tm == 0 and n % tm == 0
    spec = pl.BlockSpec((tm, d), lambda i: (i, 0))
    mspec = pl.BlockSpec((tm, d), lambda i: (i + row0 // tm, 0))
    return pl.pallas_call(
        _final_body,
        grid=(n // tm,),
        in_specs=[mspec, mspec, mspec, pl.BlockSpec((1, d), lambda i: (0, 0))],
        out_specs=spec,
        out_shape=jax.ShapeDtypeStruct((n, d), F32),
        compiler_params=_cparams(("parallel",)),
        name="final_norm",
    )(x, ma, mb, g.reshape(1, d))


def _tile(n, pref):
    t = min(n, pref)
    while n % t:
        t //= 2
    return t


def _rope_table(pos, b):
    half = ROPE_DIM // 2
    inv = ROPE_BASE ** (-jnp.arange(half, dtype=F32) / half)
    ang = pos.astype(F32)[:, None] * inv[None, :]
    cos, sin = jnp.cos(ang), jnp.sin(ang)
    cs = jnp.concatenate([cos, cos, sin, sin], axis=-1)
    return jnp.tile(cs, (b, 1))


def _rot_cols(w):
    half = ROPE_DIM // 2
    return jnp.concatenate([-w[..., half:], w[..., :half]], axis=-1)


def _moe(h_bf, eid, wgt, counts, w_gate, w_up, w_down, bm):
    n, d = h_bf.shape
    n_exp = w_gate.shape[0]
    a = n * TOP_K
    e_flat = eid.reshape(-1)
    w_flat = wgt.reshape(-1)
    padded = (counts + bm - 1) // bm * bm
    pend = jnp.cumsum(padded)
    n_blocks = (a + n_exp * (bm - 1) + bm - 1) // bm
    n_slots = n_blocks * bm
    block_e = jnp.minimum(jnp.searchsorted(pend, jnp.arange(n_blocks, dtype=jnp.int32) * bm, side='right'),
                          n_exp - 1).astype(jnp.int32)
    first = jnp.concatenate([jnp.ones((1,), jnp.int32), (block_e[1:] != block_e[:-1]).astype(jnp.int32)])
    nused = (pend[-1] // bm).astype(jnp.int32).reshape(1)
    n_pad = n_slots - a
    pad_end = jnp.cumsum(padded - counts)
    pad_key = jnp.searchsorted(pad_end, jnp.arange(n_pad, dtype=jnp.int32), side='right').astype(jnp.int32)
    keys = jnp.concatenate([e_flat, pad_key])
    toks = jnp.concatenate([jnp.arange(a, dtype=jnp.int32) // TOP_K, jnp.arange(n_pad, dtype=jnp.int32) % n])
    wts = jnp.concatenate([w_flat, jnp.zeros((n_pad,), F32)])
    iota = jnp.arange(n_slots, dtype=jnp.int32)
    _, slot_tok, slot_w, slot_src = lax.sort((keys, toks, wts, iota), num_keys=1, is_stable=True)
    x_sorted = h_bf[slot_tok]
    out = experts(x_sorted, slot_w.reshape(n_slots, 1), block_e, first, nused, w_gate, w_up, w_down, bm)
    _, slot_of = lax.sort((slot_src, iota), num_keys=1)
    slot_of = slot_of[:a].reshape(n, TOP_K)
    return out[slot_of[:, 0]], out[slot_of[:, 1]]


def _mixer(x3, pos, ckv_past, kr_past, s0, conv0, wp):
    b, s, d = x3.shape
    n = b * s
    n_heads = wp['n_heads']
    hd = HEAD_DIM
    x = x3.reshape(n, d)
    cols = wp['cols']

    u = rms_matmul(x, wp['attn_norm_g'], wp['w_all'], _tile(n, 1024), IN_PROJ_TN)

    cs = _rope_table(pos, b)
    scale = (hd + ROPE_DIM) ** -0.5 * float(np.log2(np.e))
    prompt = ckv_past is None
    t_att = _tile(s, ATTN_TILE)
    tm = min(_tile(s, 256), t_att)
    res = mla_prep(u, cs, wp['q_norm_g'], wp['kv_norm_g'], wp['wq_ext'], wp['wk'] if prompt else None,
                   wp['wvt'] if prompt else None, b, s, tm, t_att, cols['q_lat'] // 512, cols['kv_lat'] // 512,
                   cols['small'] // 512, n_heads, scale)
    ckv, kr_pad, q = res[:3]
    kr = kr_pad[:, :ROPE_DIM]
    lora = ckv.shape[1]
    if prompt:
        o_a = flash_prompt(q, res[3], res[4], t_att, ATTN_HEADS_PER_STEP).reshape(n, n_heads * hd)
    else:
        qn = q[..., :hd].transpose(1, 0, 2, 3).reshape(n_heads, n, hd)
        q_abs = head_matmul(qn, wp['w_uk_t'])
        q_abs = q_abs.reshape(n_heads, b, s, lora).transpose(1, 2, 0, 3).reshape(b, s * n_heads, lora)
        qr = q[..., hd:].transpose(0, 2, 1, 3).reshape(b, s * n_heads, hd)
        kr_past_pad = jnp.pad(kr_past, ((0, 0), (0, 0), (0, LANES - ROPE_DIM)))
        ckv_new = jnp.pad(ckv.reshape(b, s, lora), ((0, 0), (0, LANES - s), (0, 0)))
        kr_new = jnp.pad(kr_pad.reshape(b, s, LANES), ((0, 0), (0, LANES - s), (0, 0)))
        o_lat = attn_sample(q_abs, qr, ckv_past, kr_past_pad, ckv_new, kr_new, s, n_heads)
        o_lat = o_lat.reshape(b, s, n_heads, lora).transpose(2, 0, 1, 3).reshape(n_heads, n, lora)
        o_h = head_matmul(o_lat, wp['w_uv_h'])
        o_a = o_h.reshape(n_heads, n, hd).transpose(1, 0, 2).reshape(n, n_heads * hd)

    qkv_w = 3 * n_heads * hd
    prev0 = jnp.pad(conv0, ((0, 0), (SUBLANES - (CONV_W - 1), 0), (0, 0)))
    merged, s_new = gdn(u, o_a, prev0, s0, wp['conv_w'], wp['a_log'], wp['dt_bias'], wp['gdn_norm_g'], b, s,
                        cols['z'] // d, cols['small'] // 512, cols['gate_a'] // d, cols['gate_b'] // d, n_heads)
    tail = u.reshape(b, s, -1)[:, max(s - (CONV_W - 1), 0):, :qkv_w]
    conv_new = jnp.concatenate([conv0, tail], axis=1)[:, -(CONV_W - 1):]

    return (merged, x), ckv.reshape(b, s, lora), kr.reshape(b, s, ROPE_DIM), s_new, conv_new


def _prep_weights(l, attn_norm_g, w_in, q_norm_g, w_uq, kv_norm_g, w_uk, w_uv, conv_w, a_log, dt_bias, gdn_norm_g,
                  w_out, ffn_norm_g, w_group, b_group, w_router, b_router, w_gate, w_up, w_down):
    d = w_in.shape[1]
    q_lora, n_heads, _ = w_uq.shape[1:]
    kv_lora = w_uk.shape[1]
    hd = HEAD_DIM
    qkv_w = 3 * n_heads * hd
    sizes = (q_lora, kv_lora, ROPE_DIM, qkv_w, n_heads, n_heads, n_heads * hd, d, d)
    offs = np.concatenate([[0], np.cumsum(sizes)])
    wi = w_in[l]
    part = lambda i: wi[:, offs[i]:offs[i + 1]]
    w_qlat, w_kvlat, w_kr, w_qkv, w_a, w_b, w_z, w_ga, w_gb = [part(i) for i in range(9)]
    zpad = lambda c: jnp.zeros((d, c), F32)
    small = jnp.concatenate([w_kr, _rot_cols(w_kr), w_a, zpad(LANES - n_heads), w_b, zpad(LANES - n_heads),
                             zpad(LANES)], axis=1)
    w_all = jnp.concatenate([w_qkv, w_z, w_ga, w_gb, w_qlat, w_kvlat, small], axis=1).astype(BF16)
    cols = {'qkv': 0, 'z': qkv_w, 'gate_a': qkv_w + n_heads * hd, 'gate_b': qkv_w + n_heads * hd + d,
            'q_lat': qkv_w + n_heads * hd + 2 * d}
    cols['kv_lat'] = cols['q_lat'] + q_lora
    cols['small'] = cols['kv_lat'] + kv_lora
    wq = w_uq[l]
    wq_nope = wq[:, :, :hd].reshape(q_lora, n_heads * hd)
    wq_r = wq[:, :, hd:]
    wq_rope = jnp.concatenate([wq_r, _rot_cols(wq_r)], axis=-1).reshape(q_lora, n_heads * hd)
    wq_ext = jnp.concatenate([wq_nope, wq_rope], axis=1).astype(BF16)
    wk = w_uk[l].reshape(kv_lora, n_heads * hd).astype(BF16)
    wvt = w_uv[l].reshape(kv_lora, n_heads * hd).T.astype(BF16)
    wr = jnp.concatenate([w_group[l], w_router[l].transpose(1, 0, 2).reshape(d, -1)], axis=1)
    wr = jnp.pad(wr, ((0, 0), (0, LANES - wr.shape[1])))
    wr_hi = wr.astype(BF16)
    wr_lo = (wr - wr_hi.astype(F32)).astype(BF16)
    br = jnp.concatenate([b_group[l], b_router[l].reshape(-1)])
    br = jnp.pad(br, (0, LANES - br.shape[0])).reshape(1, LANES)
    return {
        'n_heads': n_heads, 'cols': cols, 'w_all': w_all, 'attn_norm_g': attn_norm_g[l],
        'q_norm_g': q_norm_g[l], 'kv_norm_g': kv_norm_g[l], 'wq_ext': wq_ext, 'wk': wk, 'wvt': wvt,
        'w_uk_t': w_uk[l].transpose(1, 2, 0).astype(BF16),
        'w_uv_h': w_uv[l].transpose(1, 0, 2).astype(BF16),
        'conv_w': conv_w[l], 'a_log': a_log[l], 'dt_bias': dt_bias[l], 'gdn_norm_g': gdn_norm_g[l],
        'w_out': w_out[l].astype(BF16), 'ffn_norm_g': ffn_norm_g[l], 'wr_hi': wr_hi, 'wr_lo': wr_lo, 'br': br,
        'w_gate': w_gate[l], 'w_up': w_up[l], 'w_down': w_down[l],
    }


def kernel(x_prompt, x_sample, cache_ckv, cache_k_rope, state_gdn, state_conv, attn_norm_g, w_in, q_norm_g, w_uq, kv_norm_g, w_uk, w_uv, conv_w, a_log, dt_bias, gdn_norm_g, w_out, ffn_norm_g, w_group, b_group, w_router, b_router, w_gate, w_up, w_down, final_norm_g):
    depth = w_in.shape[0]
    assert depth == 1, "final RMSNorm is fused into the last layer; deeper trunks are not supported"
    b_p, s_p, d = x_prompt.shape
    b_s, s_s, _ = x_sample.shape
    past = cache_ckv.shape[2]
    n_heads = w_uq.shape[2]
    qkv_w = 3 * n_heads * HEAD_DIM
    pos_p = jnp.arange(s_p, dtype=jnp.int32)
    pos_s = past + jnp.arange(s_s, dtype=jnp.int32)
    wp = _prep_weights(0, attn_norm_g, w_in, q_norm_g, w_uq, kv_norm_g, w_uk, w_uv, conv_w, a_log, dt_bias,
                       gdn_norm_g, w_out, ffn_norm_g, w_group, b_group, w_router, b_router, w_gate, w_up, w_down)
    s0 = jnp.zeros((b_p, n_heads, HEAD_DIM, HEAD_DIM), F32)
    c0 = jnp.zeros((b_p, CONV_W - 1, qkv_w), F32)
    (m_p, xr_p), ckv_p, kr_p, sg_p, sc_p = _mixer(x_prompt, pos_p, None, None, s0, c0, wp)
    (m_s, xr_s), ckv_s, kr_s, sg_s, sc_s = _mixer(x_sample, pos_s, cache_ckv[0], cache_k_rope[0], state_gdn[0],
                                                  state_conv[0], wp)

    n_p, n_s = b_p * s_p, b_s * s_s
    n_all = n_p + n_s
    tm_s = _tile(int(np.gcd(n_p, n_s)), 512)
    x1 = out_proj(m_p, xr_p, wp['w_out'], _tile(n_p, 1024), 512, n_all, 0)
    x1 = out_proj(m_s, xr_s, wp['w_out'], tm_s, 512, n_all, n_p, dest=x1)
    h_bf, eid, wgt, cnt = route(x1, wp['ffn_norm_g'], wp['wr_hi'], wp['wr_lo'], wp['br'], tm_s)
    n_exp = wp['w_gate'].shape[0]
    counts = cnt[:, 0, :n_exp].sum(axis=0)
    ma, mb = _moe(h_bf, eid[:, :TOP_K], wgt[:, :TOP_K], counts, wp['w_gate'], wp['w_up'], wp['w_down'], MOE_BLOCK)
    yp = final_norm(x1, ma, mb, final_norm_g, _tile(n_p, 512), n_p, 0).reshape(b_p, s_p, d)
    ys = final_norm(x1, ma, mb, final_norm_g, tm_s, n_s, n_p).reshape(b_s, s_s, d)
    return (yp, ys, ckv_p[None], kr_p[None], sg_p[None], sc_p[None], ckv_s[None], kr_s[None], sg_s[None], sc_s[None])
```

```python
import functools

import numpy as np
import jax
import jax.numpy as jnp
from jax import lax
from jax.experimental import pallas as pl
from jax.experimental.pallas import tpu as pltpu

F32 = jnp.float32
BF16 = jnp.bfloat16

EPS = 1e-6
CHUNK = 64
ROPE_BASE = 10000.0
HEAD_DIM = 128
ROPE_DIM = 64
CONV_W = 4
TOP_K = 2
N_GROUPS = 8
EXPERTS_PER_GROUP = 8

LANES = 128
SUBLANES = 8
BF16_ROWS = 16
VMEM_LIMIT = 56 * 1024 * 1024
ATTN_TILE = 512
ATTN_HEADS_PER_STEP = 4
MOE_BLOCK = 256
IN_PROJ_TN = 768


def _cparams(sem):
    return pltpu.CompilerParams(dimension_semantics=sem, vmem_limit_bytes=VMEM_LIMIT)


def _dot(a, b):
    return jnp.dot(a, b, preferred_element_type=F32)


def _dot_nt(a, b):
    return lax.dot_general(a, b, (((1,), (1,)), ((), ())), preferred_element_type=F32)


def _dot_tn(a, b):
    return lax.dot_general(a, b, (((0,), (0,)), ((), ())), preferred_element_type=F32)


def _sigmoid(x):
    return 1.0 / (1.0 + jnp.exp(-x))


def _split_bf16(a):
    hi = a.astype(BF16)
    lo = (a - hi.astype(F32)).astype(BF16)
    return hi, lo


def _dot3(a_hi, a_lo, b_hi, b_lo):
    return _dot(a_hi, b_hi) + (_dot(a_hi, b_lo) + _dot(a_lo, b_hi))


def _rms_mm_body(x_ref, g_ref, w_ref, o_ref, h_ref):
    @pl.when(pl.program_id(1) == 0)
    def _():
        x = x_ref[...]
        ms = jnp.mean(x * x, axis=-1, keepdims=True)
        h_ref[...] = ((x * lax.rsqrt(ms + EPS)) * g_ref[...]).astype(BF16)

    o_ref[...] = _dot(h_ref[...], w_ref[...])


def rms_matmul(x, g, w, tm, tn):
    n, k = x.shape
    m = w.shape[1]
    return pl.pallas_call(
        _rms_mm_body,
        grid=(n // tm, m // tn),
        in_specs=[pl.BlockSpec((tm, k), lambda i, j: (i, 0)),
                  pl.BlockSpec((1, k), lambda i, j: (0, 0)),
                  pl.BlockSpec((k, tn), lambda i, j: (0, j))],
        out_specs=pl.BlockSpec((tm, tn), lambda i, j: (i, j)),
        out_shape=jax.ShapeDtypeStruct((n, m), F32),
        scratch_shapes=[pltpu.VMEM((tm, k), BF16)],
        compiler_params=_cparams(("parallel", "arbitrary")),
        name="rms_matmul",
    )(x, g.reshape(1, k), w)


def _mla_prep_body(ql_ref, kvl_ref, sm_ref, cs_ref, gq_ref, gkv_ref, wq_ref, *rest, n_heads, scale, with_kv):
    if with_kv:
        wk_ref, wvt_ref, ckv_ref, kr_ref, q_ref, k_ref, vt_ref = rest
    else:
        ckv_ref, kr_ref, q_ref = rest
    hd = HEAD_DIM
    ql = ql_ref[...]
    qn = ((ql * lax.rsqrt(jnp.mean(ql * ql, axis=-1, keepdims=True) + EPS)) * gq_ref[...]).astype(BF16)
    q_all = _dot(qn, wq_ref[...])
    kvl = kvl_ref[...]
    ckv = (kvl * lax.rsqrt(jnp.mean(kvl * kvl, axis=-1, keepdims=True) + EPS)) * gkv_ref[...]
    ckv_ref[...] = ckv
    if with_kv:
        ckv_b = ckv.astype(BF16)
        k_all = _dot(ckv_b, wk_ref[...])
        vt_all = _dot_nt(wvt_ref[...], ckv_b)

    cs = cs_ref[...]
    lane = lax.broadcasted_iota(jnp.int32, cs.shape, 1)

    def rope(t):
        t = t * cs
        return jnp.where(lane < ROPE_DIM, t + pltpu.roll(t, ROPE_DIM, axis=1), 0.0)

    kr = rope(sm_ref[:, 0:LANES])
    kr_ref[...] = kr
    kr_b = kr.astype(BF16)
    off = n_heads * hd
    for h in range(n_heads):
        q_ref[0, h, :, 0:hd] = (q_all[:, h * hd:(h + 1) * hd] * scale).astype(BF16)
        q_ref[0, h, :, hd:2 * hd] = (rope(q_all[:, off + h * hd:off + (h + 1) * hd]) * scale).astype(BF16)
        if with_kv:
            k_ref[0, h, :, 0:hd] = k_all[:, h * hd:(h + 1) * hd].astype(BF16)
            k_ref[0, h, :, hd:2 * hd] = kr_b
            vt_ref[0, h, 0, 0:hd, :] = vt_all[h * hd:(h + 1) * hd, :].astype(BF16)
            vt_ref[0, h, 0, hd:hd + BF16_ROWS, :] = jnp.ones((BF16_ROWS, vt_all.shape[1]), BF16)


def mla_prep(u, cs, gq, gkv, wq, wk, wvt, b, s, tm, t_att, col_ql, col_kvl, col_sm, n_heads, scale):
    lora = wq.shape[0]
    nt = s // tm
    row = lambda bi, si: bi * nt + si
    hd = HEAD_DIM
    with_kv = wk is not None
    full = lambda bi, si: (0, 0)
    in_specs = [pl.BlockSpec((tm, lora), lambda bi, si: (row(bi, si), col_ql)),
                pl.BlockSpec((tm, lora), lambda bi, si: (row(bi, si), col_kvl)),
                pl.BlockSpec((tm, 512), lambda bi, si: (row(bi, si), col_sm)),
                pl.BlockSpec((tm, LANES), lambda bi, si: (row(bi, si), 0)),
                pl.BlockSpec((1, lora), full),
                pl.BlockSpec((1, lora), full),
                pl.BlockSpec(wq.shape, full)]
    out_specs = [pl.BlockSpec((tm, lora), lambda bi, si: (row(bi, si), 0)),
                 pl.BlockSpec((tm, LANES), lambda bi, si: (row(bi, si), 0)),
                 pl.BlockSpec((1, n_heads, tm, 2 * hd), lambda bi, si: (bi, 0, si, 0))]
    out_shape = [jax.ShapeDtypeStruct((b * s, lora), F32),
                 jax.ShapeDtypeStruct((b * s, LANES), F32),
                 jax.ShapeDtypeStruct((b, n_heads, s, 2 * hd), BF16)]
    args = [u, u, u, cs, gq.reshape(1, lora), gkv.reshape(1, lora), wq]
    if with_kv:
        r = t_att // tm
        in_specs += [pl.BlockSpec(wk.shape, full), pl.BlockSpec(wvt.shape, full)]
        out_specs += [pl.BlockSpec((1, n_heads, tm, 2 * hd), lambda bi, si: (bi, 0, si, 0)),
                      pl.BlockSpec((1, n_heads, 1, hd + BF16_ROWS, tm),
                                   lambda bi, si: (bi, 0, si // r, 0, si % r))]
        out_shape += [jax.ShapeDtypeStruct((b, n_heads, s, 2 * hd), BF16),
                      jax.ShapeDtypeStruct((b, n_heads, s // t_att, hd + BF16_ROWS, t_att), BF16)]
        args += [wk, wvt]
    return pl.pallas_call(
        functools.partial(_mla_prep_body, n_heads=n_heads, scale=scale, with_kv=with_kv),
        grid=(b, nt),
        in_specs=in_specs,
        out_specs=out_specs,
        out_shape=out_shape,
        compiler_params=_cparams(("parallel", "parallel")),
        name="mla_prep",
    )(*args)


def _flash_body(q_ref, k_ref, vt_ref, o_ref, m_scr, acc_scr, *, t, gh):
    qi = pl.program_id(2)
    hd = HEAD_DIM
    m_scr[...] = jnp.full(m_scr.shape, -jnp.inf, F32)
    acc_scr[...] = jnp.zeros(acc_scr.shape, F32)

    def block(j, masked):
        start = pl.multiple_of(j * t, t)
        scs = [_dot_nt(k_ref[0, g, pl.ds(start, t), :], q_ref[0, g]) for g in range(gh)]
        for g in range(gh):
            sc = scs[g]
            if masked:
                krow = lax.broadcasted_iota(jnp.int32, sc.shape, 0)
                qcol = lax.broadcasted_iota(jnp.int32, sc.shape, 1)
                sc = jnp.where(krow // CHUNK <= qcol // CHUNK, sc, -jnp.inf)
            m_prev = m_scr[g]
            m_new = jnp.maximum(m_prev, jnp.max(sc, axis=0, keepdims=True))
            alpha = jnp.exp2(m_prev - m_new)
            p = jnp.exp2(sc - m_new)
            acc_scr[g] = alpha * acc_scr[g] + _dot(vt_ref[0, g, j], p.astype(BF16))
            m_scr[g] = m_new

    def full_block(j, carry):
        block(j, False)
        return carry

    lax.fori_loop(0, qi, full_block, 0)
    block(qi, True)
    for g in range(gh):
        acc = acc_scr[g]
        o_ref[0, :, g * hd:(g + 1) * hd] = jnp.transpose(acc[:hd] / acc[hd:hd + 1])


def flash_prompt(q, k, vt, t, gh):
    b, nh, s, dk = q.shape
    hd = HEAD_DIM
    hv = vt.shape[-2]
    assert t % CHUNK == 0 and s % t == 0 and nh % gh == 0 and vt.shape[-1] == t and hv > hd
    return pl.pallas_call(
        functools.partial(_flash_body, t=t, gh=gh),
        grid=(b, nh // gh, s // t),
        in_specs=[pl.BlockSpec((1, gh, t, dk), lambda bi, hp, qi: (bi, hp, qi, 0)),
                  pl.BlockSpec((1, gh, s, dk), lambda bi, hp, qi: (bi, hp, 0, 0)),
                  pl.BlockSpec((1, gh, s // t, hv, t), lambda bi, hp, qi: (bi, hp, 0, 0, 0))],
        out_specs=pl.BlockSpec((1, t, gh * hd), lambda bi, hp, qi: (bi, qi, hp)),
        out_shape=jax.ShapeDtypeStruct((b, s, nh * hd), F32),
        scratch_shapes=[pltpu.VMEM((gh, 1, t), F32), pltpu.VMEM((gh, hv, t), F32)],
        compiler_params=_cparams(("parallel", "parallel", "arbitrary")),
        name="flash_prompt",
    )(q, k, vt)


def _bmm_body(a_ref, b_ref, o_ref):
    o_ref[0] = _dot(a_ref[0].astype(BF16), b_ref[0])


def head_matmul(a, b):
    nh, m, k = a.shape
    n = b.shape[2]
    return pl.pallas_call(
        _bmm_body,
        grid=(nh,),
        in_specs=[pl.BlockSpec((1, m, k), lambda h: (h, 0, 0)),
                  pl.BlockSpec((1, k, n), lambda h: (h, 0, 0))],
        out_specs=pl.BlockSpec((1, m, n), lambda h: (h, 0, 0)),
        out_shape=jax.ShapeDtypeStruct((nh, m, n), F32),
        compiler_params=_cparams(("parallel",)),
        name="head_matmul",
    )(a, b)


def _attn_sample_body(qa_ref, qr_ref, kp_ref, krp_ref, kn_ref, krn_ref, o_ref, *, past, s_new, n_heads):
    qa = qa_ref[0].astype(BF16)
    qr = qr_ref[0].astype(BF16)
    kp = kp_ref[0].astype(BF16)
    krp = krp_ref[0].astype(BF16)
    kn = kn_ref[0].astype(BF16)
    krn = krn_ref[0].astype(BF16)
    s_past = _dot_nt(qa, kp) + _dot_nt(qr, krp)
    s_n = _dot_nt(qa, kn) + _dot_nt(qr, krn)
    row = lax.broadcasted_iota(jnp.int32, s_n.shape, 0)
    col = lax.broadcasted_iota(jnp.int32, s_n.shape, 1)
    qpos = past + row // n_heads
    kpos = past + col
    valid = (col < s_new) & (kpos // CHUNK <= qpos // CHUNK)
    s_n = jnp.where(valid, s_n, -jnp.inf)
    m = jnp.maximum(jnp.max(s_past, axis=-1, keepdims=True), jnp.max(s_n, axis=-1, keepdims=True))
    pp = jnp.exp2(s_past - m)
    pn = jnp.exp2(s_n - m)
    l = jnp.sum(pp, axis=-1, keepdims=True) + jnp.sum(pn, axis=-1, keepdims=True)
    o_ref[0] = (_dot(pp.astype(BF16), kp) + _dot(pn.astype(BF16), kn)) / l


def attn_sample(qa, qr, ckv_past, kr_past, ckv_new, kr_new, s_new, n_heads):
    b, r, lora = qa.shape
    past = ckv_past.shape[1]
    return pl.pallas_call(
        functools.partial(_attn_sample_body, past=past, s_new=s_new, n_heads=n_heads),
        grid=(b,),
        in_specs=[pl.BlockSpec((1, r, lora), lambda i: (i, 0, 0)),
                  pl.BlockSpec((1, r, LANES), lambda i: (i, 0, 0)),
                  pl.BlockSpec((1, past, lora), lambda i: (i, 0, 0)),
                  pl.BlockSpec((1, past, LANES), lambda i: (i, 0, 0)),
                  pl.BlockSpec((1, LANES, lora), lambda i: (i, 0, 0)),
                  pl.BlockSpec((1, LANES, LANES), lambda i: (i, 0, 0))],
        out_specs=pl.BlockSpec((1, r, lora), lambda i: (i, 0, 0)),
        out_shape=jax.ShapeDtypeStruct((b, r, lora), F32),
        compiler_params=_cparams(("parallel",)),
        name="attn_sample",
    )(qa, qr, ckv_past, kr_past, ckv_new, kr_new)


def _softplus(x):
    return jnp.maximum(x, 0.0) + jnp.log1p(jnp.exp(-jnp.abs(x)))


def _gdn_body(qkv_ref, z_ref, sm_ref, ga_ref, gb_ref, oa_ref, prev0_ref, s0_ref, convw_ref, alog_ref, dtb_ref,
              gn_ref, m_ref, sout_ref, s_scr, prev_scr, y_scr, *, L, n_heads):
    hd = HEAD_DIM
    heads = range(n_heads)
    c = pl.program_id(1)

    @pl.when(c == 0)
    def _():
        s_scr[...] = s0_ref[0]
        prev_scr[...] = prev0_ref[0]

    x = qkv_ref[...]
    xp = jnp.concatenate([prev_scr[...], x], axis=0)
    base = SUBLANES - (CONV_W - 1)
    y = xp[base:base + L] * convw_ref[0:1, :]
    for j in range(1, CONV_W):
        y = y + xp[base + j:base + j + L] * convw_ref[j:j + 1, :]
    y_scr[...] = y * _sigmoid(y)
    prev_scr[...] = x[L - SUBLANES:L]

    a = sm_ref[:, LANES:2 * LANES]
    bb = sm_ref[:, 2 * LANES:3 * LANES]
    g = -jnp.exp(alog_ref[...]) * _softplus(a + dtb_ref[...])
    beta = _sigmoid(bb)
    rowl = lax.broadcasted_iota(jnp.int32, g.shape, 0)
    cum = g
    d = 1
    while d < L:
        cum = cum + jnp.where(rowl >= d, pltpu.roll(cum, d, axis=0), 0.0)
        d *= 2
    cum_t = jnp.transpose(jnp.concatenate([cum, jnp.zeros((LANES - L, LANES), F32)], axis=0))

    row = lax.broadcasted_iota(jnp.int32, (L, L), 0)
    col = lax.broadcasted_iota(jnp.int32, (L, L), 1)
    tri = row >= col
    strict = row > col
    eye = jnp.where(row == col, 1.0, 0.0)
    gn = gn_ref[...]

    q_b, k_b, kb_b, rhs_b, qg_b, kd_b = [], [], [], [], [], []
    for h in heads:
        qh = y_scr[:, h * hd:(h + 1) * hd]
        kh = y_scr[:, (n_heads + h) * hd:(n_heads + h + 1) * hd]
        vh = y_scr[:, (2 * n_heads + h) * hd:(2 * n_heads + h + 1) * hd]
        qh = (qh * lax.rsqrt(jnp.sum(qh * qh, axis=-1, keepdims=True) + EPS)) * (hd ** -0.5)
        kh = kh * lax.rsqrt(jnp.sum(kh * kh, axis=-1, keepdims=True) + EPS)
        gcol = cum[:, h:h + 1]
        bcol = beta[:, h:h + 1]
        e_g = jnp.exp(gcol)
        kb = kh * bcol
        q_b.append(qh.astype(BF16))
        k_b.append(kh.astype(BF16))
        kb_b.append(kb.astype(BF16))
        rhs_b.append(jnp.concatenate([vh * bcol, kb * e_g], axis=1).astype(BF16))
        qg_b.append((qh * e_g).astype(BF16))
        kd_b.append((kh * jnp.exp(cum[L - 1:L, h:h + 1] - gcol)).astype(BF16))

    kk = [_dot_nt(kb_b[h], k_b[h]) for h in heads]
    qk = [_dot_nt(q_b[h], k_b[h]) for h in heads]

    nmat, attn_b = [], []
    for h in heads:
        diff = cum[:, h:h + 1] - cum_t[h:h + 1, 0:L]
        decay = jnp.where(tri, jnp.exp(jnp.where(tri, diff, 0.0)), 0.0)
        nmat.append(jnp.where(strict, kk[h] * decay, 0.0))
        attn_b.append(jnp.where(tri, qk[h] * decay, 0.0).astype(BF16))

    tmat = [eye - nmat[h] for h in heads]
    p_b = [nmat[h].astype(BF16) for h in heads]
    pmat = [_dot(p_b[h], p_b[h]) for h in heads]
    power = 2
    while True:
        p_b = [pmat[h].astype(BF16) for h in heads]
        tmat = [tmat[h] + _dot(tmat[h].astype(BF16), p_b[h]) for h in heads]
        power *= 2
        if power >= L:
            break
        pmat = [_dot(p_b[h], p_b[h]) for h in heads]

    uw = [_dot(tmat[h].astype(BF16), rhs_b[h]) for h in heads]
    s_old = [s_scr[h] for h in heads]
    s_b = [s_old[h].astype(BF16) for h in heads]
    ws = [_dot(jnp.concatenate([uw[h][:, hd:].astype(BF16), qg_b[h]], axis=0), s_b[h]) for h in heads]
    vn_b = [(uw[h][:, :hd] - ws[h][:L]).astype(BF16) for h in heads]
    av = [_dot(attn_b[h], vn_b[h]) for h in heads]
    ks = [_dot_tn(kd_b[h], vn_b[h]) for h in heads]
    for h in heads:
        sl = slice(h * hd, (h + 1) * hd)
        o = ws[h][L:] + av[h]
        s_scr[h] = s_old[h] * jnp.exp(cum[L - 1:L, h:h + 1]) + ks[h]
        zh = z_ref[:, sl]
        on = (o * lax.rsqrt(jnp.mean(o * o, axis=-1, keepdims=True) + EPS)) * gn
        ob = on * (zh * _sigmoid(zh))
        m_ref[:, sl] = (_sigmoid(ga_ref[:, sl]) * oa_ref[:, sl] + _sigmoid(gb_ref[:, sl]) * ob).astype(BF16)

    @pl.when(c == pl.num_programs(1) - 1)
    def _():
        sout_ref[0] = s_scr[...]


def gdn(u, o_a, prev0, s0, conv_w, a_log, dt_bias, gn, b, s, col_z, col_sm, col_ga, col_gb, n_heads):
    hd = HEAD_DIM
    L = min(s, CHUNK)
    nc = s // L
    qkv_w = 3 * n_heads * hd
    row = lambda bi, c: bi * nc + c
    alog = jnp.zeros((1, LANES), F32).at[0, :n_heads].set(a_log)
    dtb = jnp.zeros((1, LANES), F32).at[0, :n_heads].set(dt_bias)
    return pl.pallas_call(
        functools.partial(_gdn_body, L=L, n_heads=n_heads),
        grid=(b, nc),
        in_specs=[pl.BlockSpec((L, qkv_w), lambda bi, c: (row(bi, c), 0)),
                  pl.BlockSpec((L, n_heads * hd), lambda bi, c: (row(bi, c), col_z)),
                  pl.BlockSpec((L, 512), lambda bi, c: (row(bi, c), col_sm)),
                  pl.BlockSpec((L, n_heads * hd), lambda bi, c: (row(bi, c), col_ga)),
                  pl.BlockSpec((L, n_heads * hd), lambda bi, c: (row(bi, c), col_gb)),
                  pl.BlockSpec((L, n_heads * hd), lambda bi, c: (row(bi, c), 0)),
                  pl.BlockSpec((1, SUBLANES, qkv_w), lambda bi, c: (bi, 0, 0)),
                  pl.BlockSpec((1, n_heads, hd, hd), lambda bi, c: (bi, 0, 0, 0)),
                  pl.BlockSpec((CONV_W, qkv_w), lambda bi, c: (0, 0)),
                  pl.BlockSpec((1, LANES), lambda bi, c: (0, 0)),
                  pl.BlockSpec((1, LANES), lambda bi, c: (0, 0)),
                  pl.BlockSpec((1, hd), lambda bi, c: (0, 0))],
        out_specs=[pl.BlockSpec((L, n_heads * hd), lambda bi, c: (row(bi, c), 0)),
                   pl.BlockSpec((1, n_heads, hd, hd), lambda bi, c: (bi, 0, 0, 0))],
        out_shape=[jax.ShapeDtypeStruct((b * s, n_heads * hd), BF16),
                   jax.ShapeDtypeStruct((b, n_heads, hd, hd), F32)],
        scratch_shapes=[pltpu.VMEM((n_heads, hd, hd), F32),
                        pltpu.VMEM((SUBLANES, qkv_w), F32),
                        pltpu.VMEM((L, qkv_w), F32)],
        compiler_params=_cparams(("parallel", "arbitrary")),
        name="gdn",
    )(u, u, u, u, u, o_a, prev0, s0, conv_w, alog, dtb, gn.reshape(1, hd))


def _out_proj_body(m_ref, x_ref, w_ref, o_ref):
    o_ref[...] = x_ref[...] + _dot(m_ref[...], w_ref[...])


def out_proj(m, x, w, tm, tn):
    n, d = x.shape
    return pl.pallas_call(
        _out_proj_body,
        grid=(n // tm, d // tn),
        in_specs=[pl.BlockSpec((tm, d), lambda i, j: (i, 0)),
                  pl.BlockSpec((tm, tn), lambda i, j: (i, j)),
                  pl.BlockSpec((d, tn), lambda i, j: (0, j))],
        out_specs=pl.BlockSpec((tm, tn), lambda i, j: (i, j)),
        out_shape=jax.ShapeDtypeStruct((n, d), F32),
        compiler_params=_cparams(("parallel", "arbitrary")),
        name="out_proj",
    )(m, x, w)


def _route_body(xa_ref, xb_ref, g_ref, whi_ref, wlo_ref, bias_ref, h_ref, eid_ref, wgt_ref, cnt_ref, *, tiles_a):
    x = jnp.where(pl.program_id(0) < tiles_a, xa_ref[...], xb_ref[...])
    hn = (x * lax.rsqrt(jnp.mean(x * x, axis=-1, keepdims=True) + EPS)) * g_ref[...]
    h_hi, h_lo = _split_bf16(hn)
    h_ref[...] = h_hi
    lg = _dot3(h_hi, h_lo, whi_ref[...], wlo_ref[...]) + bias_ref[...]
    lane = lax.broadcasted_iota(jnp.int32, lg.shape, 1)
    big = jnp.int32(LANES)
    neg = -jnp.inf
    glog = jnp.where(lane < N_GROUPS, lg, neg)
    gmax = jnp.max(glog, axis=-1, keepdims=True)
    pg_top = 1.0 / jnp.sum(jnp.exp(glog - gmax), axis=-1, keepdims=True)
    grp = jnp.min(jnp.where(glog == gmax, lane, big), axis=-1, keepdims=True)
    lo = N_GROUPS + grp * EXPERTS_PER_GROUP
    le = jnp.where((lane >= lo) & (lane < lo + EXPERTS_PER_GROUP), lg, neg)
    v1 = jnp.max(le, axis=-1, keepdims=True)
    i1 = jnp.min(jnp.where(le == v1, lane, big), axis=-1, keepdims=True)
    le2 = jnp.where(lane == i1, neg, le)
    v2 = jnp.max(le2, axis=-1, keepdims=True)
    i2 = jnp.min(jnp.where(le2 == v2, lane, big), axis=-1, keepdims=True)
    e2 = jnp.exp(v2 - v1)
    den = 1.0 + e2
    w1 = pg_top * (1.0 / den)
    w2 = pg_top * (e2 / den)
    eid_ref[...] = jnp.where(lane == 0, i1 - N_GROUPS, jnp.where(lane == 1, i2 - N_GROUPS, 0))
    wgt_ref[...] = jnp.where(lane == 0, w1, jnp.where(lane == 1, w2, 0.0))
    hit = jnp.where((lane == i1 - N_GROUPS) | (lane == i2 - N_GROUPS), 1.0, 0.0)
    cnt_ref[0] = jnp.broadcast_to(jnp.sum(hit, axis=0, keepdims=True), cnt_ref.shape[1:]).astype(jnp.int32)


def route(xa, xb, g, whi, wlo, bias, tm):
    d = xa.shape[1]
    ta, tb = xa.shape[0] // tm, xb.shape[0] // tm
    n = (ta + tb) * tm
    return pl.pallas_call(
        functools.partial(_route_body, tiles_a=ta),
        grid=(ta + tb,),
        in_specs=[pl.BlockSpec((tm, d), lambda i: (jnp.minimum(i, ta - 1), 0)),
                  pl.BlockSpec((tm, d), lambda i: (jnp.maximum(i - ta, 0), 0)),
                  pl.BlockSpec((1, d), lambda i: (0, 0)),
                  pl.BlockSpec((d, LANES), lambda i: (0, 0)),
                  pl.BlockSpec((d, LANES), lambda i: (0, 0)),
                  pl.BlockSpec((1, LANES), lambda i: (0, 0))],
        out_specs=[pl.BlockSpec((tm, d), lambda i: (i, 0)),
                   pl.BlockSpec((tm, LANES), lambda i: (i, 0)),
                   pl.BlockSpec((tm, LANES), lambda i: (i, 0)),
                   pl.BlockSpec((1, SUBLANES, LANES), lambda i: (i, 0, 0))],
        out_shape=[jax.ShapeDtypeStruct((n, d), BF16),
                   jax.ShapeDtypeStruct((n, LANES), jnp.int32),
                   jax.ShapeDtypeStruct((n, LANES), F32),
                   jax.ShapeDtypeStruct((n // tm, SUBLANES, LANES), jnp.int32)],
        compiler_params=_cparams(("parallel",)),
        name="route",
    )(xa, xb, g.reshape(1, d), whi, wlo, bias)


def _expert_body(be_tab, first_tab, slot_tab, next_tab, nused, x_ref, sw_ref, wg_hbm, wu_hbm, wd_hbm, o_ref,
                 stg_g, stg_u, stg_d, wg_s, wu_s, wd_s, sem):
    blk = pl.program_id(0)

    def weight_copies(e, slot):
        return (pltpu.make_async_copy(wg_hbm.at[e], stg_g.at[slot], sem.at[slot, 0]),
                pltpu.make_async_copy(wu_hbm.at[e], stg_u.at[slot], sem.at[slot, 1]),
                pltpu.make_async_copy(wd_hbm.at[e], stg_d.at[slot], sem.at[slot, 2]))

    @pl.when((blk == 0) & (nused[0] > 0))
    def _():
        for cp in weight_copies(be_tab[0], 0):
            cp.start()

    @pl.when(blk < nused[0])
    def _():
        @pl.when(first_tab[blk] == 1)
        def _():
            slot = slot_tab[blk]
            for cp in weight_copies(be_tab[blk], slot):
                cp.wait()
            wg_s[...] = stg_g[slot].astype(BF16)
            wu_s[...] = stg_u[slot].astype(BF16)
            wd_s[...] = stg_d[slot].astype(BF16)

            @pl.when(next_tab[blk] >= 0)
            def _():
                for cp in weight_copies(next_tab[blk], 1 - slot):
                    cp.start()

        x = x_ref[...]
        gate = _dot(x, wg_s[...])
        up = _dot(x, wu_s[...])
        hid = (gate * _sigmoid(gate)) * up
        o_ref[...] = _dot(hid.astype(BF16), wd_s[...]) * sw_ref[...]

    @pl.when(blk >= nused[0])
    def _():
        o_ref[...] = jnp.zeros(o_ref.shape, F32)


def experts(x_sorted, slot_w, block_e, first, slot_tab, next_tab, nused, w_gate, w_up, w_down, bm):
    n_slots, d = x_sorted.shape
    de = w_gate.shape[2]
    n_blocks = n_slots // bm
    row = lambda i, *_: (i, 0)
    grid_spec = pltpu.PrefetchScalarGridSpec(
        num_scalar_prefetch=5,
        grid=(n_blocks,),
        in_specs=[pl.BlockSpec((bm, d), row),
                  pl.BlockSpec((bm, 1), row),
                  pl.BlockSpec(memory_space=pl.ANY),
                  pl.BlockSpec(memory_space=pl.ANY),
                  pl.BlockSpec(memory_space=pl.ANY)],
        out_specs=pl.BlockSpec((bm, d), row),
        scratch_shapes=[pltpu.VMEM((2, d, de), F32), pltpu.VMEM((2, d, de), F32), pltpu.VMEM((2, de, d), F32),
                        pltpu.VMEM((d, de), BF16), pltpu.VMEM((d, de), BF16), pltpu.VMEM((de, d), BF16),
                        pltpu.SemaphoreType.DMA((2, 3))],
    )
    return pl.pallas_call(
        _expert_body,
        grid_spec=grid_spec,
        out_shape=jax.ShapeDtypeStruct((n_slots, d), F32),
        compiler_params=_cparams(("arbitrary",)),
        name="experts",
    )(block_e, first, slot_tab, next_tab, nused, x_sorted, slot_w, w_gate, w_up, w_down)


def _final_body(x_ref, ma_ref, mb_ref, g_ref, o_ref):
    x = x_ref[...] + (ma_ref[...] + mb_ref[...])
    o_ref[...] = (x * lax.rsqrt(jnp.mean(x * x, axis=-1, keepdims=True) + EPS)) * g_ref[...]


def final_norm(x, ma, mb, g, tm, row0):
    n, d = x.shape
    assert row0 % tm == 0 and n % tm == 0
    spec = pl.BlockSpec((tm, d), lambda i: (i, 0))
    mspec = pl.BlockSpec((tm, d), lambda i: (i + row0 // tm, 0))
    return pl.pallas_call(
        _final_body,
        grid=(n // tm,),
        in_specs=[spec, mspec, mspec, pl.BlockSpec((1, d), lambda i: (0, 0))],
        out_specs=spec,
        out_shape=jax.ShapeDtypeStruct((n, d), F32),
        compiler_params=_cparams(("parallel",)),
        name="final_norm",
    )(x, ma, mb, g.reshape(1, d))


def _tile(n, pref):
    t = min(n, pref)
    while n % t:
        t //= 2
    return t


def _rope_table(pos, b):
    half = ROPE_DIM // 2
    inv = ROPE_BASE ** (-jnp.arange(half, dtype=F32) / half)
    ang = pos.astype(F32)[:, None] * inv[None, :]
    cos, sin = jnp.cos(ang), jnp.sin(ang)
    cs = jnp.concatenate([cos, cos, sin, sin], axis=-1)
    return jnp.tile(cs, (b, 1))


def _rot_cols(w):
    half = ROPE_DIM // 2
    return jnp.concatenate([-w[..., half:], w[..., :half]], axis=-1)


def _moe(h_bf, eid, wgt, counts, w_gate, w_up, w_down, bm):
    n, d = h_bf.shape
    n_exp = w_gate.shape[0]
    a = n * TOP_K
    e_flat = eid.reshape(-1)
    w_flat = wgt.reshape(-1)
    padded = (counts + bm - 1) // bm * bm
    pend = jnp.cumsum(padded)
    n_blocks = (a + n_exp * (bm - 1) + bm - 1) // bm
    n_slots = n_blocks * bm
    block_e = jnp.minimum(jnp.searchsorted(pend, jnp.arange(n_blocks, dtype=jnp.int32) * bm, side='right'),
                          n_exp - 1).astype(jnp.int32)
    first = jnp.concatenate([jnp.ones((1,), jnp.int32), (block_e[1:] != block_e[:-1]).astype(jnp.int32)])
    nused = (pend[-1] // bm).astype(jnp.int32).reshape(1)
    slot_tab = (jnp.cumsum(first) - 1) % 2
    e_ids = jnp.arange(n_exp, dtype=jnp.int32)
    later = lax.cummin(jnp.where(counts > 0, e_ids, n_exp), axis=0, reverse=True)
    next_e = jnp.concatenate([later[1:], jnp.full((1,), n_exp, jnp.int32)])
    next_tab = jnp.where(next_e < n_exp, next_e, -1)[block_e].astype(jnp.int32)
    n_pad = n_slots - a
    pad_end = jnp.cumsum(padded - counts)
    pad_key = jnp.sum(jnp.arange(n_pad, dtype=jnp.int32)[None, :] >= pad_end[:, None], axis=0, dtype=jnp.int32)
    keys = jnp.concatenate([e_flat, pad_key])
    toks = jnp.concatenate([jnp.arange(a, dtype=jnp.int32) // TOP_K, jnp.arange(n_pad, dtype=jnp.int32) % n])
    wts = jnp.concatenate([w_flat, jnp.zeros((n_pad,), F32)])
    iota = jnp.arange(n_slots, dtype=jnp.int32)
    _, slot_tok, slot_w, slot_src = lax.sort((keys, toks, wts, iota), num_keys=1, is_stable=True)
    x_sorted = h_bf[slot_tok]
    out = experts(x_sorted, slot_w.reshape(n_slots, 1), block_e, first, slot_tab.astype(jnp.int32), next_tab, nused,
                  w_gate, w_up, w_down, bm)
    _, slot_of = lax.sort((slot_src, iota), num_keys=1)
    slot_of = slot_of[:a].reshape(n, TOP_K)
    return out[slot_of[:, 0]], out[slot_of[:, 1]]


def _mixer(x3, pos, ckv_past, kr_past, s0, conv0, wp):
    b, s, d = x3.shape
    n = b * s
    n_heads = wp['n_heads']
    hd = HEAD_DIM
    x = x3.reshape(n, d)
    cols = wp['cols']

    u = rms_matmul(x, wp['attn_norm_g'], wp['w_all'], _tile(n, 1024), IN_PROJ_TN)

    cs = _rope_table(pos, b)
    scale = (hd + ROPE_DIM) ** -0.5 * float(np.log2(np.e))
    prompt = ckv_past is None
    t_att = _tile(s, ATTN_TILE)
    tm = min(_tile(s, 256), t_att)
    res = mla_prep(u, cs, wp['q_norm_g'], wp['kv_norm_g'], wp['wq_ext'], wp['wk'] if prompt else None,
                   wp['wvt'] if prompt else None, b, s, tm, t_att, cols['q_lat'] // 512, cols['kv_lat'] // 512,
                   cols['small'] // 512, n_heads, scale)
    ckv, kr_pad, q = res[:3]
    kr = kr_pad[:, :ROPE_DIM]
    lora = ckv.shape[1]
    if prompt:
        o_a = flash_prompt(q, res[3], res[4], t_att, ATTN_HEADS_PER_STEP).reshape(n, n_heads * hd)
    else:
        qn = q[..., :hd].transpose(1, 0, 2, 3).reshape(n_heads, n, hd)
        q_abs = head_matmul(qn, wp['w_uk_t'])
        q_abs = q_abs.reshape(n_heads, b, s, lora).transpose(1, 2, 0, 3).reshape(b, s * n_heads, lora)
        qr = q[..., hd:].transpose(0, 2, 1, 3).reshape(b, s * n_heads, hd)
        kr_past_pad = jnp.pad(kr_past, ((0, 0), (0, 0), (0, LANES - ROPE_DIM)))
        ckv_new = jnp.pad(ckv.reshape(b, s, lora), ((0, 0), (0, LANES - s), (0, 0)))
        kr_new = jnp.pad(kr_pad.reshape(b, s, LANES), ((0, 0), (0, LANES - s), (0, 0)))
        o_lat = attn_sample(q_abs, qr, ckv_past, kr_past_pad, ckv_new, kr_new, s, n_heads)
        o_lat = o_lat.reshape(b, s, n_heads, lora).transpose(2, 0, 1, 3).reshape(n_heads, n, lora)
        o_h = head_matmul(o_lat, wp['w_uv_h'])
        o_a = o_h.reshape(n_heads, n, hd).transpose(1, 0, 2).reshape(n, n_heads * hd)

    qkv_w = 3 * n_heads * hd
    prev0 = jnp.pad(conv0, ((0, 0), (SUBLANES - (CONV_W - 1), 0), (0, 0)))
    merged, s_new = gdn(u, o_a, prev0, s0, wp['conv_w'], wp['a_log'], wp['dt_bias'], wp['gdn_norm_g'], b, s,
                        cols['z'] // d, cols['small'] // 512, cols['gate_a'] // d, cols['gate_b'] // d, n_heads)
    tail = u.reshape(b, s, -1)[:, max(s - (CONV_W - 1), 0):, :qkv_w]
    conv_new = jnp.concatenate([conv0, tail], axis=1)[:, -(CONV_W - 1):]

    return (merged, x), ckv.reshape(b, s, lora), kr.reshape(b, s, ROPE_DIM), s_new, conv_new


def _prep_weights(l, attn_norm_g, w_in, q_norm_g, w_uq, kv_norm_g, w_uk, w_uv, conv_w, a_log, dt_bias, gdn_norm_g,
                  w_out, ffn_norm_g, w_group, b_group, w_router, b_router, w_gate, w_up, w_down):
    d = w_in.shape[1]
    q_lora, n_heads, _ = w_uq.shape[1:]
    kv_lora = w_uk.shape[1]
    hd = HEAD_DIM
    qkv_w = 3 * n_heads * hd
    sizes = (q_lora, kv_lora, ROPE_DIM, qkv_w, n_heads, n_heads, n_heads * hd, d, d)
    offs = np.concatenate([[0], np.cumsum(sizes)])
    wi = w_in[l]
    part = lambda i: wi[:, offs[i]:offs[i + 1]]
    w_qlat, w_kvlat, w_kr, w_qkv, w_a, w_b, w_z, w_ga, w_gb = [part(i) for i in range(9)]
    zpad = lambda c: jnp.zeros((d, c), F32)
    small = jnp.concatenate([w_kr, _rot_cols(w_kr), w_a, zpad(LANES - n_heads), w_b, zpad(LANES - n_heads),
                             zpad(LANES)], axis=1)
    w_all = jnp.concatenate([w_qkv, w_z, w_ga, w_gb, w_qlat, w_kvlat, small], axis=1).astype(BF16)
    cols = {'qkv': 0, 'z': qkv_w, 'gate_a': qkv_w + n_heads * hd, 'gate_b': qkv_w + n_heads * hd + d,
            'q_lat': qkv_w + n_heads * hd + 2 * d}
    cols['kv_lat'] = cols['q_lat'] + q_lora
    cols['small'] = cols['kv_lat'] + kv_lora
    wq = w_uq[l]
    wq_nope = wq[:, :, :hd].reshape(q_lora, n_heads * hd)
    wq_r = wq[:, :, hd:]
    wq_rope = jnp.concatenate([wq_r, _rot_cols(wq_r)], axis=-1).reshape(q_lora, n_heads * hd)
    wq_ext = jnp.concatenate([wq_nope, wq_rope], axis=1).astype(BF16)
    wk = w_uk[l].reshape(kv_lora, n_heads * hd).astype(BF16)
    wvt = w_uv[l].reshape(kv_lora, n_heads * hd).T.astype(BF16)
    wr = jnp.concatenate([w_group[l], w_router[l].transpose(1, 0, 2).reshape(d, -1)], axis=1)
    wr = jnp.pad(wr, ((0, 0), (0, LANES - wr.shape[1])))
    wr_hi = wr.astype(BF16)
    wr_lo = (wr - wr_hi.astype(F32)).astype(BF16)
    br = jnp.concatenate([b_group[l], b_router[l].reshape(-1)])
    br = jnp.pad(br, (0, LANES - br.shape[0])).reshape(1, LANES)
    return {
        'n_heads': n_heads, 'cols': cols, 'w_all': w_all, 'attn_norm_g': attn_norm_g[l],
        'q_norm_g': q_norm_g[l], 'kv_norm_g': kv_norm_g[l], 'wq_ext': wq_ext, 'wk': wk, 'wvt': wvt,
        'w_uk_t': w_uk[l].transpose(1, 2, 0).astype(BF16),
        'w_uv_h': w_uv[l].transpose(1, 0, 2).astype(BF16),
        'conv_w': conv_w[l], 'a_log': a_log[l], 'dt_bias': dt_bias[l], 'gdn_norm_g': gdn_norm_g[l],
        'w_out': w_out[l].astype(BF16), 'ffn_norm_g': ffn_norm_g[l], 'wr_hi': wr_hi, 'wr_lo': wr_lo, 'br': br,
        'w_gate': w_gate[l], 'w_up': w_up[l], 'w_down': w_down[l],
    }


def kernel(x_prompt, x_sample, cache_ckv, cache_k_rope, state_gdn, state_conv, attn_norm_g, w_in, q_norm_g, w_uq, kv_norm_g, w_uk, w_uv, conv_w, a_log, dt_bias, gdn_norm_g, w_out, ffn_norm_g, w_group, b_group, w_router, b_router, w_gate, w_up, w_down, final_norm_g):
    depth = w_in.shape[0]
    assert depth == 1, "final RMSNorm is fused into the last layer; deeper trunks are not supported"
    b_p, s_p, d = x_prompt.shape
    b_s, s_s, _ = x_sample.shape
    past = cache_ckv.shape[2]
    n_heads = w_uq.shape[2]
    qkv_w = 3 * n_heads * HEAD_DIM
    pos_p = jnp.arange(s_p, dtype=jnp.int32)
    pos_s = past + jnp.arange(s_s, dtype=jnp.int32)
    wp = _prep_weights(0, attn_norm_g, w_in, q_norm_g, w_uq, kv_norm_g, w_uk, w_uv, conv_w, a_log, dt_bias,
                       gdn_norm_g, w_out, ffn_norm_g, w_group, b_group, w_router, b_router, w_gate, w_up, w_down)
    s0 = jnp.zeros((b_p, n_heads, HEAD_DIM, HEAD_DIM), F32)
    c0 = jnp.zeros((b_p, CONV_W - 1, qkv_w), F32)
    (m_p, xr_p), ckv_p, kr_p, sg_p, sc_p = _mixer(x_prompt, pos_p, None, None, s0, c0, wp)
    (m_s, xr_s), ckv_s, kr_s, sg_s, sc_s = _mixer(x_sample, pos_s, cache_ckv[0], cache_k_rope[0], state_gdn[0],
                                                  state_conv[0], wp)

    n_p, n_s = b_p * s_p, b_s * s_s
    tm_s = _tile(int(np.gcd(n_p, n_s)), 512)
    x1_p = out_proj(m_p, xr_p, wp['w_out'], _tile(n_p, 1024), 512)
    x1_s = out_proj(m_s, xr_s, wp['w_out'], tm_s, 512)
    h_bf, eid, wgt, cnt = route(x1_p, x1_s, wp['ffn_norm_g'], wp['wr_hi'], wp['wr_lo'], wp['br'], tm_s)
    n_exp = wp['w_gate'].shape[0]
    counts = cnt[:, 0, :n_exp].sum(axis=0)
    ma, mb = _moe(h_bf, eid[:, :TOP_K], wgt[:, :TOP_K], counts, wp['w_gate'], wp['w_up'], wp['w_down'], MOE_BLOCK)
    yp = final_norm(x1_p, ma, mb, final_norm_g, _tile(n_p, 512), 0).reshape(b_p, s_p, d)
    ys = final_norm(x1_s, ma, mb, final_norm_g, tm_s, n_p).reshape(b_s, s_s, d)
    return (yp, ys, ckv_p[None], kr_p[None], sg_p[None], sc_p[None], ckv_s[None], kr_s[None], sg_s[None], sc_s[None])
```

```python
import functools

import numpy as np
import jax
import jax.numpy as jnp
from jax import lax
from jax.experimental import pallas as pl
from jax.experimental.pallas import tpu as pltpu

F32 = jnp.float32
BF16 = jnp.bfloat16

EPS = 1e-6
CHUNK = 64
ROPE_BASE = 10000.0
HEAD_DIM = 128
ROPE_DIM = 64
CONV_W = 4
TOP_K = 2
N_GROUPS = 8
EXPERTS_PER_GROUP = 8

LANES = 128
SUBLANES = 8
BF16_ROWS = 16
VMEM_LIMIT = 56 * 1024 * 1024
ATTN_TILE = 512
ATTN_HEADS_PER_STEP = 4
MOE_BLOCK = 256
IN_PROJ_TN = 768


def _cparams(sem):
    return pltpu.CompilerParams(dimension_semantics=sem, vmem_limit_bytes=VMEM_LIMIT)


def _dot(a, b):
    return jnp.dot(a, b, preferred_element_type=F32)


def _dot_nt(a, b):
    return lax.dot_general(a, b, (((1,), (1,)), ((), ())), preferred_element_type=F32)


def _dot_tn(a, b):
    return lax.dot_general(a, b, (((0,), (0,)), ((), ())), preferred_element_type=F32)


def _sigmoid(x):
    return 1.0 / (1.0 + jnp.exp(-x))


def _split_bf16(a):
    hi = a.astype(BF16)
    lo = (a - hi.astype(F32)).astype(BF16)
    return hi, lo


def _dot3(a_hi, a_lo, b_hi, b_lo):
    return _dot(a_hi, b_hi) + (_dot(a_hi, b_lo) + _dot(a_lo, b_hi))


def _rms_mm_body(x_ref, g_ref, w_ref, o_ref, h_ref):
    @pl.when(pl.program_id(1) == 0)
    def _():
        x = x_ref[...]
        ms = jnp.mean(x * x, axis=-1, keepdims=True)
        h_ref[...] = ((x * lax.rsqrt(ms + EPS)) * g_ref[...]).astype(BF16)

    o_ref[...] = _dot_nt(h_ref[...], w_ref[...])


def rms_matmul(x, g, w_t, tm, tn):
    n, k = x.shape
    m = w_t.shape[0]
    return pl.pallas_call(
        _rms_mm_body,
        grid=(n // tm, m // tn),
        in_specs=[pl.BlockSpec((tm, k), lambda i, j: (i, 0)),
                  pl.BlockSpec((1, k), lambda i, j: (0, 0)),
                  pl.BlockSpec((tn, k), lambda i, j: (j, 0))],
        out_specs=pl.BlockSpec((tm, tn), lambda i, j: (i, j)),
        out_shape=jax.ShapeDtypeStruct((n, m), F32),
        scratch_shapes=[pltpu.VMEM((tm, k), BF16)],
        compiler_params=_cparams(("parallel", "arbitrary")),
        name="rms_matmul",
    )(x, g.reshape(1, k), w_t)


def _mla_prep_body(ql_ref, kvl_ref, sm_ref, cs_ref, gq_ref, gkv_ref, wq_ref, *rest, n_heads, scale, with_kv):
    if with_kv:
        wk_ref, wvt_ref, ckv_ref, kr_ref, q_ref, k_ref, vt_ref = rest
    else:
        ckv_ref, kr_ref, q_ref = rest
    hd = HEAD_DIM
    ql = ql_ref[...]
    qn = ((ql * lax.rsqrt(jnp.mean(ql * ql, axis=-1, keepdims=True) + EPS)) * gq_ref[...]).astype(BF16)
    q_all = _dot(qn, wq_ref[...])
    kvl = kvl_ref[...]
    ckv = (kvl * lax.rsqrt(jnp.mean(kvl * kvl, axis=-1, keepdims=True) + EPS)) * gkv_ref[...]
    ckv_ref[...] = ckv
    if with_kv:
        ckv_b = ckv.astype(BF16)
        k_all = _dot(ckv_b, wk_ref[...])
        vt_all = _dot_nt(wvt_ref[...], ckv_b)

    cs = cs_ref[...]
    lane = lax.broadcasted_iota(jnp.int32, cs.shape, 1)

    def rope(t):
        t = t * cs
        return jnp.where(lane < ROPE_DIM, t + pltpu.roll(t, ROPE_DIM, axis=1), 0.0)

    kr = rope(sm_ref[:, 0:LANES])
    kr_ref[...] = kr
    kr_b = kr.astype(BF16)
    off = n_heads * hd
    for h in range(n_heads):
        q_ref[0, h, :, 0:hd] = (q_all[:, h * hd:(h + 1) * hd] * scale).astype(BF16)
        q_ref[0, h, :, hd:2 * hd] = (rope(q_all[:, off + h * hd:off + (h + 1) * hd]) * scale).astype(BF16)
        if with_kv:
            k_ref[0, h, :, 0:hd] = k_all[:, h * hd:(h + 1) * hd].astype(BF16)
            k_ref[0, h, :, hd:2 * hd] = kr_b
            vt_ref[0, h, 0, 0:hd, :] = vt_all[h * hd:(h + 1) * hd, :].astype(BF16)
            vt_ref[0, h, 0, hd:hd + BF16_ROWS, :] = jnp.ones((BF16_ROWS, vt_all.shape[1]), BF16)


def mla_prep(u, cs, gq, gkv, wq, wk, wvt, b, s, tm, t_att, col_ql, col_kvl, col_sm, n_heads, scale):
    lora = wq.shape[0]
    nt = s // tm
    row = lambda bi, si: bi * nt + si
    hd = HEAD_DIM
    with_kv = wk is not None
    full = lambda bi, si: (0, 0)
    in_specs = [pl.BlockSpec((tm, lora), lambda bi, si: (row(bi, si), col_ql)),
                pl.BlockSpec((tm, lora), lambda bi, si: (row(bi, si), col_kvl)),
                pl.BlockSpec((tm, 512), lambda bi, si: (row(bi, si), col_sm)),
                pl.BlockSpec((tm, LANES), lambda bi, si: (row(bi, si), 0)),
                pl.BlockSpec((1, lora), full),
                pl.BlockSpec((1, lora), full),
                pl.BlockSpec(wq.shape, full)]
    out_specs = [pl.BlockSpec((tm, lora), lambda bi, si: (row(bi, si), 0)),
                 pl.BlockSpec((tm, LANES), lambda bi, si: (row(bi, si), 0)),
                 pl.BlockSpec((1, n_heads, tm, 2 * hd), lambda bi, si: (bi, 0, si, 0))]
    out_shape = [jax.ShapeDtypeStruct((b * s, lora), F32),
                 jax.ShapeDtypeStruct((b * s, LANES), F32),
                 jax.ShapeDtypeStruct((b, n_heads, s, 2 * hd), BF16)]
    args = [u, u, u, cs, gq.reshape(1, lora), gkv.reshape(1, lora), wq]
    if with_kv:
        r = t_att // tm
        in_specs += [pl.BlockSpec(wk.shape, full), pl.BlockSpec(wvt.shape, full)]
        out_specs += [pl.BlockSpec((1, n_heads, tm, 2 * hd), lambda bi, si: (bi, 0, si, 0)),
                      pl.BlockSpec((1, n_heads, 1, hd + BF16_ROWS, tm),
                                   lambda bi, si: (bi, 0, si // r, 0, si % r))]
        out_shape += [jax.ShapeDtypeStruct((b, n_heads, s, 2 * hd), BF16),
                      jax.ShapeDtypeStruct((b, n_heads, s // t_att, hd + BF16_ROWS, t_att), BF16)]
        args += [wk, wvt]
    return pl.pallas_call(
        functools.partial(_mla_prep_body, n_heads=n_heads, scale=scale, with_kv=with_kv),
        grid=(b, nt),
        in_specs=in_specs,
        out_specs=out_specs,
        out_shape=out_shape,
        compiler_params=_cparams(("parallel", "parallel")),
        name="mla_prep",
    )(*args)


def _flash_body(q_ref, k_ref, vt_ref, o_ref, m_scr, acc_scr, *, t, gh):
    qi = pl.program_id(2)
    hd = HEAD_DIM
    m_scr[...] = jnp.full(m_scr.shape, -jnp.inf, F32)
    acc_scr[...] = jnp.zeros(acc_scr.shape, F32)

    def block(j, masked):
        start = pl.multiple_of(j * t, t)
        scs = [_dot_nt(k_ref[0, g, pl.ds(start, t), :], q_ref[0, g]) for g in range(gh)]
        for g in range(gh):
            sc = scs[g]
            if masked:
                krow = lax.broadcasted_iota(jnp.int32, sc.shape, 0)
                qcol = lax.broadcasted_iota(jnp.int32, sc.shape, 1)
                sc = jnp.where(krow // CHUNK <= qcol // CHUNK, sc, -jnp.inf)
            m_prev = m_scr[g]
            m_new = jnp.maximum(m_prev, jnp.max(sc, axis=0, keepdims=True))
            alpha = jnp.exp2(m_prev - m_new)
            p = jnp.exp2(sc - m_new)
            acc_scr[g] = alpha * acc_scr[g] + _dot(vt_ref[0, g, j], p.astype(BF16))
            m_scr[g] = m_new

    def full_block(j, carry):
        block(j, False)
        return carry

    lax.fori_loop(0, qi, full_block, 0)
    block(qi, True)
    for g in range(gh):
        acc = acc_scr[g]
        o_ref[0, :, g * hd:(g + 1) * hd] = jnp.transpose(acc[:hd] / acc[hd:hd + 1])


def flash_prompt(q, k, vt, t, gh):
    b, nh, s, dk = q.shape
    hd = HEAD_DIM
    hv = vt.shape[-2]
    assert t % CHUNK == 0 and s % t == 0 and nh % gh == 0 and vt.shape[-1] == t and hv > hd
    return pl.pallas_call(
        functools.partial(_flash_body, t=t, gh=gh),
        grid=(b, nh // gh, s // t),
        in_specs=[pl.BlockSpec((1, gh, t, dk), lambda bi, hp, qi: (bi, hp, qi, 0)),
                  pl.BlockSpec((1, gh, s, dk), lambda bi, hp, qi: (bi, hp, 0, 0)),
                  pl.BlockSpec((1, gh, s // t, hv, t), lambda bi, hp, qi: (bi, hp, 0, 0, 0))],
        out_specs=pl.BlockSpec((1, t, gh * hd), lambda bi, hp, qi: (bi, qi, hp)),
        out_shape=jax.ShapeDtypeStruct((b, s, nh * hd), F32),
        scratch_shapes=[pltpu.VMEM((gh, 1, t), F32), pltpu.VMEM((gh, hv, t), F32)],
        compiler_params=_cparams(("parallel", "parallel", "arbitrary")),
        name="flash_prompt",
    )(q, k, vt)


def _bmm_body(a_ref, b_ref, o_ref):
    o_ref[0] = _dot(a_ref[0].astype(BF16), b_ref[0])


def head_matmul(a, b):
    nh, m, k = a.shape
    n = b.shape[2]
    return pl.pallas_call(
        _bmm_body,
        grid=(nh,),
        in_specs=[pl.BlockSpec((1, m, k), lambda h: (h, 0, 0)),
                  pl.BlockSpec((1, k, n), lambda h: (h, 0, 0))],
        out_specs=pl.BlockSpec((1, m, n), lambda h: (h, 0, 0)),
        out_shape=jax.ShapeDtypeStruct((nh, m, n), F32),
        compiler_params=_cparams(("parallel",)),
        name="head_matmul",
    )(a, b)


def _attn_sample_body(qa_ref, qr_ref, kp_ref, krp_ref, kn_ref, krn_ref, o_ref, *, past, s_new, n_heads):
    qa = qa_ref[0].astype(BF16)
    qr = qr_ref[0].astype(BF16)
    kp = kp_ref[0].astype(BF16)
    krp = krp_ref[0].astype(BF16)
    kn = kn_ref[0].astype(BF16)
    krn = krn_ref[0].astype(BF16)
    s_past = _dot_nt(qa, kp) + _dot_nt(qr, krp)
    s_n = _dot_nt(qa, kn) + _dot_nt(qr, krn)
    row = lax.broadcasted_iota(jnp.int32, s_n.shape, 0)
    col = lax.broadcasted_iota(jnp.int32, s_n.shape, 1)
    qpos = past + row // n_heads
    kpos = past + col
    valid = (col < s_new) & (kpos // CHUNK <= qpos // CHUNK)
    s_n = jnp.where(valid, s_n, -jnp.inf)
    m = jnp.maximum(jnp.max(s_past, axis=-1, keepdims=True), jnp.max(s_n, axis=-1, keepdims=True))
    pp = jnp.exp2(s_past - m)
    pn = jnp.exp2(s_n - m)
    l = jnp.sum(pp, axis=-1, keepdims=True) + jnp.sum(pn, axis=-1, keepdims=True)
    o_ref[0] = (_dot(pp.astype(BF16), kp) + _dot(pn.astype(BF16), kn)) / l


def attn_sample(qa, qr, ckv_past, kr_past, ckv_new, kr_new, s_new, n_heads):
    b, r, lora = qa.shape
    past = ckv_past.shape[1]
    return pl.pallas_call(
        functools.partial(_attn_sample_body, past=past, s_new=s_new, n_heads=n_heads),
        grid=(b,),
        in_specs=[pl.BlockSpec((1, r, lora), lambda i: (i, 0, 0)),
                  pl.BlockSpec((1, r, LANES), lambda i: (i, 0, 0)),
                  pl.BlockSpec((1, past, lora), lambda i: (i, 0, 0)),
                  pl.BlockSpec((1, past, LANES), lambda i: (i, 0, 0)),
                  pl.BlockSpec((1, LANES, lora), lambda i: (i, 0, 0)),
                  pl.BlockSpec((1, LANES, LANES), lambda i: (i, 0, 0))],
        out_specs=pl.BlockSpec((1, r, lora), lambda i: (i, 0, 0)),
        out_shape=jax.ShapeDtypeStruct((b, r, lora), F32),
        compiler_params=_cparams(("parallel",)),
        name="attn_sample",
    )(qa, qr, ckv_past, kr_past, ckv_new, kr_new)


def _softplus(x):
    return jnp.maximum(x, 0.0) + jnp.log1p(jnp.exp(-jnp.abs(x)))


def _gdn_body(qkv_ref, z_ref, sm_ref, ga_ref, gb_ref, oa_ref, prev0_ref, s0_ref, convw_ref, alog_ref, dtb_ref,
              gn_ref, m_ref, sout_ref, s_scr, prev_scr, y_scr, *, L, n_heads):
    hd = HEAD_DIM
    heads = range(n_heads)
    c = pl.program_id(1)

    @pl.when(c == 0)
    def _():
        s_scr[...] = s0_ref[0]
        prev_scr[...] = prev0_ref[0]

    x = qkv_ref[...]
    xp = jnp.concatenate([prev_scr[...], x], axis=0)
    base = SUBLANES - (CONV_W - 1)
    y = xp[base:base + L] * convw_ref[0:1, :]
    for j in range(1, CONV_W):
        y = y + xp[base + j:base + j + L] * convw_ref[j:j + 1, :]
    y_scr[...] = y * _sigmoid(y)
    prev_scr[...] = x[L - SUBLANES:L]

    a = sm_ref[:, LANES:2 * LANES]
    bb = sm_ref[:, 2 * LANES:3 * LANES]
    g = -jnp.exp(alog_ref[...]) * _softplus(a + dtb_ref[...])
    beta = _sigmoid(bb)
    rowl = lax.broadcasted_iota(jnp.int32, g.shape, 0)
    cum = g
    d = 1
    while d < L:
        cum = cum + jnp.where(rowl >= d, pltpu.roll(cum, d, axis=0), 0.0)
        d *= 2
    cum_t = jnp.transpose(jnp.concatenate([cum, jnp.zeros((LANES - L, LANES), F32)], axis=0))

    row = lax.broadcasted_iota(jnp.int32, (L, L), 0)
    col = lax.broadcasted_iota(jnp.int32, (L, L), 1)
    tri = row >= col
    strict = row > col
    eye = jnp.where(row == col, 1.0, 0.0)
    gn = gn_ref[...]

    q_b, k_b, kb_b, rhs_b, qg_b, kd_b = [], [], [], [], [], []
    for h in heads:
        qh = y_scr[:, h * hd:(h + 1) * hd]
        kh = y_scr[:, (n_heads + h) * hd:(n_heads + h + 1) * hd]
        vh = y_scr[:, (2 * n_heads + h) * hd:(2 * n_heads + h + 1) * hd]
        qh = (qh * lax.rsqrt(jnp.sum(qh * qh, axis=-1, keepdims=True) + EPS)) * (hd ** -0.5)
        kh = kh * lax.rsqrt(jnp.sum(kh * kh, axis=-1, keepdims=True) + EPS)
        gcol = cum[:, h:h + 1]
        bcol = beta[:, h:h + 1]
        e_g = jnp.exp(gcol)
        kb = kh * bcol
        q_b.append(qh.astype(BF16))
        k_b.append(kh.astype(BF16))
        kb_b.append(kb.astype(BF16))
        rhs_b.append(jnp.concatenate([vh * bcol, kb * e_g], axis=1).astype(BF16))
        qg_b.append((qh * e_g).astype(BF16))
        kd_b.append((kh * jnp.exp(cum[L - 1:L, h:h + 1] - gcol)).astype(BF16))

    kk = [_dot_nt(kb_b[h], k_b[h]) for h in heads]
    qk = [_dot_nt(q_b[h], k_b[h]) for h in heads]

    nmat, attn_b = [], []
    for h in heads:
        diff = cum[:, h:h + 1] - cum_t[h:h + 1, 0:L]
        decay = jnp.where(tri, jnp.exp(jnp.where(tri, diff, 0.0)), 0.0)
        nmat.append(jnp.where(strict, kk[h] * decay, 0.0))
        attn_b.append(jnp.where(tri, qk[h] * decay, 0.0).astype(BF16))

    tmat = [eye - nmat[h] for h in heads]
    p_b = [nmat[h].astype(BF16) for h in heads]
    pmat = [_dot(p_b[h], p_b[h]) for h in heads]
    power = 2
    while True:
        p_b = [pmat[h].astype(BF16) for h in heads]
        tmat = [tmat[h] + _dot(tmat[h].astype(BF16), p_b[h]) for h in heads]
        power *= 2
        if power >= L:
            break
        pmat = [_dot(p_b[h], p_b[h]) for h in heads]

    uw = [_dot(tmat[h].astype(BF16), rhs_b[h]) for h in heads]
    s_old = [s_scr[h] for h in heads]
    s_b = [s_old[h].astype(BF16) for h in heads]
    ws = [_dot(jnp.concatenate([uw[h][:, hd:].astype(BF16), qg_b[h]], axis=0), s_b[h]) for h in heads]
    vn_b = [(uw[h][:, :hd] - ws[h][:L]).astype(BF16) for h in heads]
    av = [_dot(attn_b[h], vn_b[h]) for h in heads]
    ks = [_dot_tn(kd_b[h], vn_b[h]) for h in heads]
    for h in heads:
        sl = slice(h * hd, (h + 1) * hd)
        o = ws[h][L:] + av[h]
        s_scr[h] = s_old[h] * jnp.exp(cum[L - 1:L, h:h + 1]) + ks[h]
        zh = z_ref[:, sl]
        on = (o * lax.rsqrt(jnp.mean(o * o, axis=-1, keepdims=True) + EPS)) * gn
        ob = on * (zh * _sigmoid(zh))
        m_ref[:, sl] = (_sigmoid(ga_ref[:, sl]) * oa_ref[:, sl] + _sigmoid(gb_ref[:, sl]) * ob).astype(BF16)

    @pl.when(c == pl.num_programs(1) - 1)
    def _():
        sout_ref[0] = s_scr[...]


def gdn(u, o_a, prev0, s0, conv_w, a_log, dt_bias, gn, b, s, col_z, col_sm, col_ga, col_gb, n_heads):
    hd = HEAD_DIM
    L = min(s, CHUNK)
    nc = s // L
    qkv_w = 3 * n_heads * hd
    row = lambda bi, c: bi * nc + c
    alog = jnp.zeros((1, LANES), F32).at[0, :n_heads].set(a_log)
    dtb = jnp.zeros((1, LANES), F32).at[0, :n_heads].set(dt_bias)
    return pl.pallas_call(
        functools.partial(_gdn_body, L=L, n_heads=n_heads),
        grid=(b, nc),
        in_specs=[pl.BlockSpec((L, qkv_w), lambda bi, c: (row(bi, c), 0)),
                  pl.BlockSpec((L, n_heads * hd), lambda bi, c: (row(bi, c), col_z)),
                  pl.BlockSpec((L, 512), lambda bi, c: (row(bi, c), col_sm)),
                  pl.BlockSpec((L, n_heads * hd), lambda bi, c: (row(bi, c), col_ga)),
                  pl.BlockSpec((L, n_heads * hd), lambda bi, c: (row(bi, c), col_gb)),
                  pl.BlockSpec((L, n_heads * hd), lambda bi, c: (row(bi, c), 0)),
                  pl.BlockSpec((1, SUBLANES, qkv_w), lambda bi, c: (bi, 0, 0)),
                  pl.BlockSpec((1, n_heads, hd, hd), lambda bi, c: (bi, 0, 0, 0)),
                  pl.BlockSpec((CONV_W, qkv_w), lambda bi, c: (0, 0)),
                  pl.BlockSpec((1, LANES), lambda bi, c: (0, 0)),
                  pl.BlockSpec((1, LANES), lambda bi, c: (0, 0)),
                  pl.BlockSpec((1, hd), lambda bi, c: (0, 0))],
        out_specs=[pl.BlockSpec((L, n_heads * hd), lambda bi, c: (row(bi, c), 0)),
                   pl.BlockSpec((1, n_heads, hd, hd), lambda bi, c: (bi, 0, 0, 0))],
        out_shape=[jax.ShapeDtypeStruct((b * s, n_heads * hd), BF16),
                   jax.ShapeDtypeStruct((b, n_heads, hd, hd), F32)],
        scratch_shapes=[pltpu.VMEM((n_heads, hd, hd), F32),
                        pltpu.VMEM((SUBLANES, qkv_w), F32),
                        pltpu.VMEM((L, qkv_w), F32)],
        compiler_params=_cparams(("parallel", "arbitrary")),
        name="gdn",
    )(u, u, u, u, u, o_a, prev0, s0, conv_w, alog, dtb, gn.reshape(1, hd))


def _out_proj_body(m_ref, x_ref, w_ref, o_ref):
    o_ref[...] = x_ref[...] + _dot(m_ref[...], w_ref[...])


def out_proj(m, x, w, tm, tn):
    n, d = x.shape
    return pl.pallas_call(
        _out_proj_body,
        grid=(n // tm, d // tn),
        in_specs=[pl.BlockSpec((tm, d), lambda i, j: (i, 0)),
                  pl.BlockSpec((tm, tn), lambda i, j: (i, j)),
                  pl.BlockSpec((d, tn), lambda i, j: (0, j))],
        out_specs=pl.BlockSpec((tm, tn), lambda i, j: (i, j)),
        out_shape=jax.ShapeDtypeStruct((n, d), F32),
        compiler_params=_cparams(("parallel", "arbitrary")),
        name="out_proj",
    )(m, x, w)


def _route_body(xa_ref, xb_ref, g_ref, whi_ref, wlo_ref, bias_ref, h_ref, eid_ref, wgt_ref, cnt_ref, *, tiles_a):
    x = jnp.where(pl.program_id(0) < tiles_a, xa_ref[...], xb_ref[...])
    hn = (x * lax.rsqrt(jnp.mean(x * x, axis=-1, keepdims=True) + EPS)) * g_ref[...]
    h_hi, h_lo = _split_bf16(hn)
    h_ref[...] = h_hi
    lg = _dot3(h_hi, h_lo, whi_ref[...], wlo_ref[...]) + bias_ref[...]
    lane = lax.broadcasted_iota(jnp.int32, lg.shape, 1)
    big = jnp.int32(LANES)
    neg = -jnp.inf
    glog = jnp.where(lane < N_GROUPS, lg, neg)
    gmax = jnp.max(glog, axis=-1, keepdims=True)
    pg_top = 1.0 / jnp.sum(jnp.exp(glog - gmax), axis=-1, keepdims=True)
    grp = jnp.min(jnp.where(glog == gmax, lane, big), axis=-1, keepdims=True)
    lo = N_GROUPS + grp * EXPERTS_PER_GROUP
    le = jnp.where((lane >= lo) & (lane < lo + EXPERTS_PER_GROUP), lg, neg)
    v1 = jnp.max(le, axis=-1, keepdims=True)
    i1 = jnp.min(jnp.where(le == v1, lane, big), axis=-1, keepdims=True)
    le2 = jnp.where(lane == i1, neg, le)
    v2 = jnp.max(le2, axis=-1, keepdims=True)
    i2 = jnp.min(jnp.where(le2 == v2, lane, big), axis=-1, keepdims=True)
    e2 = jnp.exp(v2 - v1)
    den = 1.0 + e2
    w1 = pg_top * (1.0 / den)
    w2 = pg_top * (e2 / den)
    eid_ref[...] = jnp.where(lane == 0, i1 - N_GROUPS, jnp.where(lane == 1, i2 - N_GROUPS, 0))
    wgt_ref[...] = jnp.where(lane == 0, w1, jnp.where(lane == 1, w2, 0.0))
    hit = jnp.where((lane == i1 - N_GROUPS) | (lane == i2 - N_GROUPS), 1.0, 0.0)
    cnt_ref[0] = jnp.broadcast_to(jnp.sum(hit, axis=0, keepdims=True), cnt_ref.shape[1:]).astype(jnp.int32)


def route(xa, xb, g, whi, wlo, bias, tm):
    d = xa.shape[1]
    ta, tb = xa.shape[0] // tm, xb.shape[0] // tm
    n = (ta + tb) * tm
    return pl.pallas_call(
        functools.partial(_route_body, tiles_a=ta),
        grid=(ta + tb,),
        in_specs=[pl.BlockSpec((tm, d), lambda i: (jnp.minimum(i, ta - 1), 0)),
                  pl.BlockSpec((tm, d), lambda i: (jnp.maximum(i - ta, 0), 0)),
                  pl.BlockSpec((1, d), lambda i: (0, 0)),
                  pl.BlockSpec((d, LANES), lambda i: (0, 0)),
                  pl.BlockSpec((d, LANES), lambda i: (0, 0)),
                  pl.BlockSpec((1, LANES), lambda i: (0, 0))],
        out_specs=[pl.BlockSpec((tm, d), lambda i: (i, 0)),
                   pl.BlockSpec((tm, LANES), lambda i: (i, 0)),
                   pl.BlockSpec((tm, LANES), lambda i: (i, 0)),
                   pl.BlockSpec((1, SUBLANES, LANES), lambda i: (i, 0, 0))],
        out_shape=[jax.ShapeDtypeStruct((n, d), BF16),
                   jax.ShapeDtypeStruct((n, LANES), jnp.int32),
                   jax.ShapeDtypeStruct((n, LANES), F32),
                   jax.ShapeDtypeStruct((n // tm, SUBLANES, LANES), jnp.int32)],
        compiler_params=_cparams(("parallel",)),
        name="route",
    )(xa, xb, g.reshape(1, d), whi, wlo, bias)


def _expert_body(be_tab, first_tab, slot_tab, next_tab, nused, x_ref, sw_ref, wg_hbm, wu_hbm, wd_hbm, o_ref,
                 stg_g, stg_u, stg_d, wg_s, wu_s, wd_s, sem):
    blk = pl.program_id(0)

    def weight_copies(e, slot):
        return (pltpu.make_async_copy(wg_hbm.at[e], stg_g.at[slot], sem.at[slot, 0]),
                pltpu.make_async_copy(wu_hbm.at[e], stg_u.at[slot], sem.at[slot, 1]),
                pltpu.make_async_copy(wd_hbm.at[e], stg_d.at[slot], sem.at[slot, 2]))

    @pl.when((blk == 0) & (nused[0] > 0))
    def _():
        for cp in weight_copies(be_tab[0], 0):
            cp.start()

    @pl.when(blk < nused[0])
    def _():
        @pl.when(first_tab[blk] == 1)
        def _():
            slot = slot_tab[blk]

            @pl.when(next_tab[blk] >= 0)
            def _():
                for cp in weight_copies(next_tab[blk], 1 - slot):
                    cp.start()

            for cp in weight_copies(be_tab[blk], slot):
                cp.wait()
            wg_s[...] = stg_g[slot].astype(BF16)
            wu_s[...] = stg_u[slot].astype(BF16)
            wd_s[...] = stg_d[slot].astype(BF16)

        x = x_ref[...]
        gate = _dot(x, wg_s[...])
        up = _dot(x, wu_s[...])
        hid = (gate * _sigmoid(gate)) * up
        o_ref[...] = (_dot(hid.astype(BF16), wd_s[...]) * sw_ref[...]).astype(o_ref.dtype)

    @pl.when(blk >= nused[0])
    def _():
        o_ref[...] = jnp.zeros(o_ref.shape, o_ref.dtype)


def experts(x_sorted, slot_w, block_e, first, slot_tab, next_tab, nused, w_gate, w_up, w_down, bm):
    n_slots, d = x_sorted.shape
    de = w_gate.shape[2]
    n_blocks = n_slots // bm
    row = lambda i, *_: (i, 0)
    grid_spec = pltpu.PrefetchScalarGridSpec(
        num_scalar_prefetch=5,
        grid=(n_blocks,),
        in_specs=[pl.BlockSpec((bm, d), row),
                  pl.BlockSpec((bm, 1), row),
                  pl.BlockSpec(memory_space=pl.ANY),
                  pl.BlockSpec(memory_space=pl.ANY),
                  pl.BlockSpec(memory_space=pl.ANY)],
        out_specs=pl.BlockSpec((bm, d), row),
        scratch_shapes=[pltpu.VMEM((2, d, de), F32), pltpu.VMEM((2, d, de), F32), pltpu.VMEM((2, de, d), F32),
                        pltpu.VMEM((d, de), BF16), pltpu.VMEM((d, de), BF16), pltpu.VMEM((de, d), BF16),
                        pltpu.SemaphoreType.DMA((2, 3))],
    )
    return pl.pallas_call(
        _expert_body,
        grid_spec=grid_spec,
        out_shape=jax.ShapeDtypeStruct((n_slots, d), BF16),
        compiler_params=_cparams(("arbitrary",)),
        name="experts",
    )(block_e, first, slot_tab, next_tab, nused, x_sorted, slot_w, w_gate, w_up, w_down)


def _final_body(x_ref, ma_ref, mb_ref, g_ref, o_ref):
    x = x_ref[...] + (ma_ref[...].astype(F32) + mb_ref[...].astype(F32))
    o_ref[...] = (x * lax.rsqrt(jnp.mean(x * x, axis=-1, keepdims=True) + EPS)) * g_ref[...]


def final_norm(x, ma, mb, g, tm, row0):
    n, d = x.shape
    assert row0 % tm == 0 and n % tm == 0
    spec = pl.BlockSpec((tm, d), lambda i: (i, 0))
    mspec = pl.BlockSpec((tm, d), lambda i: (i + row0 // tm, 0))
    return pl.pallas_call(
        _final_body,
        grid=(n // tm,),
        in_specs=[spec, mspec, mspec, pl.BlockSpec((1, d), lambda i: (0, 0))],
        out_specs=spec,
        out_shape=jax.ShapeDtypeStruct((n, d), F32),
        compiler_params=_cparams(("parallel",)),
        name="final_norm",
    )(x, ma, mb, g.reshape(1, d))


def _tile(n, pref):
    t = min(n, pref)
    while n % t:
        t //= 2
    return t


def _rope_table(pos, b):
    half = ROPE_DIM // 2
    inv = ROPE_BASE ** (-jnp.arange(half, dtype=F32) / half)
    ang = pos.astype(F32)[:, None] * inv[None, :]
    cos, sin = jnp.cos(ang), jnp.sin(ang)
    cs = jnp.concatenate([cos, cos, sin, sin], axis=-1)
    return jnp.tile(cs, (b, 1))


def _rot_cols(w):
    half = ROPE_DIM // 2
    return jnp.concatenate([-w[..., half:], w[..., :half]], axis=-1)


def _moe(h_bf, eid, wgt, counts, w_gate, w_up, w_down, bm):
    n, d = h_bf.shape
    n_exp = w_gate.shape[0]
    a = n * TOP_K
    e_flat = eid.reshape(-1)
    w_flat = wgt.reshape(-1)
    padded = (counts + bm - 1) // bm * bm
    pend = jnp.cumsum(padded)
    n_blocks = (a + n_exp * (bm - 1) + bm - 1) // bm
    n_slots = n_blocks * bm
    block_e = jnp.minimum(jnp.searchsorted(pend, jnp.arange(n_blocks, dtype=jnp.int32) * bm, side='right'),
                          n_exp - 1).astype(jnp.int32)
    first = jnp.concatenate([jnp.ones((1,), jnp.int32), (block_e[1:] != block_e[:-1]).astype(jnp.int32)])
    nused = (pend[-1] // bm).astype(jnp.int32).reshape(1)
    slot_tab = (jnp.cumsum(first) - 1) % 2
    e_ids = jnp.arange(n_exp, dtype=jnp.int32)
    later = lax.cummin(jnp.where(counts > 0, e_ids, n_exp), axis=0, reverse=True)
    next_e = jnp.concatenate([later[1:], jnp.full((1,), n_exp, jnp.int32)])
    next_tab = jnp.where(next_e < n_exp, next_e, -1)[block_e].astype(jnp.int32)
    n_pad = n_slots - a
    pad_end = jnp.cumsum(padded - counts)
    pad_key = jnp.sum(jnp.arange(n_pad, dtype=jnp.int32)[None, :] >= pad_end[:, None], axis=0, dtype=jnp.int32)
    keys = jnp.concatenate([e_flat, pad_key])
    toks = jnp.concatenate([jnp.arange(a, dtype=jnp.int32) // TOP_K, jnp.arange(n_pad, dtype=jnp.int32) % n])
    wts = jnp.concatenate([w_flat, jnp.zeros((n_pad,), F32)])
    iota = jnp.arange(n_slots, dtype=jnp.int32)
    _, slot_tok, slot_w, slot_src = lax.sort((keys, toks, wts, iota), num_keys=1, is_stable=True)
    x_sorted = h_bf[slot_tok]
    out = experts(x_sorted, slot_w.reshape(n_slots, 1), block_e, first, slot_tab.astype(jnp.int32), next_tab, nused,
                  w_gate, w_up, w_down, bm)
    _, slot_of = lax.sort((slot_src, iota), num_keys=1)
    slot_of = slot_of[:a].reshape(n, TOP_K)
    return out[slot_of[:, 0]], out[slot_of[:, 1]]


def _mixer(x3, pos, ckv_past, kr_past, s0, conv0, wp):
    b, s, d = x3.shape
    n = b * s
    n_heads = wp['n_heads']
    hd = HEAD_DIM
    x = x3.reshape(n, d)
    cols = wp['cols']

    u = rms_matmul(x, wp['attn_norm_g'], wp['w_all'], _tile(n, 1024), IN_PROJ_TN)

    cs = _rope_table(pos, b)
    scale = (hd + ROPE_DIM) ** -0.5 * float(np.log2(np.e))
    prompt = ckv_past is None
    t_att = _tile(s, ATTN_TILE)
    tm = min(_tile(s, 256), t_att)
    res = mla_prep(u, cs, wp['q_norm_g'], wp['kv_norm_g'], wp['wq_ext'], wp['wk'] if prompt else None,
                   wp['wvt'] if prompt else None, b, s, tm, t_att, cols['q_lat'] // 512, cols['kv_lat'] // 512,
                   cols['small'] // 512, n_heads, scale)
    ckv, kr_pad, q = res[:3]
    kr = kr_pad[:, :ROPE_DIM]
    lora = ckv.shape[1]
    if prompt:
        o_a = flash_prompt(q, res[3], res[4], t_att, ATTN_HEADS_PER_STEP).reshape(n, n_heads * hd)
    else:
        qn = q[..., :hd].transpose(1, 0, 2, 3).reshape(n_heads, n, hd)
        q_abs = head_matmul(qn, wp['w_uk_t'])
        q_abs = q_abs.reshape(n_heads, b, s, lora).transpose(1, 2, 0, 3).reshape(b, s * n_heads, lora)
        qr = q[..., hd:].transpose(0, 2, 1, 3).reshape(b, s * n_heads, hd)
        kr_past_pad = jnp.pad(kr_past, ((0, 0), (0, 0), (0, LANES - ROPE_DIM)))
        ckv_new = jnp.pad(ckv.reshape(b, s, lora), ((0, 0), (0, LANES - s), (0, 0)))
        kr_new = jnp.pad(kr_pad.reshape(b, s, LANES), ((0, 0), (0, LANES - s), (0, 0)))
        o_lat = attn_sample(q_abs, qr, ckv_past, kr_past_pad, ckv_new, kr_new, s, n_heads)
        o_lat = o_lat.reshape(b, s, n_heads, lora).transpose(2, 0, 1, 3).reshape(n_heads, n, lora)
        o_h = head_matmul(o_lat, wp['w_uv_h'])
        o_a = o_h.reshape(n_heads, n, hd).transpose(1, 0, 2).reshape(n, n_heads * hd)

    qkv_w = 3 * n_heads * hd
    prev0 = jnp.pad(conv0, ((0, 0), (SUBLANES - (CONV_W - 1), 0), (0, 0)))
    merged, s_new = gdn(u, o_a, prev0, s0, wp['conv_w'], wp['a_log'], wp['dt_bias'], wp['gdn_norm_g'], b, s,
                        cols['z'] // d, cols['small'] // 512, cols['gate_a'] // d, cols['gate_b'] // d, n_heads)
    tail = u.reshape(b, s, -1)[:, max(s - (CONV_W - 1), 0):, :qkv_w]
    conv_new = jnp.concatenate([conv0, tail], axis=1)[:, -(CONV_W - 1):]

    return (merged, x), ckv.reshape(b, s, lora), kr.reshape(b, s, ROPE_DIM), s_new, conv_new


def _prep_weights(l, attn_norm_g, w_in, q_norm_g, w_uq, kv_norm_g, w_uk, w_uv, conv_w, a_log, dt_bias, gdn_norm_g,
                  w_out, ffn_norm_g, w_group, b_group, w_router, b_router, w_gate, w_up, w_down):
    d = w_in.shape[1]
    q_lora, n_heads, _ = w_uq.shape[1:]
    kv_lora = w_uk.shape[1]
    hd = HEAD_DIM
    qkv_w = 3 * n_heads * hd
    sizes = (q_lora, kv_lora, ROPE_DIM, qkv_w, n_heads, n_heads, n_heads * hd, d, d)
    offs = np.concatenate([[0], np.cumsum(sizes)])
    wi_t = jnp.swapaxes(w_in[l], 0, 1)
    part = lambda i: wi_t[offs[i]:offs[i + 1]]
    w_qlat, w_kvlat, w_kr, w_qkv, w_a, w_b, w_z, w_ga, w_gb = [part(i) for i in range(9)]
    zpad = lambda c: jnp.zeros((c, d), F32)
    half = ROPE_DIM // 2
    w_kr_rot = jnp.concatenate([-w_kr[half:], w_kr[:half]], axis=0)
    small = jnp.concatenate([w_kr, w_kr_rot, w_a, zpad(LANES - n_heads), w_b, zpad(LANES - n_heads),
                             zpad(LANES)], axis=0)
    w_all = jnp.concatenate([w_qkv, w_z, w_ga, w_gb, w_qlat, w_kvlat, small], axis=0).astype(BF16)
    cols = {'qkv': 0, 'z': qkv_w, 'gate_a': qkv_w + n_heads * hd, 'gate_b': qkv_w + n_heads * hd + d,
            'q_lat': qkv_w + n_heads * hd + 2 * d}
    cols['kv_lat'] = cols['q_lat'] + q_lora
    cols['small'] = cols['kv_lat'] + kv_lora
    wq = w_uq[l]
    wq_nope = wq[:, :, :hd].reshape(q_lora, n_heads * hd)
    wq_r = wq[:, :, hd:]
    wq_rope = jnp.concatenate([wq_r, _rot_cols(wq_r)], axis=-1).reshape(q_lora, n_heads * hd)
    wq_ext = jnp.concatenate([wq_nope, wq_rope], axis=1).astype(BF16)
    wk = w_uk[l].reshape(kv_lora, n_heads * hd).astype(BF16)
    wvt = w_uv[l].reshape(kv_lora, n_heads * hd).T.astype(BF16)
    wr = jnp.concatenate([w_group[l], w_router[l].transpose(1, 0, 2).reshape(d, -1)], axis=1)
    wr = jnp.pad(wr, ((0, 0), (0, LANES - wr.shape[1])))
    wr_hi = wr.astype(BF16)
    wr_lo = (wr - wr_hi.astype(F32)).astype(BF16)
    br = jnp.concatenate([b_group[l], b_router[l].reshape(-1)])
    br = jnp.pad(br, (0, LANES - br.shape[0])).reshape(1, LANES)
    return {
        'n_heads': n_heads, 'cols': cols, 'w_all': w_all, 'attn_norm_g': attn_norm_g[l],
        'q_norm_g': q_norm_g[l], 'kv_norm_g': kv_norm_g[l], 'wq_ext': wq_ext, 'wk': wk, 'wvt': wvt,
        'w_uk_t': w_uk[l].transpose(1, 2, 0).astype(BF16),
        'w_uv_h': w_uv[l].transpose(1, 0, 2).astype(BF16),
        'conv_w': conv_w[l], 'a_log': a_log[l], 'dt_bias': dt_bias[l], 'gdn_norm_g': gdn_norm_g[l],
        'w_out': w_out[l].astype(BF16), 'ffn_norm_g': ffn_norm_g[l], 'wr_hi': wr_hi, 'wr_lo': wr_lo, 'br': br,
        'w_gate': w_gate[l], 'w_up': w_up[l], 'w_down': w_down[l],
    }


def kernel(x_prompt, x_sample, cache_ckv, cache_k_rope, state_gdn, state_conv, attn_norm_g, w_in, q_norm_g, w_uq, kv_norm_g, w_uk, w_uv, conv_w, a_log, dt_bias, gdn_norm_g, w_out, ffn_norm_g, w_group, b_group, w_router, b_router, w_gate, w_up, w_down, final_norm_g):
    depth = w_in.shape[0]
    assert depth == 1, "final RMSNorm is fused into the last layer; deeper trunks are not supported"
    b_p, s_p, d = x_prompt.shape
    b_s, s_s, _ = x_sample.shape
    past = cache_ckv.shape[2]
    n_heads = w_uq.shape[2]
    qkv_w = 3 * n_heads * HEAD_DIM
    pos_p = jnp.arange(s_p, dtype=jnp.int32)
    pos_s = past + jnp.arange(s_s, dtype=jnp.int32)
    wp = _prep_weights(0, attn_norm_g, w_in, q_norm_g, w_uq, kv_norm_g, w_uk, w_uv, conv_w, a_log, dt_bias,
                       gdn_norm_g, w_out, ffn_norm_g, w_group, b_group, w_router, b_router, w_gate, w_up, w_down)
    s0 = jnp.zeros((b_p, n_heads, HEAD_DIM, HEAD_DIM), F32)
    c0 = jnp.zeros((b_p, CONV_W - 1, qkv_w), F32)
    (m_p, xr_p), ckv_p, kr_p, sg_p, sc_p = _mixer(x_prompt, pos_p, None, None, s0, c0, wp)
    (m_s, xr_s), ckv_s, kr_s, sg_s, sc_s = _mixer(x_sample, pos_s, cache_ckv[0], cache_k_rope[0], state_gdn[0],
                                                  state_conv[0], wp)

    n_p, n_s = b_p * s_p, b_s * s_s
    tm_s = _tile(int(np.gcd(n_p, n_s)), 512)
    x1_p = out_proj(m_p, xr_p, wp['w_out'], _tile(n_p, 1024), 512)
    x1_s = out_proj(m_s, xr_s, wp['w_out'], tm_s, 512)
    h_bf, eid, wgt, cnt = route(x1_p, x1_s, wp['ffn_norm_g'], wp['wr_hi'], wp['wr_lo'], wp['br'], tm_s)
    n_exp = wp['w_gate'].shape[0]
    counts = cnt[:, 0, :n_exp].sum(axis=0)
    ma, mb = _moe(h_bf, eid[:, :TOP_K], wgt[:, :TOP_K], counts, wp['w_gate'], wp['w_up'], wp['w_down'], MOE_BLOCK)
    yp = final_norm(x1_p, ma, mb, final_norm_g, _tile(n_p, 512), 0).reshape(b_p, s_p, d)
    ys = final_norm(x1_s, ma, mb, final_norm_g, tm_s, n_p).reshape(b_s, s_s, d)
    return (yp, ys, ckv_p[None], kr_p[None], sg_p[None], sc_p[None], ckv_s[None], kr_s[None], sg_s[None], sc_s[None])
```

```python
import functools

import numpy as np
import jax
import jax.numpy as jnp
from jax import lax
from jax.experimental import pallas as pl
from jax.experimental.pallas import tpu as pltpu

F32 = jnp.float32
BF16 = jnp.bfloat16

EPS = 1e-6
CHUNK = 64
ROPE_BASE = 10000.0
HEAD_DIM = 128
ROPE_DIM = 64
CONV_W = 4
TOP_K = 2
N_GROUPS = 8
EXPERTS_PER_GROUP = 8

LANES = 128
SUBLANES = 8
BF16_ROWS = 16
VMEM_LIMIT = 56 * 1024 * 1024
ATTN_TILE = 512
ATTN_HEADS_PER_STEP = 4
MOE_BLOCK = 256
IN_PROJ_TN = 1536
IN_PROJ_TM = 1024


def _cparams(sem):
    return pltpu.CompilerParams(dimension_semantics=sem, vmem_limit_bytes=VMEM_LIMIT)


def _dot(a, b):
    return jnp.dot(a, b, preferred_element_type=F32)


def _dot_nt(a, b):
    return lax.dot_general(a, b, (((1,), (1,)), ((), ())), preferred_element_type=F32)


def _dot_tn(a, b):
    return lax.dot_general(a, b, (((0,), (0,)), ((), ())), preferred_element_type=F32)


def _sigmoid(x):
    return 1.0 / (1.0 + jnp.exp(-x))


def _split_bf16(a):
    hi = a.astype(BF16)
    lo = (a - hi.astype(F32)).astype(BF16)
    return hi, lo


def _dot3(a_hi, a_lo, b_hi, b_lo):
    return _dot(a_hi, b_hi) + (_dot(a_hi, b_lo) + _dot(a_lo, b_hi))


def _rms_mm_body(x_ref, g_ref, w_ref, o_ref, h_ref):
    @pl.when(pl.program_id(1) == 0)
    def _():
        x = x_ref[...]
        ms = jnp.mean(x * x, axis=-1, keepdims=True)
        h_ref[...] = ((x * lax.rsqrt(ms + EPS)) * g_ref[...]).astype(BF16)

    o_ref[...] = _dot_nt(h_ref[...], w_ref[...])


def rms_matmul(x, g, w_t, tm, tn):
    n, k = x.shape
    m = w_t.shape[0]
    return pl.pallas_call(
        _rms_mm_body,
        grid=(n // tm, m // tn),
        in_specs=[pl.BlockSpec((tm, k), lambda i, j: (i, 0)),
                  pl.BlockSpec((1, k), lambda i, j: (0, 0)),
                  pl.BlockSpec((tn, k), lambda i, j: (j, 0))],
        out_specs=pl.BlockSpec((tm, tn), lambda i, j: (i, j)),
        out_shape=jax.ShapeDtypeStruct((n, m), F32),
        scratch_shapes=[pltpu.VMEM((tm, k), BF16)],
        compiler_params=_cparams(("parallel", "arbitrary")),
        name="rms_matmul",
    )(x, g.reshape(1, k), w_t)


def _conv_silu(x, prev8, w):
    row8 = lax.broadcasted_iota(jnp.int32, prev8.shape, 0)

    def shifted(k):
        xk = pltpu.roll(x, k, axis=0)
        top = jnp.where(row8 < k, pltpu.roll(prev8, k, axis=0), xk[0:SUBLANES])
        return jnp.concatenate([top, xk[SUBLANES:]], axis=0)

    y = shifted(CONV_W - 1) * w[0:1, :]
    for j in range(1, CONV_W - 1):
        y = y + shifted(CONV_W - 1 - j) * w[j:j + 1, :]
    y = y + x * w[CONV_W - 1:CONV_W, :]
    return y * _sigmoid(y)


def _mla_prep_body(ql_ref, kvl_ref, sm_ref, cs_ref, gq_ref, gkv_ref, wq_ref, *rest, n_heads, scale, with_kv):
    if with_kv:
        wk_ref, wvt_ref, ckv_ref, kr_ref, q_ref, k_ref, vt_ref = rest
    else:
        ckv_ref, kr_ref, q_ref = rest
    hd = HEAD_DIM
    ql = ql_ref[...]
    qn = ((ql * lax.rsqrt(jnp.mean(ql * ql, axis=-1, keepdims=True) + EPS)) * gq_ref[...]).astype(BF16)
    q_all = _dot(qn, wq_ref[...])
    kvl = kvl_ref[...]
    ckv = (kvl * lax.rsqrt(jnp.mean(kvl * kvl, axis=-1, keepdims=True) + EPS)) * gkv_ref[...]
    ckv_ref[...] = ckv
    if with_kv:
        ckv_b = ckv.astype(BF16)
        k_all = _dot(ckv_b, wk_ref[...])
        vt_all = _dot_nt(wvt_ref[...], ckv_b)

    cs = cs_ref[...]
    lane = lax.broadcasted_iota(jnp.int32, cs.shape, 1)

    def rope(t):
        t = t * cs
        return jnp.where(lane < ROPE_DIM, t + pltpu.roll(t, ROPE_DIM, axis=1), 0.0)

    kr = rope(sm_ref[:, 0:LANES])
    kr_ref[...] = kr
    kr_b = kr.astype(BF16)
    off = n_heads * hd
    for h in range(n_heads):
        q_ref[0, h, :, 0:hd] = (q_all[:, h * hd:(h + 1) * hd] * scale).astype(BF16)
        q_ref[0, h, :, hd:2 * hd] = (rope(q_all[:, off + h * hd:off + (h + 1) * hd]) * scale).astype(BF16)
        if with_kv:
            k_ref[0, h, :, 0:hd] = k_all[:, h * hd:(h + 1) * hd].astype(BF16)
            k_ref[0, h, :, hd:2 * hd] = kr_b
            vt_ref[0, h, 0, 0:hd, :] = vt_all[h * hd:(h + 1) * hd, :].astype(BF16)
            vt_ref[0, h, 0, hd:hd + BF16_ROWS, :] = jnp.ones((BF16_ROWS, vt_all.shape[1]), BF16)


def mla_prep(u, cs, gq, gkv, wq, wk, wvt, b, s, tm, t_att, col_ql, col_kvl, col_sm, n_heads, scale):
    lora = wq.shape[0]
    nt = s // tm
    row = lambda bi, si: bi * nt + si
    hd = HEAD_DIM
    with_kv = wk is not None
    full = lambda bi, si: (0, 0)
    in_specs = [pl.BlockSpec((tm, lora), lambda bi, si: (row(bi, si), col_ql)),
                pl.BlockSpec((tm, lora), lambda bi, si: (row(bi, si), col_kvl)),
                pl.BlockSpec((tm, 512), lambda bi, si: (row(bi, si), col_sm)),
                pl.BlockSpec((tm, LANES), lambda bi, si: (row(bi, si), 0)),
                pl.BlockSpec((1, lora), full),
                pl.BlockSpec((1, lora), full),
                pl.BlockSpec(wq.shape, full)]
    out_specs = [pl.BlockSpec((tm, lora), lambda bi, si: (row(bi, si), 0)),
                 pl.BlockSpec((tm, LANES), lambda bi, si: (row(bi, si), 0)),
                 pl.BlockSpec((1, n_heads, tm, 2 * hd), lambda bi, si: (bi, 0, si, 0))]
    out_shape = [jax.ShapeDtypeStruct((b * s, lora), F32),
                 jax.ShapeDtypeStruct((b * s, LANES), F32),
                 jax.ShapeDtypeStruct((b, n_heads, s, 2 * hd), BF16)]
    args = [u, u, u, cs, gq.reshape(1, lora), gkv.reshape(1, lora), wq]
    if with_kv:
        r = t_att // tm
        in_specs += [pl.BlockSpec(wk.shape, full), pl.BlockSpec(wvt.shape, full)]
        out_specs += [pl.BlockSpec((1, n_heads, tm, 2 * hd), lambda bi, si: (bi, 0, si, 0)),
                      pl.BlockSpec((1, n_heads, 1, hd + BF16_ROWS, tm),
                                   lambda bi, si: (bi, 0, si // r, 0, si % r))]
        out_shape += [jax.ShapeDtypeStruct((b, n_heads, s, 2 * hd), BF16),
                      jax.ShapeDtypeStruct((b, n_heads, s // t_att, hd + BF16_ROWS, t_att), BF16)]
        args += [wk, wvt]
    return pl.pallas_call(
        functools.partial(_mla_prep_body, n_heads=n_heads, scale=scale, with_kv=with_kv),
        grid=(b, nt),
        in_specs=in_specs,
        out_specs=out_specs,
        out_shape=out_shape,
        compiler_params=_cparams(("parallel", "parallel")),
        name="mla_prep",
    )(*args)


def _flash_body(q_ref, k_ref, vt_ref, o_ref, m_scr, acc_scr, *, t, gh):
    qi = pl.program_id(2)
    hd = HEAD_DIM
    m_scr[...] = jnp.full(m_scr.shape, -jnp.inf, F32)
    acc_scr[...] = jnp.zeros(acc_scr.shape, F32)

    def block(j, masked):
        start = pl.multiple_of(j * t, t)
        scs = [_dot_nt(k_ref[0, g, pl.ds(start, t), :], q_ref[0, g]) for g in range(gh)]
        for g in range(gh):
            sc = scs[g]
            if masked:
                krow = lax.broadcasted_iota(jnp.int32, sc.shape, 0)
                qcol = lax.broadcasted_iota(jnp.int32, sc.shape, 1)
                sc = jnp.where(krow // CHUNK <= qcol // CHUNK, sc, -jnp.inf)
            m_prev = m_scr[g]
            m_new = jnp.maximum(m_prev, jnp.max(sc, axis=0, keepdims=True))
            alpha = jnp.exp2(m_prev - m_new)
            p = jnp.exp2(sc - m_new)
            acc_scr[g] = alpha * acc_scr[g] + _dot(vt_ref[0, g, j], p.astype(BF16))
            m_scr[g] = m_new

    def full_block(j, carry):
        block(j, False)
        return carry

    lax.fori_loop(0, qi, full_block, 0)
    block(qi, True)
    for g in range(gh):
        acc = acc_scr[g]
        o_ref[0, :, g * hd:(g + 1) * hd] = jnp.transpose(acc[:hd] / acc[hd:hd + 1])


def flash_prompt(q, k, vt, t, gh):
    b, nh, s, dk = q.shape
    hd = HEAD_DIM
    hv = vt.shape[-2]
    assert t % CHUNK == 0 and s % t == 0 and nh % gh == 0 and vt.shape[-1] == t and hv > hd
    return pl.pallas_call(
        functools.partial(_flash_body, t=t, gh=gh),
        grid=(b, nh // gh, s // t),
        in_specs=[pl.BlockSpec((1, gh, t, dk), lambda bi, hp, qi: (bi, hp, qi, 0)),
                  pl.BlockSpec((1, gh, s, dk), lambda bi, hp, qi: (bi, hp, 0, 0)),
                  pl.BlockSpec((1, gh, s // t, hv, t), lambda bi, hp, qi: (bi, hp, 0, 0, 0))],
        out_specs=pl.BlockSpec((1, t, gh * hd), lambda bi, hp, qi: (bi, qi, hp)),
        out_shape=jax.ShapeDtypeStruct((b, s, nh * hd), F32),
        scratch_shapes=[pltpu.VMEM((gh, 1, t), F32), pltpu.VMEM((gh, hv, t), F32)],
        compiler_params=_cparams(("parallel", "parallel", "arbitrary")),
        name="flash_prompt",
    )(q, k, vt)


def _bmm_body(a_ref, b_ref, o_ref):
    o_ref[0] = _dot(a_ref[0].astype(BF16), b_ref[0])


def head_matmul(a, b):
    nh, m, k = a.shape
    n = b.shape[2]
    return pl.pallas_call(
        _bmm_body,
        grid=(nh,),
        in_specs=[pl.BlockSpec((1, m, k), lambda h: (h, 0, 0)),
                  pl.BlockSpec((1, k, n), lambda h: (h, 0, 0))],
        out_specs=pl.BlockSpec((1, m, n), lambda h: (h, 0, 0)),
        out_shape=jax.ShapeDtypeStruct((nh, m, n), F32),
        compiler_params=_cparams(("parallel",)),
        name="head_matmul",
    )(a, b)


def _attn_sample_body(qa_ref, qr_ref, kp_ref, krp_ref, kn_ref, krn_ref, o_ref, *, past, s_new, n_heads):
    qa = qa_ref[0].astype(BF16)
    qr = qr_ref[0].astype(BF16)
    kp = kp_ref[0].astype(BF16)
    krp_t = krp_ref[0].astype(BF16)
    kn = kn_ref[0].astype(BF16)
    krn = krn_ref[0].astype(BF16)
    s_past = _dot_nt(qa, kp) + _dot(qr[:, :ROPE_DIM], krp_t)
    s_n = _dot_nt(qa, kn) + _dot_nt(qr, krn)
    row = lax.broadcasted_iota(jnp.int32, s_n.shape, 0)
    col = lax.broadcasted_iota(jnp.int32, s_n.shape, 1)
    qpos = past + row // n_heads
    kpos = past + col
    valid = (col < s_new) & (kpos // CHUNK <= qpos // CHUNK)
    s_n = jnp.where(valid, s_n, -jnp.inf)
    m = jnp.maximum(jnp.max(s_past, axis=-1, keepdims=True), jnp.max(s_n, axis=-1, keepdims=True))
    pp = jnp.exp2(s_past - m)
    pn = jnp.exp2(s_n - m)
    l = jnp.sum(pp, axis=-1, keepdims=True) + jnp.sum(pn, axis=-1, keepdims=True)
    o_ref[0] = (_dot(pp.astype(BF16), kp) + _dot(pn.astype(BF16), kn)) / l


def attn_sample(qa, qr, ckv_past, kr_past, ckv_new, kr_new, s_new, n_heads):
    b, r, lora = qa.shape
    past = ckv_past.shape[1]
    return pl.pallas_call(
        functools.partial(_attn_sample_body, past=past, s_new=s_new, n_heads=n_heads),
        grid=(b,),
        in_specs=[pl.BlockSpec((1, r, lora), lambda i: (i, 0, 0)),
                  pl.BlockSpec((1, r, LANES), lambda i: (i, 0, 0)),
                  pl.BlockSpec((1, past, lora), lambda i: (i, 0, 0)),
                  pl.BlockSpec((1, ROPE_DIM, past), lambda i: (i, 0, 0)),
                  pl.BlockSpec((1, LANES, lora), lambda i: (i, 0, 0)),
                  pl.BlockSpec((1, LANES, LANES), lambda i: (i, 0, 0))],
        out_specs=pl.BlockSpec((1, r, lora), lambda i: (i, 0, 0)),
        out_shape=jax.ShapeDtypeStruct((b, r, lora), F32),
        compiler_params=_cparams(("parallel",)),
        name="attn_sample",
    )(qa, qr, ckv_past, kr_past, ckv_new, kr_new)


def _softplus(x):
    return jnp.maximum(x, 0.0) + jnp.log1p(jnp.exp(-jnp.abs(x)))


def _gdn_body(qkv_ref, z_ref, sm_ref, ga_ref, gb_ref, oa_ref, prev0_ref, s0_ref, convw_ref, alog_ref, dtb_ref,
              gn_ref, m_ref, sout_ref, s_scr, prev_scr, y_scr, *, L, n_heads):
    hd = HEAD_DIM
    heads = range(n_heads)
    c = pl.program_id(1)

    @pl.when(c == 0)
    def _():
        s_scr[...] = s0_ref[0]
        prev_scr[...] = prev0_ref[0]

    x = qkv_ref[...]
    y_scr[...] = _conv_silu(x, prev_scr[...], convw_ref[...])
    prev_scr[...] = x[L - SUBLANES:L]

    a = sm_ref[:, LANES:2 * LANES]
    bb = sm_ref[:, 2 * LANES:3 * LANES]
    g = -jnp.exp(alog_ref[...]) * _softplus(a + dtb_ref[...])
    beta = _sigmoid(bb)
    rowl = lax.broadcasted_iota(jnp.int32, g.shape, 0)
    cum = g
    d = 1
    while d < L:
        cum = cum + jnp.where(rowl >= d, pltpu.roll(cum, d, axis=0), 0.0)
        d *= 2
    cum_t = jnp.transpose(jnp.concatenate([cum, jnp.zeros((LANES - L, LANES), F32)], axis=0))

    row = lax.broadcasted_iota(jnp.int32, (L, L), 0)
    col = lax.broadcasted_iota(jnp.int32, (L, L), 1)
    tri = row >= col
    strict = row > col
    eye = jnp.where(row == col, 1.0, 0.0)
    gn = gn_ref[...]

    q_b, k_b, kb_b, rhs_b, qg_b, kd_b = [], [], [], [], [], []
    for h in heads:
        qh = y_scr[:, h * hd:(h + 1) * hd]
        kh = y_scr[:, (n_heads + h) * hd:(n_heads + h + 1) * hd]
        vh = y_scr[:, (2 * n_heads + h) * hd:(2 * n_heads + h + 1) * hd]
        qh = (qh * lax.rsqrt(jnp.sum(qh * qh, axis=-1, keepdims=True) + EPS)) * (hd ** -0.5)
        kh = kh * lax.rsqrt(jnp.sum(kh * kh, axis=-1, keepdims=True) + EPS)
        gcol = cum[:, h:h + 1]
        bcol = beta[:, h:h + 1]
        e_g = jnp.exp(gcol)
        kb = kh * bcol
        q_b.append(qh.astype(BF16))
        k_b.append(kh.astype(BF16))
        kb_b.append(kb.astype(BF16))
        rhs_b.append(jnp.concatenate([vh * bcol, kb * e_g], axis=1).astype(BF16))
        qg_b.append((qh * e_g).astype(BF16))
        kd_b.append((kh * jnp.exp(cum[L - 1:L, h:h + 1] - gcol)).astype(BF16))

    kk = [_dot_nt(kb_b[h], k_b[h]) for h in heads]
    qk = [_dot_nt(q_b[h], k_b[h]) for h in heads]

    nmat, attn_b = [], []
    for h in heads:
        diff = cum[:, h:h + 1] - cum_t[h:h + 1, 0:L]
        decay = jnp.where(tri, jnp.exp(jnp.where(tri, diff, 0.0)), 0.0)
        nmat.append(jnp.where(strict, kk[h] * decay, 0.0))
        attn_b.append(jnp.where(tri, qk[h] * decay, 0.0).astype(BF16))

    tmat = [eye - nmat[h] for h in heads]
    p_b = [nmat[h].astype(BF16) for h in heads]
    pmat = [_dot(p_b[h], p_b[h]) for h in heads]
    power = 2
    while True:
        p_b = [pmat[h].astype(BF16) for h in heads]
        tmat = [tmat[h] + _dot(tmat[h].astype(BF16), p_b[h]) for h in heads]
        power *= 2
        if power >= L:
            break
        pmat = [_dot(p_b[h], p_b[h]) for h in heads]

    uw = [_dot(tmat[h].astype(BF16), rhs_b[h]) for h in heads]
    s_old = [s_scr[h] for h in heads]
    s_b = [s_old[h].astype(BF16) for h in heads]
    ws = [_dot(jnp.concatenate([uw[h][:, hd:].astype(BF16), qg_b[h]], axis=0), s_b[h]) for h in heads]
    vn_b = [(uw[h][:, :hd] - ws[h][:L]).astype(BF16) for h in heads]
    av = [_dot(attn_b[h], vn_b[h]) for h in heads]
    ks = [_dot_tn(kd_b[h], vn_b[h]) for h in heads]
    for h in heads:
        sl = slice(h * hd, (h + 1) * hd)
        o = ws[h][L:] + av[h]
        s_scr[h] = s_old[h] * jnp.exp(cum[L - 1:L, h:h + 1]) + ks[h]
        zh = z_ref[:, sl]
        on = (o * lax.rsqrt(jnp.mean(o * o, axis=-1, keepdims=True) + EPS)) * gn
        ob = on * (zh * _sigmoid(zh))
        m_ref[:, sl] = (_sigmoid(ga_ref[:, sl]) * oa_ref[:, sl] + _sigmoid(gb_ref[:, sl]) * ob).astype(BF16)

    @pl.when(c == pl.num_programs(1) - 1)
    def _():
        sout_ref[0] = s_scr[...]


def gdn(u, o_a, prev0, s0, conv_w, a_log, dt_bias, gn, b, s, col_z, col_sm, col_ga, col_gb, n_heads):
    hd = HEAD_DIM
    L = min(s, CHUNK)
    nc = s // L
    qkv_w = 3 * n_heads * hd
    row = lambda bi, c: bi * nc + c
    alog = jnp.zeros((1, LANES), F32).at[0, :n_heads].set(a_log)
    dtb = jnp.zeros((1, LANES), F32).at[0, :n_heads].set(dt_bias)
    return pl.pallas_call(
        functools.partial(_gdn_body, L=L, n_heads=n_heads),
        grid=(b, nc),
        in_specs=[pl.BlockSpec((L, qkv_w), lambda bi, c: (row(bi, c), 0)),
                  pl.BlockSpec((L, n_heads * hd), lambda bi, c: (row(bi, c), col_z)),
                  pl.BlockSpec((L, 512), lambda bi, c: (row(bi, c), col_sm)),
                  pl.BlockSpec((L, n_heads * hd), lambda bi, c: (row(bi, c), col_ga)),
                  pl.BlockSpec((L, n_heads * hd), lambda bi, c: (row(bi, c), col_gb)),
                  pl.BlockSpec((L, n_heads * hd), lambda bi, c: (row(bi, c), 0)),
                  pl.BlockSpec((1, SUBLANES, qkv_w), lambda bi, c: (bi, 0, 0)),
                  pl.BlockSpec((1, n_heads, hd, hd), lambda bi, c: (bi, 0, 0, 0)),
                  pl.BlockSpec((CONV_W, qkv_w), lambda bi, c: (0, 0)),
                  pl.BlockSpec((1, LANES), lambda bi, c: (0, 0)),
                  pl.BlockSpec((1, LANES), lambda bi, c: (0, 0)),
                  pl.BlockSpec((1, hd), lambda bi, c: (0, 0))],
        out_specs=[pl.BlockSpec((L, n_heads * hd), lambda bi, c: (row(bi, c), 0)),
                   pl.BlockSpec((1, n_heads, hd, hd), lambda bi, c: (bi, 0, 0, 0))],
        out_shape=[jax.ShapeDtypeStruct((b * s, n_heads * hd), BF16),
                   jax.ShapeDtypeStruct((b, n_heads, hd, hd), F32)],
        scratch_shapes=[pltpu.VMEM((n_heads, hd, hd), F32),
                        pltpu.VMEM((SUBLANES, qkv_w), F32),
                        pltpu.VMEM((L, qkv_w), F32)],
        compiler_params=_cparams(("parallel", "arbitrary")),
        name="gdn",
    )(u, u, u, u, u, o_a, prev0, s0, conv_w, alog, dtb, gn.reshape(1, hd))


def _out_proj_body(m_ref, x_ref, w_ref, o_ref):
    o_ref[...] = x_ref[...] + _dot(m_ref[...], w_ref[...])


def out_proj(m, x, w, tm, tn):
    n, d = x.shape
    return pl.pallas_call(
        _out_proj_body,
        grid=(n // tm, d // tn),
        in_specs=[pl.BlockSpec((tm, d), lambda i, j: (i, 0)),
                  pl.BlockSpec((tm, tn), lambda i, j: (i, j)),
                  pl.BlockSpec((d, tn), lambda i, j: (0, j))],
        out_specs=pl.BlockSpec((tm, tn), lambda i, j: (i, j)),
        out_shape=jax.ShapeDtypeStruct((n, d), F32),
        compiler_params=_cparams(("parallel", "arbitrary")),
        name="out_proj",
    )(m, x, w)


def _route_body(xa_ref, xb_ref, g_ref, whi_ref, wlo_ref, bias_ref, h_ref, eid_ref, wgt_ref, cnt_ref, *, tiles_a):
    x = jnp.where(pl.program_id(0) < tiles_a, xa_ref[...], xb_ref[...])
    hn = (x * lax.rsqrt(jnp.mean(x * x, axis=-1, keepdims=True) + EPS)) * g_ref[...]
    h_hi, h_lo = _split_bf16(hn)
    h_ref[...] = h_hi
    lg = _dot3(h_hi, h_lo, whi_ref[...], wlo_ref[...]) + bias_ref[...]
    lane = lax.broadcasted_iota(jnp.int32, lg.shape, 1)
    big = jnp.int32(LANES)
    neg = -jnp.inf
    glog = jnp.where(lane < N_GROUPS, lg, neg)
    gmax = jnp.max(glog, axis=-1, keepdims=True)
    pg_top = 1.0 / jnp.sum(jnp.exp(glog - gmax), axis=-1, keepdims=True)
    grp = jnp.min(jnp.where(glog == gmax, lane, big), axis=-1, keepdims=True)
    lo = N_GROUPS + grp * EXPERTS_PER_GROUP
    le = jnp.where((lane >= lo) & (lane < lo + EXPERTS_PER_GROUP), lg, neg)
    v1 = jnp.max(le, axis=-1, keepdims=True)
    i1 = jnp.min(jnp.where(le == v1, lane, big), axis=-1, keepdims=True)
    le2 = jnp.where(lane == i1, neg, le)
    v2 = jnp.max(le2, axis=-1, keepdims=True)
    i2 = jnp.min(jnp.where(le2 == v2, lane, big), axis=-1, keepdims=True)
    e2 = jnp.exp(v2 - v1)
    den = 1.0 + e2
    w1 = pg_top * (1.0 / den)
    w2 = pg_top * (e2 / den)
    eid_ref[...] = jnp.where(lane == 0, i1 - N_GROUPS, jnp.where(lane == 1, i2 - N_GROUPS, 0))
    wgt_ref[...] = jnp.where(lane == 0, w1, jnp.where(lane == 1, w2, 0.0))
    hit = jnp.where((lane == i1 - N_GROUPS) | (lane == i2 - N_GROUPS), 1.0, 0.0)
    cnt_ref[0] = jnp.broadcast_to(jnp.sum(hit, axis=0, keepdims=True), cnt_ref.shape[1:]).astype(jnp.int32)


def route(xa, xb, g, whi, wlo, bias, tm):
    d = xa.shape[1]
    ta, tb = xa.shape[0] // tm, xb.shape[0] // tm
    n = (ta + tb) * tm
    return pl.pallas_call(
        functools.partial(_route_body, tiles_a=ta),
        grid=(ta + tb,),
        in_specs=[pl.BlockSpec((tm, d), lambda i: (jnp.minimum(i, ta - 1), 0)),
                  pl.BlockSpec((tm, d), lambda i: (jnp.maximum(i - ta, 0), 0)),
                  pl.BlockSpec((1, d), lambda i: (0, 0)),
                  pl.BlockSpec((d, LANES), lambda i: (0, 0)),
                  pl.BlockSpec((d, LANES), lambda i: (0, 0)),
                  pl.BlockSpec((1, LANES), lambda i: (0, 0))],
        out_specs=[pl.BlockSpec((tm, d), lambda i: (i, 0)),
                   pl.BlockSpec((tm, LANES), lambda i: (i, 0)),
                   pl.BlockSpec((tm, LANES), lambda i: (i, 0)),
                   pl.BlockSpec((1, SUBLANES, LANES), lambda i: (i, 0, 0))],
        out_shape=[jax.ShapeDtypeStruct((n, d), BF16),
                   jax.ShapeDtypeStruct((n, LANES), jnp.int32),
                   jax.ShapeDtypeStruct((n, LANES), F32),
                   jax.ShapeDtypeStruct((n // tm, SUBLANES, LANES), jnp.int32)],
        compiler_params=_cparams(("parallel",)),
        name="route",
    )(xa, xb, g.reshape(1, d), whi, wlo, bias)


def _expert_body(be_tab, first_tab, slot_tab, next_tab, nused, x_ref, sw_ref, wg_hbm, wu_hbm, wd_hbm, o_ref,
                 stg_g, stg_u, stg_d, wg_s, wu_s, wd_s, sem):
    blk = pl.program_id(0)

    def weight_copies(e, slot):
        return (pltpu.make_async_copy(wg_hbm.at[e], stg_g.at[slot], sem.at[slot, 0]),
                pltpu.make_async_copy(wu_hbm.at[e], stg_u.at[slot], sem.at[slot, 1]),
                pltpu.make_async_copy(wd_hbm.at[e], stg_d.at[slot], sem.at[slot, 2]))

    @pl.when((blk == 0) & (nused[0] > 0))
    def _():
        for cp in weight_copies(be_tab[0], 0):
            cp.start()

    @pl.when(blk < nused[0])
    def _():
        @pl.when(first_tab[blk] == 1)
        def _():
            slot = slot_tab[blk]

            @pl.when(next_tab[blk] >= 0)
            def _():
                for cp in weight_copies(next_tab[blk], 1 - slot):
                    cp.start()

            for cp in weight_copies(be_tab[blk], slot):
                cp.wait()
            wg_s[...] = stg_g[slot].astype(BF16)
            wu_s[...] = stg_u[slot].astype(BF16)
            wd_s[...] = stg_d[slot].astype(BF16)

        x = x_ref[...]
        gate = _dot(x, wg_s[...])
        up = _dot(x, wu_s[...])
        hid = (gate * _sigmoid(gate)) * up
        o_ref[...] = (_dot(hid.astype(BF16), wd_s[...]) * sw_ref[...]).astype(o_ref.dtype)

    @pl.when(blk >= nused[0])
    def _():
        o_ref[...] = jnp.zeros(o_ref.shape, o_ref.dtype)


def experts(x_sorted, slot_w, block_e, first, slot_tab, next_tab, nused, w_gate, w_up, w_down, bm):
    n_slots, d = x_sorted.shape
    de = w_gate.shape[2]
    n_blocks = n_slots // bm
    row = lambda i, *_: (i, 0)
    grid_spec = pltpu.PrefetchScalarGridSpec(
        num_scalar_prefetch=5,
        grid=(n_blocks,),
        in_specs=[pl.BlockSpec((bm, d), row),
                  pl.BlockSpec((bm, 1), row),
                  pl.BlockSpec(memory_space=pl.ANY),
                  pl.BlockSpec(memory_space=pl.ANY),
                  pl.BlockSpec(memory_space=pl.ANY)],
        out_specs=pl.BlockSpec((bm, d), row),
        scratch_shapes=[pltpu.VMEM((2, d, de), F32), pltpu.VMEM((2, d, de), F32), pltpu.VMEM((2, de, d), F32),
                        pltpu.VMEM((d, de), BF16), pltpu.VMEM((d, de), BF16), pltpu.VMEM((de, d), BF16),
                        pltpu.SemaphoreType.DMA((2, 3))],
    )
    return pl.pallas_call(
        _expert_body,
        grid_spec=grid_spec,
        out_shape=jax.ShapeDtypeStruct((n_slots, d), BF16),
        compiler_params=_cparams(("arbitrary",)),
        name="experts",
    )(block_e, first, slot_tab, next_tab, nused, x_sorted, slot_w, w_gate, w_up, w_down)


def _final_body(x_ref, ma_ref, mb_ref, g_ref, o_ref):
    x = x_ref[...] + (ma_ref[...].astype(F32) + mb_ref[...].astype(F32))
    o_ref[...] = (x * lax.rsqrt(jnp.mean(x * x, axis=-1, keepdims=True) + EPS)) * g_ref[...]


def final_norm(x, ma, mb, g, tm, row0):
    n, d = x.shape
    assert row0 % tm == 0 and n % tm == 0
    spec = pl.BlockSpec((tm, d), lambda i: (i, 0))
    mspec = pl.BlockSpec((tm, d), lambda i: (i + row0 // tm, 0))
    return pl.pallas_call(
        _final_body,
        grid=(n // tm,),
        in_specs=[spec, mspec, mspec, pl.BlockSpec((1, d), lambda i: (0, 0))],
        out_specs=spec,
        out_shape=jax.ShapeDtypeStruct((n, d), F32),
        compiler_params=_cparams(("parallel",)),
        name="final_norm",
    )(x, ma, mb, g.reshape(1, d))


def _tile(n, pref):
    t = min(n, pref)
    while n % t:
        t //= 2
    return t


def _rope_table(pos, b):
    half = ROPE_DIM // 2
    inv = ROPE_BASE ** (-jnp.arange(half, dtype=F32) / half)
    ang = pos.astype(F32)[:, None] * inv[None, :]
    cos, sin = jnp.cos(ang), jnp.sin(ang)
    cs = jnp.concatenate([cos, cos, sin, sin], axis=-1)
    return jnp.tile(cs, (b, 1))


def _rot_cols(w):
    half = ROPE_DIM // 2
    return jnp.concatenate([-w[..., half:], w[..., :half]], axis=-1)


def _moe(h_bf, eid, wgt, counts, w_gate, w_up, w_down, bm):
    n, d = h_bf.shape
    n_exp = w_gate.shape[0]
    a = n * TOP_K
    e_flat = eid.reshape(-1)
    w_flat = wgt.reshape(-1)
    padded = (counts + bm - 1) // bm * bm
    pend = jnp.cumsum(padded)
    n_blocks = (a + n_exp * (bm - 1) + bm - 1) // bm
    n_slots = n_blocks * bm
    block_e = jnp.minimum(jnp.searchsorted(pend, jnp.arange(n_blocks, dtype=jnp.int32) * bm, side='right'),
                          n_exp - 1).astype(jnp.int32)
    first = jnp.concatenate([jnp.ones((1,), jnp.int32), (block_e[1:] != block_e[:-1]).astype(jnp.int32)])
    nused = (pend[-1] // bm).astype(jnp.int32).reshape(1)
    slot_tab = (jnp.cumsum(first) - 1) % 2
    e_ids = jnp.arange(n_exp, dtype=jnp.int32)
    later = lax.cummin(jnp.where(counts > 0, e_ids, n_exp), axis=0, reverse=True)
    next_e = jnp.concatenate([later[1:], jnp.full((1,), n_exp, jnp.int32)])
    next_tab = jnp.where(next_e < n_exp, next_e, -1)[block_e].astype(jnp.int32)
    n_pad = n_slots - a
    pad_end = jnp.cumsum(padded - counts)
    pad_key = jnp.sum(jnp.arange(n_pad, dtype=jnp.int32)[None, :] >= pad_end[:, None], axis=0, dtype=jnp.int32)
    keys = jnp.concatenate([e_flat, pad_key])
    toks = jnp.concatenate([jnp.arange(a, dtype=jnp.int32) // TOP_K, jnp.arange(n_pad, dtype=jnp.int32) % n])
    wts = jnp.concatenate([w_flat, jnp.zeros((n_pad,), F32)])
    iota = jnp.arange(n_slots, dtype=jnp.int32)
    _, slot_tok, slot_w, slot_src = lax.sort((keys, toks, wts, iota), num_keys=1, is_stable=True)
    x_sorted = h_bf[slot_tok]
    out = experts(x_sorted, slot_w.reshape(n_slots, 1), block_e, first, slot_tab.astype(jnp.int32), next_tab, nused,
                  w_gate, w_up, w_down, bm)
    _, slot_of = lax.sort((slot_src, iota), num_keys=1)
    slot_of = slot_of[:a].reshape(n, TOP_K)
    return out[slot_of[:, 0]], out[slot_of[:, 1]]


def _mixer(x3, pos, ckv_past, kr_past, s0, conv0, wp):
    b, s, d = x3.shape
    n = b * s
    n_heads = wp['n_heads']
    hd = HEAD_DIM
    x = x3.reshape(n, d)
    cols = wp['cols']

    u = rms_matmul(x, wp['attn_norm_g'], wp['w_all'], _tile(n, IN_PROJ_TM), IN_PROJ_TN)

    cs = _rope_table(pos, b)
    scale = (hd + ROPE_DIM) ** -0.5 * float(np.log2(np.e))
    prompt = ckv_past is None
    t_att = _tile(s, ATTN_TILE)
    tm = min(_tile(s, 256), t_att)
    res = mla_prep(u, cs, wp['q_norm_g'], wp['kv_norm_g'], wp['wq_ext'], wp['wk'] if prompt else None,
                   wp['wvt'] if prompt else None, b, s, tm, t_att, cols['q_lat'] // 512, cols['kv_lat'] // 512,
                   cols['small'] // 512, n_heads, scale)
    ckv, kr_pad, q = res[:3]
    kr = kr_pad[:, :ROPE_DIM]
    lora = ckv.shape[1]
    if prompt:
        o_a = flash_prompt(q, res[3], res[4], t_att, ATTN_HEADS_PER_STEP).reshape(n, n_heads * hd)
    else:
        qn = q[..., :hd].transpose(1, 0, 2, 3).reshape(n_heads, n, hd)
        q_abs = head_matmul(qn, wp['w_uk_t'])
        q_abs = q_abs.reshape(n_heads, b, s, lora).transpose(1, 2, 0, 3).reshape(b, s * n_heads, lora)
        qr = q[..., hd:].transpose(0, 2, 1, 3).reshape(b, s * n_heads, hd)
        kr_past_t = jnp.swapaxes(kr_past, 1, 2)
        ckv_new = jnp.pad(ckv.reshape(b, s, lora), ((0, 0), (0, LANES - s), (0, 0)))
        kr_new = jnp.pad(kr_pad.reshape(b, s, LANES), ((0, 0), (0, LANES - s), (0, 0)))
        o_lat = attn_sample(q_abs, qr, ckv_past, kr_past_t, ckv_new, kr_new, s, n_heads)
        o_lat = o_lat.reshape(b, s, n_heads, lora).transpose(2, 0, 1, 3).reshape(n_heads, n, lora)
        o_h = head_matmul(o_lat, wp['w_uv_h'])
        o_a = o_h.reshape(n_heads, n, hd).transpose(1, 0, 2).reshape(n, n_heads * hd)

    qkv_w = 3 * n_heads * hd
    prev0 = jnp.pad(conv0, ((0, 0), (SUBLANES - (CONV_W - 1), 0), (0, 0)))
    merged, s_new = gdn(u, o_a, prev0, s0, wp['conv_w'], wp['a_log'], wp['dt_bias'], wp['gdn_norm_g'], b, s,
                        cols['z'] // d, cols['small'] // 512, cols['gate_a'] // d, cols['gate_b'] // d, n_heads)
    tail = u.reshape(b, s, -1)[:, max(s - (CONV_W - 1), 0):, :qkv_w]
    conv_new = jnp.concatenate([conv0, tail], axis=1)[:, -(CONV_W - 1):]

    return (merged, x), ckv.reshape(b, s, lora), kr.reshape(b, s, ROPE_DIM), s_new, conv_new


def _prep_weights(l, attn_norm_g, w_in, q_norm_g, w_uq, kv_norm_g, w_uk, w_uv, conv_w, a_log, dt_bias, gdn_norm_g,
                  w_out, ffn_norm_g, w_group, b_group, w_router, b_router, w_gate, w_up, w_down):
    d = w_in.shape[1]
    q_lora, n_heads, _ = w_uq.shape[1:]
    kv_lora = w_uk.shape[1]
    hd = HEAD_DIM
    qkv_w = 3 * n_heads * hd
    sizes = (q_lora, kv_lora, ROPE_DIM, qkv_w, n_heads, n_heads, n_heads * hd, d, d)
    offs = np.concatenate([[0], np.cumsum(sizes)])
    wi_t = jnp.swapaxes(w_in[l], 0, 1)
    part = lambda i: wi_t[offs[i]:offs[i + 1]]
    w_qlat, w_kvlat, w_kr, w_qkv, w_a, w_b, w_z, w_ga, w_gb = [part(i) for i in range(9)]
    zpad = lambda c: jnp.zeros((c, d), F32)
    half = ROPE_DIM // 2
    w_kr_rot = jnp.concatenate([-w_kr[half:], w_kr[:half]], axis=0)
    small = jnp.concatenate([w_kr, w_kr_rot, w_a, zpad(LANES - n_heads), w_b, zpad(LANES - n_heads),
                             zpad(LANES)], axis=0)
    w_all = jnp.concatenate([w_qkv, w_z, w_ga, w_gb, w_qlat, w_kvlat, small], axis=0).astype(BF16)
    cols = {'qkv': 0, 'z': qkv_w, 'gate_a': qkv_w + n_heads * hd, 'gate_b': qkv_w + n_heads * hd + d,
            'q_lat': qkv_w + n_heads * hd + 2 * d}
    cols['kv_lat'] = cols['q_lat'] + q_lora
    cols['small'] = cols['kv_lat'] + kv_lora
    wq = w_uq[l]
    wq_nope = wq[:, :, :hd].reshape(q_lora, n_heads * hd)
    wq_r = wq[:, :, hd:]
    wq_rope = jnp.concatenate([wq_r, _rot_cols(wq_r)], axis=-1).reshape(q_lora, n_heads * hd)
    wq_ext = jnp.concatenate([wq_nope, wq_rope], axis=1).astype(BF16)
    wk = w_uk[l].reshape(kv_lora, n_heads * hd).astype(BF16)
    wvt = w_uv[l].reshape(kv_lora, n_heads * hd).T.astype(BF16)
    wr = jnp.concatenate([w_group[l], w_router[l].transpose(1, 0, 2).reshape(d, -1)], axis=1)
    wr = jnp.pad(wr, ((0, 0), (0, LANES - wr.shape[1])))
    wr_hi = wr.astype(BF16)
    wr_lo = (wr - wr_hi.astype(F32)).astype(BF16)
    br = jnp.concatenate([b_group[l], b_router[l].reshape(-1)])
    br = jnp.pad(br, (0, LANES - br.shape[0])).reshape(1, LANES)
    return {
        'n_heads': n_heads, 'cols': cols, 'w_all': w_all, 'attn_norm_g': attn_norm_g[l],
        'q_norm_g': q_norm_g[l], 'kv_norm_g': kv_norm_g[l], 'wq_ext': wq_ext, 'wk': wk, 'wvt': wvt,
        'w_uk_t': w_uk[l].transpose(1, 2, 0).astype(BF16),
        'w_uv_h': w_uv[l].transpose(1, 0, 2).astype(BF16),
        'conv_w': conv_w[l], 'a_log': a_log[l], 'dt_bias': dt_bias[l], 'gdn_norm_g': gdn_norm_g[l],
        'w_out': w_out[l].astype(BF16), 'ffn_norm_g': ffn_norm_g[l], 'wr_hi': wr_hi, 'wr_lo': wr_lo, 'br': br,
        'w_gate': w_gate[l], 'w_up': w_up[l], 'w_down': w_down[l],
    }


def kernel(x_prompt, x_sample, cache_ckv, cache_k_rope, state_gdn, state_conv, attn_norm_g, w_in, q_norm_g, w_uq, kv_norm_g, w_uk, w_uv, conv_w, a_log, dt_bias, gdn_norm_g, w_out, ffn_norm_g, w_group, b_group, w_router, b_router, w_gate, w_up, w_down, final_norm_g):
    depth = w_in.shape[0]
    assert depth == 1, "final RMSNorm is fused into the last layer; deeper trunks are not supported"
    b_p, s_p, d = x_prompt.shape
    b_s, s_s, _ = x_sample.shape
    past = cache_ckv.shape[2]
    n_heads = w_uq.shape[2]
    qkv_w = 3 * n_heads * HEAD_DIM
    pos_p = jnp.arange(s_p, dtype=jnp.int32)
    pos_s = past + jnp.arange(s_s, dtype=jnp.int32)
    wp = _prep_weights(0, attn_norm_g, w_in, q_norm_g, w_uq, kv_norm_g, w_uk, w_uv, conv_w, a_log, dt_bias,
                       gdn_norm_g, w_out, ffn_norm_g, w_group, b_group, w_router, b_router, w_gate, w_up, w_down)
    s0 = jnp.zeros((b_p, n_heads, HEAD_DIM, HEAD_DIM), F32)
    c0 = jnp.zeros((b_p, CONV_W - 1, qkv_w), F32)
    (m_p, xr_p), ckv_p, kr_p, sg_p, sc_p = _mixer(x_prompt, pos_p, None, None, s0, c0, wp)
    (m_s, xr_s), ckv_s, kr_s, sg_s, sc_s = _mixer(x_sample, pos_s, cache_ckv[0], cache_k_rope[0], state_gdn[0],
                                                  state_conv[0], wp)

    n_p, n_s = b_p * s_p, b_s * s_s
    tm_s = _tile(int(np.gcd(n_p, n_s)), 512)
    x1_p = out_proj(m_p, xr_p, wp['w_out'], _tile(n_p, 1024), 1024)
    x1_s = out_proj(m_s, xr_s, wp['w_out'], tm_s, 512)
    h_bf, eid, wgt, cnt = route(x1_p, x1_s, wp['ffn_norm_g'], wp['wr_hi'], wp['wr_lo'], wp['br'], tm_s)
    n_exp = wp['w_gate'].shape[0]
    counts = cnt[:, 0, :n_exp].sum(axis=0)
    ma, mb = _moe(h_bf, eid[:, :TOP_K], wgt[:, :TOP_K], counts, wp['w_gate'], wp['w_up'], wp['w_down'], MOE_BLOCK)
    yp = final_norm(x1_p, ma, mb, final_norm_g, _tile(n_p, 512), 0).reshape(b_p, s_p, d)
    ys = final_norm(x1_s, ma, mb, final_norm_g, tm_s, n_p).reshape(b_s, s_s, d)
    return (yp, ys, ckv_p[None], kr_p[None], sg_p[None], sc_p[None], ckv_s[None], kr_s[None], sg_s[None], sc_s[None])
```

```python
import functools

import numpy as np
import jax
import jax.numpy as jnp
from jax import lax
from jax.experimental import pallas as pl
from jax.experimental.pallas import tpu as pltpu

F32 = jnp.float32
BF16 = jnp.bfloat16

EPS = 1e-6
CHUNK = 64
ROPE_BASE = 10000.0
HEAD_DIM = 128
ROPE_DIM = 64
CONV_W = 4
TOP_K = 2
N_GROUPS = 8
EXPERTS_PER_GROUP = 8

LANES = 128
SUBLANES = 8
BF16_ROWS = 16
VMEM_LIMIT = 56 * 1024 * 1024
ATTN_TILE = 512
ATTN_HEADS_PER_STEP = 4
MOE_BLOCK = 256
IN_PROJ_TN = 1536
IN_PROJ_TM = 1024


def _cparams(sem):
    return pltpu.CompilerParams(dimension_semantics=sem, vmem_limit_bytes=VMEM_LIMIT)


def _dot(a, b):
    return jnp.dot(a, b, preferred_element_type=F32)


def _dot_nt(a, b):
    return lax.dot_general(a, b, (((1,), (1,)), ((), ())), preferred_element_type=F32)


def _dot_tn(a, b):
    return lax.dot_general(a, b, (((0,), (0,)), ((), ())), preferred_element_type=F32)


def _sigmoid(x):
    return 1.0 / (1.0 + jnp.exp(-x))


def _split_bf16(a):
    hi = a.astype(BF16)
    lo = (a - hi.astype(F32)).astype(BF16)
    return hi, lo


def _dot3(a_hi, a_lo, b_hi, b_lo):
    return _dot(a_hi, b_hi) + (_dot(a_hi, b_lo) + _dot(a_lo, b_hi))


def _rms_mm_body(x_ref, g_ref, w_ref, o_ref, h_ref):
    @pl.when(pl.program_id(1) == 0)
    def _():
        x = x_ref[...]
        ms = jnp.mean(x * x, axis=-1, keepdims=True)
        h_ref[...] = ((x * lax.rsqrt(ms + EPS)) * g_ref[...]).astype(BF16)

    o_ref[...] = _dot_nt(h_ref[...], w_ref[...])


def rms_matmul(x, g, w_t, tm, tn):
    n, k = x.shape
    m = w_t.shape[0]
    return pl.pallas_call(
        _rms_mm_body,
        grid=(n // tm, m // tn),
        in_specs=[pl.BlockSpec((tm, k), lambda i, j: (i, 0)),
                  pl.BlockSpec((1, k), lambda i, j: (0, 0)),
                  pl.BlockSpec((tn, k), lambda i, j: (j, 0))],
        out_specs=pl.BlockSpec((tm, tn), lambda i, j: (i, j)),
        out_shape=jax.ShapeDtypeStruct((n, m), F32),
        scratch_shapes=[pltpu.VMEM((tm, k), BF16)],
        compiler_params=_cparams(("parallel", "arbitrary")),
        name="rms_matmul",
    )(x, g.reshape(1, k), w_t)


def _conv_silu(x, prev8, w):
    row8 = lax.broadcasted_iota(jnp.int32, prev8.shape, 0)

    def shifted(k):
        xk = pltpu.roll(x, k, axis=0)
        top = jnp.where(row8 < k, pltpu.roll(prev8, k, axis=0), xk[0:SUBLANES])
        return jnp.concatenate([top, xk[SUBLANES:]], axis=0)

    y = shifted(CONV_W - 1) * w[0:1, :]
    for j in range(1, CONV_W - 1):
        y = y + shifted(CONV_W - 1 - j) * w[j:j + 1, :]
    y = y + x * w[CONV_W - 1:CONV_W, :]
    return y * _sigmoid(y)


def _mla_prep_body(ql_ref, kvl_ref, sm_ref, cs_ref, gq_ref, gkv_ref, wq_ref, *rest, n_heads, scale, with_kv):
    if with_kv:
        wk_ref, wvt_ref, ckv_ref, kr_ref, q_ref, k_ref, vt_ref = rest
    else:
        ckv_ref, kr_ref, q_ref = rest
    hd = HEAD_DIM
    ql = ql_ref[...]
    qn = ((ql * lax.rsqrt(jnp.mean(ql * ql, axis=-1, keepdims=True) + EPS)) * gq_ref[...]).astype(BF16)
    q_all = _dot(qn, wq_ref[...])
    kvl = kvl_ref[...]
    ckv = (kvl * lax.rsqrt(jnp.mean(kvl * kvl, axis=-1, keepdims=True) + EPS)) * gkv_ref[...]
    ckv_ref[...] = ckv
    if with_kv:
        ckv_b = ckv.astype(BF16)
        k_all = _dot(ckv_b, wk_ref[...])
        vt_all = _dot_nt(wvt_ref[...], ckv_b)

    cs = cs_ref[...]
    lane = lax.broadcasted_iota(jnp.int32, cs.shape, 1)

    def rope(t):
        t = t * cs
        return jnp.where(lane < ROPE_DIM, t + pltpu.roll(t, ROPE_DIM, axis=1), 0.0)

    kr = rope(sm_ref[:, 0:LANES])
    kr_ref[...] = kr
    kr_b = kr.astype(BF16)
    off = n_heads * hd
    for h in range(n_heads):
        q_ref[0, h, :, 0:hd] = (q_all[:, h * hd:(h + 1) * hd] * scale).astype(BF16)
        q_ref[0, h, :, hd:2 * hd] = (rope(q_all[:, off + h * hd:off + (h + 1) * hd]) * scale).astype(BF16)
        if with_kv:
            k_ref[0, h, :, 0:hd] = k_all[:, h * hd:(h + 1) * hd].astype(BF16)
            k_ref[0, h, :, hd:2 * hd] = kr_b
            vt_ref[0, h, 0, 0:hd, :] = vt_all[h * hd:(h + 1) * hd, :].astype(BF16)
            vt_ref[0, h, 0, hd:hd + BF16_ROWS, :] = jnp.ones((BF16_ROWS, vt_all.shape[1]), BF16)


def mla_prep(u, cs, gq, gkv, wq, wk, wvt, b, s, tm, t_att, col_ql, col_kvl, col_sm, n_heads, scale):
    lora = wq.shape[0]
    nt = s // tm
    row = lambda bi, si: bi * nt + si
    hd = HEAD_DIM
    with_kv = wk is not None
    full = lambda bi, si: (0, 0)
    in_specs = [pl.BlockSpec((tm, lora), lambda bi, si: (row(bi, si), col_ql)),
                pl.BlockSpec((tm, lora), lambda bi, si: (row(bi, si), col_kvl)),
                pl.BlockSpec((tm, 512), lambda bi, si: (row(bi, si), col_sm)),
                pl.BlockSpec((tm, LANES), lambda bi, si: (row(bi, si), 0)),
                pl.BlockSpec((1, lora), full),
                pl.BlockSpec((1, lora), full),
                pl.BlockSpec(wq.shape, full)]
    out_specs = [pl.BlockSpec((tm, lora), lambda bi, si: (row(bi, si), 0)),
                 pl.BlockSpec((tm, LANES), lambda bi, si: (row(bi, si), 0)),
                 pl.BlockSpec((1, n_heads, tm, 2 * hd), lambda bi, si: (bi, 0, si, 0))]
    out_shape = [jax.ShapeDtypeStruct((b * s, lora), F32),
                 jax.ShapeDtypeStruct((b * s, LANES), F32),
                 jax.ShapeDtypeStruct((b, n_heads, s, 2 * hd), BF16)]
    args = [u, u, u, cs, gq.reshape(1, lora), gkv.reshape(1, lora), wq]
    if with_kv:
        r = t_att // tm
        in_specs += [pl.BlockSpec(wk.shape, full), pl.BlockSpec(wvt.shape, full)]
        out_specs += [pl.BlockSpec((1, n_heads, tm, 2 * hd), lambda bi, si: (bi, 0, si, 0)),
                      pl.BlockSpec((1, n_heads, 1, hd + BF16_ROWS, tm),
                                   lambda bi, si: (bi, 0, si // r, 0, si % r))]
        out_shape += [jax.ShapeDtypeStruct((b, n_heads, s, 2 * hd), BF16),
                      jax.ShapeDtypeStruct((b, n_heads, s // t_att, hd + BF16_ROWS, t_att), BF16)]
        args += [wk, wvt]
    return pl.pallas_call(
        functools.partial(_mla_prep_body, n_heads=n_heads, scale=scale, with_kv=with_kv),
        grid=(b, nt),
        in_specs=in_specs,
        out_specs=out_specs,
        out_shape=out_shape,
        compiler_params=_cparams(("parallel", "parallel")),
        name="mla_prep",
    )(*args)


def _flash_body(q_ref, k_ref, vt_ref, o_ref, m_scr, acc_scr, *, t, gh):
    qi = pl.program_id(2)
    hd = HEAD_DIM
    m_scr[...] = jnp.full(m_scr.shape, -jnp.inf, F32)
    acc_scr[...] = jnp.zeros(acc_scr.shape, F32)

    def block(j, masked):
        start = pl.multiple_of(j * t, t)
        scs = [_dot_nt(k_ref[0, g, pl.ds(start, t), :], q_ref[0, g]) for g in range(gh)]
        for g in range(gh):
            sc = scs[g]
            if masked:
                krow = lax.broadcasted_iota(jnp.int32, sc.shape, 0)
                qcol = lax.broadcasted_iota(jnp.int32, sc.shape, 1)
                sc = jnp.where(krow // CHUNK <= qcol // CHUNK, sc, -jnp.inf)
            m_prev = m_scr[g]
            m_new = jnp.maximum(m_prev, jnp.max(sc, axis=0, keepdims=True))
            alpha = jnp.exp2(m_prev - m_new)
            p = jnp.exp2(sc - m_new)
            acc_scr[g] = alpha * acc_scr[g] + _dot(vt_ref[0, g, j], p.astype(BF16))
            m_scr[g] = m_new

    def full_block(j, carry):
        block(j, False)
        return carry

    lax.fori_loop(0, qi, full_block, 0)
    block(qi, True)
    for g in range(gh):
        acc = acc_scr[g]
        o_ref[0, :, g * hd:(g + 1) * hd] = jnp.transpose(acc[:hd] / acc[hd:hd + 1])


def flash_prompt(q, k, vt, t, gh):
    b, nh, s, dk = q.shape
    hd = HEAD_DIM
    hv = vt.shape[-2]
    assert t % CHUNK == 0 and s % t == 0 and nh % gh == 0 and vt.shape[-1] == t and hv > hd
    return pl.pallas_call(
        functools.partial(_flash_body, t=t, gh=gh),
        grid=(b, nh // gh, s // t),
        in_specs=[pl.BlockSpec((1, gh, t, dk), lambda bi, hp, qi: (bi, hp, qi, 0)),
                  pl.BlockSpec((1, gh, s, dk), lambda bi, hp, qi: (bi, hp, 0, 0)),
                  pl.BlockSpec((1, gh, s // t, hv, t), lambda bi, hp, qi: (bi, hp, 0, 0, 0))],
        out_specs=pl.BlockSpec((1, t, gh * hd), lambda bi, hp, qi: (bi, qi, hp)),
        out_shape=jax.ShapeDtypeStruct((b, s, nh * hd), F32),
        scratch_shapes=[pltpu.VMEM((gh, 1, t), F32), pltpu.VMEM((gh, hv, t), F32)],
        compiler_params=_cparams(("parallel", "parallel", "arbitrary")),
        name="flash_prompt",
    )(q, k, vt)


def _bmm_body(a_ref, b_ref, o_ref):
    o_ref[0] = _dot(a_ref[0].astype(BF16), b_ref[0])


def head_matmul(a, b):
    nh, m, k = a.shape
    n = b.shape[2]
    return pl.pallas_call(
        _bmm_body,
        grid=(nh,),
        in_specs=[pl.BlockSpec((1, m, k), lambda h: (h, 0, 0)),
                  pl.BlockSpec((1, k, n), lambda h: (h, 0, 0))],
        out_specs=pl.BlockSpec((1, m, n), lambda h: (h, 0, 0)),
        out_shape=jax.ShapeDtypeStruct((nh, m, n), F32),
        compiler_params=_cparams(("parallel",)),
        name="head_matmul",
    )(a, b)


def _attn_sample_body(qa_ref, qr_ref, kp_ref, krp_ref, kn_ref, krn_ref, o_ref, *, past, s_new, n_heads):
    qa = qa_ref[0].astype(BF16)
    qr = qr_ref[0].astype(BF16)
    kp = kp_ref[0].astype(BF16)
    krp_t = krp_ref[0].astype(BF16)
    kn = kn_ref[0].astype(BF16)
    krn = krn_ref[0].astype(BF16)
    s_past = _dot_nt(qa, kp) + _dot(qr[:, :ROPE_DIM], krp_t)
    s_n = _dot_nt(qa, kn) + _dot_nt(qr, krn)
    row = lax.broadcasted_iota(jnp.int32, s_n.shape, 0)
    col = lax.broadcasted_iota(jnp.int32, s_n.shape, 1)
    qpos = past + row // n_heads
    kpos = past + col
    valid = (col < s_new) & (kpos // CHUNK <= qpos // CHUNK)
    s_n = jnp.where(valid, s_n, -jnp.inf)
    m = jnp.maximum(jnp.max(s_past, axis=-1, keepdims=True), jnp.max(s_n, axis=-1, keepdims=True))
    pp = jnp.exp2(s_past - m)
    pn = jnp.exp2(s_n - m)
    l = jnp.sum(pp, axis=-1, keepdims=True) + jnp.sum(pn, axis=-1, keepdims=True)
    o_ref[0] = (_dot(pp.astype(BF16), kp) + _dot(pn.astype(BF16), kn)) / l


def attn_sample(qa, qr, ckv_past, kr_past, ckv_new, kr_new, s_new, n_heads):
    b, r, lora = qa.shape
    past = ckv_past.shape[1]
    return pl.pallas_call(
        functools.partial(_attn_sample_body, past=past, s_new=s_new, n_heads=n_heads),
        grid=(b,),
        in_specs=[pl.BlockSpec((1, r, lora), lambda i: (i, 0, 0)),
                  pl.BlockSpec((1, r, LANES), lambda i: (i, 0, 0)),
                  pl.BlockSpec((1, past, lora), lambda i: (i, 0, 0)),
                  pl.BlockSpec((1, ROPE_DIM, past), lambda i: (i, 0, 0)),
                  pl.BlockSpec((1, LANES, lora), lambda i: (i, 0, 0)),
                  pl.BlockSpec((1, LANES, LANES), lambda i: (i, 0, 0))],
        out_specs=pl.BlockSpec((1, r, lora), lambda i: (i, 0, 0)),
        out_shape=jax.ShapeDtypeStruct((b, r, lora), F32),
        compiler_params=_cparams(("parallel",)),
        name="attn_sample",
    )(qa, qr, ckv_past, kr_past, ckv_new, kr_new)


def _softplus(x):
    return jnp.maximum(x, 0.0) + jnp.log1p(jnp.exp(-jnp.abs(x)))


def _gdn_body(qkv_ref, z_ref, sm_ref, ga_ref, gb_ref, oa_ref, prev0_ref, s0_ref, convw_ref, alog_ref, dtb_ref,
              gn_ref, m_ref, sout_ref, s_scr, prev_scr, y_scr, *, L, n_heads):
    hd = HEAD_DIM
    heads = range(n_heads)
    c = pl.program_id(1)

    @pl.when(c == 0)
    def _():
        s_scr[...] = s0_ref[0]
        prev_scr[...] = prev0_ref[0]

    x = qkv_ref[...]
    y_scr[...] = _conv_silu(x, prev_scr[...], convw_ref[...])
    prev_scr[...] = x[L - SUBLANES:L]

    a = sm_ref[:, LANES:2 * LANES]
    bb = sm_ref[:, 2 * LANES:3 * LANES]
    g = -jnp.exp(alog_ref[...]) * _softplus(a + dtb_ref[...])
    beta = _sigmoid(bb)
    rowl = lax.broadcasted_iota(jnp.int32, g.shape, 0)
    cum = g
    d = 1
    while d < L:
        cum = cum + jnp.where(rowl >= d, pltpu.roll(cum, d, axis=0), 0.0)
        d *= 2
    cum_t = jnp.transpose(jnp.concatenate([cum, jnp.zeros((LANES - L, LANES), F32)], axis=0))

    row = lax.broadcasted_iota(jnp.int32, (L, L), 0)
    col = lax.broadcasted_iota(jnp.int32, (L, L), 1)
    tri = row >= col
    strict = row > col
    eye = jnp.where(row == col, 1.0, 0.0)
    gn = gn_ref[...]

    q_b, k_b, kb_b, rhs_b, qg_b, kd_b = [], [], [], [], [], []
    for h in heads:
        qh = y_scr[:, h * hd:(h + 1) * hd]
        kh = y_scr[:, (n_heads + h) * hd:(n_heads + h + 1) * hd]
        vh = y_scr[:, (2 * n_heads + h) * hd:(2 * n_heads + h + 1) * hd]
        qh = (qh * lax.rsqrt(jnp.sum(qh * qh, axis=-1, keepdims=True) + EPS)) * (hd ** -0.5)
        kh = kh * lax.rsqrt(jnp.sum(kh * kh, axis=-1, keepdims=True) + EPS)
        gcol = cum[:, h:h + 1]
        bcol = beta[:, h:h + 1]
        e_g = jnp.exp(gcol)
        kb = kh * bcol
        q_b.append(qh.astype(BF16))
        k_b.append(kh.astype(BF16))
        kb_b.append(kb.astype(BF16))
        rhs_b.append(jnp.concatenate([vh * bcol, kb * e_g], axis=1).astype(BF16))
        qg_b.append((qh * e_g).astype(BF16))
        kd_b.append((kh * jnp.exp(cum[L - 1:L, h:h + 1] - gcol)).astype(BF16))

    kk = [_dot_nt(kb_b[h], k_b[h]) for h in heads]
    qk = [_dot_nt(q_b[h], k_b[h]) for h in heads]

    nmat, attn_b = [], []
    for h in heads:
        diff = cum[:, h:h + 1] - cum_t[h:h + 1, 0:L]
        decay = jnp.where(tri, jnp.exp(jnp.where(tri, diff, 0.0)), 0.0)
        nmat.append(jnp.where(strict, kk[h] * decay, 0.0))
        attn_b.append(jnp.where(tri, qk[h] * decay, 0.0).astype(BF16))

    tmat = [eye - nmat[h] for h in heads]
    p_b = [nmat[h].astype(BF16) for h in heads]
    pmat = [_dot(p_b[h], p_b[h]) for h in heads]
    power = 2
    while True:
        p_b = [pmat[h].astype(BF16) for h in heads]
        tmat = [tmat[h] + _dot(tmat[h].astype(BF16), p_b[h]) for h in heads]
        power *= 2
        if power >= L:
            break
        pmat = [_dot(p_b[h], p_b[h]) for h in heads]

    uw = [_dot(tmat[h].astype(BF16), rhs_b[h]) for h in heads]
    s_old = [s_scr[h] for h in heads]
    s_b = [s_old[h].astype(BF16) for h in heads]
    ws = [_dot(jnp.concatenate([uw[h][:, hd:].astype(BF16), qg_b[h]], axis=0), s_b[h]) for h in heads]
    vn_b = [(uw[h][:, :hd] - ws[h][:L]).astype(BF16) for h in heads]
    av = [_dot(attn_b[h], vn_b[h]) for h in heads]
    ks = [_dot_tn(kd_b[h], vn_b[h]) for h in heads]
    for h in heads:
        sl = slice(h * hd, (h + 1) * hd)
        o = ws[h][L:] + av[h]
        s_scr[h] = s_old[h] * jnp.exp(cum[L - 1:L, h:h + 1]) + ks[h]
        zh = z_ref[:, sl]
        on = (o * lax.rsqrt(jnp.mean(o * o, axis=-1, keepdims=True) + EPS)) * gn
        ob = on * (zh * _sigmoid(zh))
        m_ref[:, sl] = (_sigmoid(ga_ref[:, sl]) * oa_ref[:, sl] + _sigmoid(gb_ref[:, sl]) * ob).astype(BF16)

    @pl.when(c == pl.num_programs(1) - 1)
    def _():
        sout_ref[0] = s_scr[...]


def gdn(u, o_a, prev0, s0, conv_w, a_log, dt_bias, gn, b, s, col_z, col_sm, col_ga, col_gb, n_heads):
    hd = HEAD_DIM
    L = min(s, CHUNK)
    nc = s // L
    qkv_w = 3 * n_heads * hd
    row = lambda bi, c: bi * nc + c
    alog = jnp.zeros((1, LANES), F32).at[0, :n_heads].set(a_log)
    dtb = jnp.zeros((1, LANES), F32).at[0, :n_heads].set(dt_bias)
    return pl.pallas_call(
        functools.partial(_gdn_body, L=L, n_heads=n_heads),
        grid=(b, nc),
        in_specs=[pl.BlockSpec((L, qkv_w), lambda bi, c: (row(bi, c), 0)),
                  pl.BlockSpec((L, n_heads * hd), lambda bi, c: (row(bi, c), col_z)),
                  pl.BlockSpec((L, 512), lambda bi, c: (row(bi, c), col_sm)),
                  pl.BlockSpec((L, n_heads * hd), lambda bi, c: (row(bi, c), col_ga)),
                  pl.BlockSpec((L, n_heads * hd), lambda bi, c: (row(bi, c), col_gb)),
                  pl.BlockSpec((L, n_heads * hd), lambda bi, c: (row(bi, c), 0)),
                  pl.BlockSpec((1, SUBLANES, qkv_w), lambda bi, c: (bi, 0, 0)),
                  pl.BlockSpec((1, n_heads, hd, hd), lambda bi, c: (bi, 0, 0, 0)),
                  pl.BlockSpec((CONV_W, qkv_w), lambda bi, c: (0, 0)),
                  pl.BlockSpec((1, LANES), lambda bi, c: (0, 0)),
                  pl.BlockSpec((1, LANES), lambda bi, c: (0, 0)),
                  pl.BlockSpec((1, hd), lambda bi, c: (0, 0))],
        out_specs=[pl.BlockSpec((L, n_heads * hd), lambda bi, c: (row(bi, c), 0)),
                   pl.BlockSpec((1, n_heads, hd, hd), lambda bi, c: (bi, 0, 0, 0))],
        out_shape=[jax.ShapeDtypeStruct((b * s, n_heads * hd), BF16),
                   jax.ShapeDtypeStruct((b, n_heads, hd, hd), F32)],
        scratch_shapes=[pltpu.VMEM((n_heads, hd, hd), F32),
                        pltpu.VMEM((SUBLANES, qkv_w), F32),
                        pltpu.VMEM((L, qkv_w), F32)],
        compiler_params=_cparams(("parallel", "arbitrary")),
        name="gdn",
    )(u, u, u, u, u, o_a, prev0, s0, conv_w, alog, dtb, gn.reshape(1, hd))


def _out_route_body(ma_ref, mb_ref, xa_ref, xb_ref, wo_ref, g_ref, whi_ref, wlo_ref, bias_ref,
                    x1_ref, h_ref, eid_ref, wgt_ref, cnt_ref, *, tiles_a):
    is_a = pl.program_id(0) < tiles_a
    m = jnp.where(is_a, ma_ref[...], mb_ref[...])
    x = jnp.where(is_a, xa_ref[...], xb_ref[...]) + _dot(m, wo_ref[...])
    x1_ref[...] = x
    hn = (x * lax.rsqrt(jnp.mean(x * x, axis=-1, keepdims=True) + EPS)) * g_ref[...]
    h_hi, h_lo = _split_bf16(hn)
    h_ref[...] = h_hi
    lg = jnp.transpose(_dot3(h_hi, h_lo, whi_ref[...], wlo_ref[...]) + bias_ref[...])
    rowi = lax.broadcasted_iota(jnp.int32, lg.shape, 0)
    big = jnp.int32(LANES)
    neg = -jnp.inf
    glog = jnp.where(rowi < N_GROUPS, lg, neg)
    gmax = jnp.max(glog, axis=0, keepdims=True)
    pg_top = 1.0 / jnp.sum(jnp.exp(glog - gmax), axis=0, keepdims=True)
    grp = jnp.min(jnp.where(glog == gmax, rowi, big), axis=0, keepdims=True)
    lo = N_GROUPS + grp * EXPERTS_PER_GROUP
    le = jnp.where((rowi >= lo) & (rowi < lo + EXPERTS_PER_GROUP), lg, neg)
    v1 = jnp.max(le, axis=0, keepdims=True)
    i1 = jnp.min(jnp.where(le == v1, rowi, big), axis=0, keepdims=True)
    le2 = jnp.where(rowi == i1, neg, le)
    v2 = jnp.max(le2, axis=0, keepdims=True)
    i2 = jnp.min(jnp.where(le2 == v2, rowi, big), axis=0, keepdims=True)
    e2 = jnp.exp(v2 - v1)
    den = 1.0 + e2
    w1 = pg_top * (1.0 / den)
    w2 = pg_top * (e2 / den)
    row8 = lax.broadcasted_iota(jnp.int32, eid_ref.shape, 0)
    eid_ref[...] = jnp.where(row8 == 0, i1 - N_GROUPS, jnp.where(row8 == 1, i2 - N_GROUPS, 0))
    wgt_ref[...] = jnp.where(row8 == 0, w1, jnp.where(row8 == 1, w2, 0.0))
    hit = jnp.where((rowi == i1 - N_GROUPS) | (rowi == i2 - N_GROUPS), 1.0, 0.0)
    per_expert = jnp.broadcast_to(jnp.sum(hit, axis=1, keepdims=True), (LANES, LANES))
    cnt_ref[0] = jnp.transpose(per_expert)[:SUBLANES].astype(jnp.int32)


def out_proj_route(ma, mb, xa, xb, w_out, g, whi, wlo, bias, tm):
    d = xa.shape[1]
    ta, tb = xa.shape[0] // tm, xb.shape[0] // tm
    n = (ta + tb) * tm
    rows_a = lambda i: (jnp.minimum(i, ta - 1), 0)
    rows_b = lambda i: (jnp.maximum(i - ta, 0), 0)
    fixed = lambda i: (0, 0)
    return pl.pallas_call(
        functools.partial(_out_route_body, tiles_a=ta),
        grid=(ta + tb,),
        in_specs=[pl.BlockSpec((tm, d), rows_a),
                  pl.BlockSpec((tm, d), rows_b),
                  pl.BlockSpec((tm, d), rows_a),
                  pl.BlockSpec((tm, d), rows_b),
                  pl.BlockSpec((d, d), fixed),
                  pl.BlockSpec((1, d), fixed),
                  pl.BlockSpec((d, LANES), fixed),
                  pl.BlockSpec((d, LANES), fixed),
                  pl.BlockSpec((1, LANES), fixed)],
        out_specs=[pl.BlockSpec((tm, d), lambda i: (i, 0)),
                   pl.BlockSpec((tm, d), lambda i: (i, 0)),
                   pl.BlockSpec((SUBLANES, tm), lambda i: (0, i)),
                   pl.BlockSpec((SUBLANES, tm), lambda i: (0, i)),
                   pl.BlockSpec((1, SUBLANES, LANES), lambda i: (i, 0, 0))],
        out_shape=[jax.ShapeDtypeStruct((n, d), F32),
                   jax.ShapeDtypeStruct((n, d), BF16),
                   jax.ShapeDtypeStruct((SUBLANES, n), jnp.int32),
                   jax.ShapeDtypeStruct((SUBLANES, n), F32),
                   jax.ShapeDtypeStruct((n // tm, SUBLANES, LANES), jnp.int32)],
        compiler_params=_cparams(("parallel",)),
        name="out_proj_route",
    )(ma, mb, xa, xb, w_out, g.reshape(1, d), whi, wlo, bias)


def _expert_body(be_tab, first_tab, slot_tab, next_tab, nused, x_ref, sw_ref, wg_hbm, wu_hbm, wd_hbm, o_ref,
                 stg_g, stg_u, stg_d, wg_s, wu_s, wd_s, sem):
    blk = pl.program_id(0)

    def weight_copies(e, slot):
        return (pltpu.make_async_copy(wg_hbm.at[e], stg_g.at[slot], sem.at[slot, 0]),
                pltpu.make_async_copy(wu_hbm.at[e], stg_u.at[slot], sem.at[slot, 1]),
                pltpu.make_async_copy(wd_hbm.at[e], stg_d.at[slot], sem.at[slot, 2]))

    @pl.when((blk == 0) & (nused[0] > 0))
    def _():
        for cp in weight_copies(be_tab[0], 0):
            cp.start()

    @pl.when(blk < nused[0])
    def _():
        @pl.when(first_tab[blk] == 1)
        def _():
            slot = slot_tab[blk]

            @pl.when(next_tab[blk] >= 0)
            def _():
                for cp in weight_copies(next_tab[blk], 1 - slot):
                    cp.start()

            for cp in weight_copies(be_tab[blk], slot):
                cp.wait()
            wg_s[...] = stg_g[slot].astype(BF16)
            wu_s[...] = stg_u[slot].astype(BF16)
            wd_s[...] = stg_d[slot].astype(BF16)

        x = x_ref[...]
        gate = _dot(x, wg_s[...])
        up = _dot(x, wu_s[...])
        hid = (gate * _sigmoid(gate)) * up
        o_ref[...] = (_dot(hid.astype(BF16), wd_s[...]) * sw_ref[...]).astype(o_ref.dtype)

    @pl.when(blk >= nused[0])
    def _():
        o_ref[...] = jnp.zeros(o_ref.shape, o_ref.dtype)


def experts(x_sorted, slot_w, block_e, first, slot_tab, next_tab, nused, w_gate, w_up, w_down, bm):
    n_slots, d = x_sorted.shape
    de = w_gate.shape[2]
    n_blocks = n_slots // bm
    row = lambda i, *_: (i, 0)
    grid_spec = pltpu.PrefetchScalarGridSpec(
        num_scalar_prefetch=5,
        grid=(n_blocks,),
        in_specs=[pl.BlockSpec((bm, d), row),
                  pl.BlockSpec((bm, 1), row),
                  pl.BlockSpec(memory_space=pl.ANY),
                  pl.BlockSpec(memory_space=pl.ANY),
                  pl.BlockSpec(memory_space=pl.ANY)],
        out_specs=pl.BlockSpec((bm, d), row),
        scratch_shapes=[pltpu.VMEM((2, d, de), F32), pltpu.VMEM((2, d, de), F32), pltpu.VMEM((2, de, d), F32),
                        pltpu.VMEM((d, de), BF16), pltpu.VMEM((d, de), BF16), pltpu.VMEM((de, d), BF16),
                        pltpu.SemaphoreType.DMA((2, 3))],
    )
    return pl.pallas_call(
        _expert_body,
        grid_spec=grid_spec,
        out_shape=jax.ShapeDtypeStruct((n_slots, d), BF16),
        compiler_params=_cparams(("arbitrary",)),
        name="experts",
    )(block_e, first, slot_tab, next_tab, nused, x_sorted, slot_w, w_gate, w_up, w_down)


def _final_body(x_ref, ma_ref, mb_ref, g_ref, o_ref):
    x = x_ref[...] + (ma_ref[...].astype(F32) + mb_ref[...].astype(F32))
    o_ref[...] = (x * lax.rsqrt(jnp.mean(x * x, axis=-1, keepdims=True) + EPS)) * g_ref[...]


def final_norm(x, ma, mb, g, tm, n, row0):
    d = x.shape[1]
    assert row0 % tm == 0 and n % tm == 0
    spec = pl.BlockSpec((tm, d), lambda i: (i, 0))
    mspec = pl.BlockSpec((tm, d), lambda i: (i + row0 // tm, 0))
    return pl.pallas_call(
        _final_body,
        grid=(n // tm,),
        in_specs=[mspec, mspec, mspec, pl.BlockSpec((1, d), lambda i: (0, 0))],
        out_specs=spec,
        out_shape=jax.ShapeDtypeStruct((n, d), F32),
        compiler_params=_cparams(("parallel",)),
        name="final_norm",
    )(x, ma, mb, g.reshape(1, d))


def _tile(n, pref):
    t = min(n, pref)
    while n % t:
        t //= 2
    return t


def _rope_table(pos, b):
    half = ROPE_DIM // 2
    inv = ROPE_BASE ** (-jnp.arange(half, dtype=F32) / half)
    ang = pos.astype(F32)[:, None] * inv[None, :]
    cos, sin = jnp.cos(ang), jnp.sin(ang)
    cs = jnp.concatenate([cos, cos, sin, sin], axis=-1)
    return jnp.tile(cs, (b, 1))


def _rot_cols(w):
    half = ROPE_DIM // 2
    return jnp.concatenate([-w[..., half:], w[..., :half]], axis=-1)


def _moe(h_bf, eid, wgt, counts, w_gate, w_up, w_down, bm):
    n, d = h_bf.shape
    n_exp = w_gate.shape[0]
    a = n * TOP_K
    e_flat = eid.reshape(-1)
    w_flat = wgt.reshape(-1)
    padded = (counts + bm - 1) // bm * bm
    pend = jnp.cumsum(padded)
    n_blocks = (a + n_exp * (bm - 1) + bm - 1) // bm
    n_slots = n_blocks * bm
    block_e = jnp.minimum(jnp.searchsorted(pend, jnp.arange(n_blocks, dtype=jnp.int32) * bm, side='right'),
                          n_exp - 1).astype(jnp.int32)
    first = jnp.concatenate([jnp.ones((1,), jnp.int32), (block_e[1:] != block_e[:-1]).astype(jnp.int32)])
    nused = (pend[-1] // bm).astype(jnp.int32).reshape(1)
    slot_tab = (jnp.cumsum(first) - 1) % 2
    e_ids = jnp.arange(n_exp, dtype=jnp.int32)
    later = lax.cummin(jnp.where(counts > 0, e_ids, n_exp), axis=0, reverse=True)
    next_e = jnp.concatenate([later[1:], jnp.full((1,), n_exp, jnp.int32)])
    next_tab = jnp.where(next_e < n_exp, next_e, -1)[block_e].astype(jnp.int32)
    n_pad = n_slots - a
    pad_end = jnp.cumsum(padded - counts)
    pad_key = jnp.sum(jnp.arange(n_pad, dtype=jnp.int32)[None, :] >= pad_end[:, None], axis=0, dtype=jnp.int32)
    keys = jnp.concatenate([e_flat, pad_key])
    toks = jnp.concatenate([jnp.arange(a, dtype=jnp.int32) // TOP_K, jnp.arange(n_pad, dtype=jnp.int32) % n])
    wts = jnp.concatenate([w_flat, jnp.zeros((n_pad,), F32)])
    iota = jnp.arange(n_slots, dtype=jnp.int32)
    _, slot_tok, slot_w, slot_src = lax.sort((keys, toks, wts, iota), num_keys=1, is_stable=True)
    x_sorted = h_bf[slot_tok]
    out = experts(x_sorted, slot_w.reshape(n_slots, 1), block_e, first, slot_tab.astype(jnp.int32), next_tab, nused,
                  w_gate, w_up, w_down, bm)
    _, slot_of = lax.sort((slot_src, iota), num_keys=1)
    slot_of = slot_of[:a].reshape(n, TOP_K)
    return out[slot_of[:, 0]], out[slot_of[:, 1]]


def _mixer(x3, pos, ckv_past, kr_past, s0, conv0, wp):
    b, s, d = x3.shape
    n = b * s
    n_heads = wp['n_heads']
    hd = HEAD_DIM
    x = x3.reshape(n, d)
    cols = wp['cols']

    u = rms_matmul(x, wp['attn_norm_g'], wp['w_all'], _tile(n, IN_PROJ_TM), IN_PROJ_TN)

    cs = _rope_table(pos, b)
    scale = (hd + ROPE_DIM) ** -0.5 * float(np.log2(np.e))
    prompt = ckv_past is None
    t_att = _tile(s, ATTN_TILE)
    tm = min(_tile(s, 256), t_att)
    res = mla_prep(u, cs, wp['q_norm_g'], wp['kv_norm_g'], wp['wq_ext'], wp['wk'] if prompt else None,
                   wp['wvt'] if prompt else None, b, s, tm, t_att, cols['q_lat'] // 512, cols['kv_lat'] // 512,
                   cols['small'] // 512, n_heads, scale)
    ckv, kr_pad, q = res[:3]
    kr = kr_pad[:, :ROPE_DIM]
    lora = ckv.shape[1]
    if prompt:
        o_a = flash_prompt(q, res[3], res[4], t_att, ATTN_HEADS_PER_STEP).reshape(n, n_heads * hd)
    else:
        qn = q[..., :hd].transpose(1, 0, 2, 3).reshape(n_heads, n, hd)
        q_abs = head_matmul(qn, wp['w_uk_t'])
        q_abs = q_abs.reshape(n_heads, b, s, lora).transpose(1, 2, 0, 3).reshape(b, s * n_heads, lora)
        qr = q[..., hd:].transpose(0, 2, 1, 3).reshape(b, s * n_heads, hd)
        kr_past_t = jnp.swapaxes(kr_past, 1, 2)
        ckv_new = jnp.pad(ckv.reshape(b, s, lora), ((0, 0), (0, LANES - s), (0, 0)))
        kr_new = jnp.pad(kr_pad.reshape(b, s, LANES), ((0, 0), (0, LANES - s), (0, 0)))
        o_lat = attn_sample(q_abs, qr, ckv_past, kr_past_t, ckv_new, kr_new, s, n_heads)
        o_lat = o_lat.reshape(b, s, n_heads, lora).transpose(2, 0, 1, 3).reshape(n_heads, n, lora)
        o_h = head_matmul(o_lat, wp['w_uv_h'])
        o_a = o_h.reshape(n_heads, n, hd).transpose(1, 0, 2).reshape(n, n_heads * hd)

    qkv_w = 3 * n_heads * hd
    prev0 = jnp.pad(conv0, ((0, 0), (SUBLANES - (CONV_W - 1), 0), (0, 0)))
    merged, s_new = gdn(u, o_a, prev0, s0, wp['conv_w'], wp['a_log'], wp['dt_bias'], wp['gdn_norm_g'], b, s,
                        cols['z'] // d, cols['small'] // 512, cols['gate_a'] // d, cols['gate_b'] // d, n_heads)
    tail = u.reshape(b, s, -1)[:, max(s - (CONV_W - 1), 0):, :qkv_w]
    conv_new = jnp.concatenate([conv0, tail], axis=1)[:, -(CONV_W - 1):]

    return (merged, x), ckv.reshape(b, s, lora), kr.reshape(b, s, ROPE_DIM), s_new, conv_new


def _prep_weights(l, attn_norm_g, w_in, q_norm_g, w_uq, kv_norm_g, w_uk, w_uv, conv_w, a_log, dt_bias, gdn_norm_g,
                  w_out, ffn_norm_g, w_group, b_group, w_router, b_router, w_gate, w_up, w_down):
    d = w_in.shape[1]
    q_lora, n_heads, _ = w_uq.shape[1:]
    kv_lora = w_uk.shape[1]
    hd = HEAD_DIM
    qkv_w = 3 * n_heads * hd
    sizes = (q_lora, kv_lora, ROPE_DIM, qkv_w, n_heads, n_heads, n_heads * hd, d, d)
    offs = np.concatenate([[0], np.cumsum(sizes)])
    wi_t = jnp.swapaxes(w_in[l], 0, 1)
    part = lambda i: wi_t[offs[i]:offs[i + 1]]
    w_qlat, w_kvlat, w_kr, w_qkv, w_a, w_b, w_z, w_ga, w_gb = [part(i) for i in range(9)]
    zpad = lambda c: jnp.zeros((c, d), F32)
    half = ROPE_DIM // 2
    w_kr_rot = jnp.concatenate([-w_kr[half:], w_kr[:half]], axis=0)
    small = jnp.concatenate([w_kr, w_kr_rot, w_a, zpad(LANES - n_heads), w_b, zpad(LANES - n_heads),
                             zpad(LANES)], axis=0)
    w_all = jnp.concatenate([w_qkv, w_z, w_ga, w_gb, w_qlat, w_kvlat, small], axis=0).astype(BF16)
    cols = {'qkv': 0, 'z': qkv_w, 'gate_a': qkv_w + n_heads * hd, 'gate_b': qkv_w + n_heads * hd + d,
            'q_lat': qkv_w + n_heads * hd + 2 * d}
    cols['kv_lat'] = cols['q_lat'] + q_lora
    cols['small'] = cols['kv_lat'] + kv_lora
    wq = w_uq[l]
    wq_nope = wq[:, :, :hd].reshape(q_lora, n_heads * hd)
    wq_r = wq[:, :, hd:]
    wq_rope = jnp.concatenate([wq_r, _rot_cols(wq_r)], axis=-1).reshape(q_lora, n_heads * hd)
    wq_ext = jnp.concatenate([wq_nope, wq_rope], axis=1).astype(BF16)
    wk = w_uk[l].reshape(kv_lora, n_heads * hd).astype(BF16)
    wvt = w_uv[l].reshape(kv_lora, n_heads * hd).T.astype(BF16)
    wr = jnp.concatenate([w_group[l], w_router[l].transpose(1, 0, 2).reshape(d, -1)], axis=1)
    wr = jnp.pad(wr, ((0, 0), (0, LANES - wr.shape[1])))
    wr_hi = wr.astype(BF16)
    wr_lo = (wr - wr_hi.astype(F32)).astype(BF16)
    br = jnp.concatenate([b_group[l], b_router[l].reshape(-1)])
    br = jnp.pad(br, (0, LANES - br.shape[0])).reshape(1, LANES)
    return {
        'n_heads': n_heads, 'cols': cols, 'w_all': w_all, 'attn_norm_g': attn_norm_g[l],
        'q_norm_g': q_norm_g[l], 'kv_norm_g': kv_norm_g[l], 'wq_ext': wq_ext, 'wk': wk, 'wvt': wvt,
        'w_uk_t': w_uk[l].transpose(1, 2, 0).astype(BF16),
        'w_uv_h': w_uv[l].transpose(1, 0, 2).astype(BF16),
        'conv_w': conv_w[l], 'a_log': a_log[l], 'dt_bias': dt_bias[l], 'gdn_norm_g': gdn_norm_g[l],
        'w_out': w_out[l].astype(BF16), 'ffn_norm_g': ffn_norm_g[l], 'wr_hi': wr_hi, 'wr_lo': wr_lo, 'br': br,
        'w_gate': w_gate[l], 'w_up': w_up[l], 'w_down': w_down[l],
    }


def kernel(x_prompt, x_sample, cache_ckv, cache_k_rope, state_gdn, state_conv, attn_norm_g, w_in, q_norm_g, w_uq, kv_norm_g, w_uk, w_uv, conv_w, a_log, dt_bias, gdn_norm_g, w_out, ffn_norm_g, w_group, b_group, w_router, b_router, w_gate, w_up, w_down, final_norm_g):
    depth = w_in.shape[0]
    assert depth == 1, "final RMSNorm is fused into the last layer; deeper trunks are not supported"
    b_p, s_p, d = x_prompt.shape
    b_s, s_s, _ = x_sample.shape
    past = cache_ckv.shape[2]
    n_heads = w_uq.shape[2]
    qkv_w = 3 * n_heads * HEAD_DIM
    pos_p = jnp.arange(s_p, dtype=jnp.int32)
    pos_s = past + jnp.arange(s_s, dtype=jnp.int32)
    wp = _prep_weights(0, attn_norm_g, w_in, q_norm_g, w_uq, kv_norm_g, w_uk, w_uv, conv_w, a_log, dt_bias,
                       gdn_norm_g, w_out, ffn_norm_g, w_group, b_group, w_router, b_router, w_gate, w_up, w_down)
    s0 = jnp.zeros((b_p, n_heads, HEAD_DIM, HEAD_DIM), F32)
    c0 = jnp.zeros((b_p, CONV_W - 1, qkv_w), F32)
    (m_p, xr_p), ckv_p, kr_p, sg_p, sc_p = _mixer(x_prompt, pos_p, None, None, s0, c0, wp)
    (m_s, xr_s), ckv_s, kr_s, sg_s, sc_s = _mixer(x_sample, pos_s, cache_ckv[0], cache_k_rope[0], state_gdn[0],
                                                  state_conv[0], wp)

    n_p, n_s = b_p * s_p, b_s * s_s
    tm_s = _tile(int(np.gcd(n_p, n_s)), 512)
    x1, h_bf, eid, wgt, cnt = out_proj_route(m_p, m_s, xr_p, xr_s, wp['w_out'], wp['ffn_norm_g'], wp['wr_hi'],
                                             wp['wr_lo'], wp['br'], tm_s)
    n_exp = wp['w_gate'].shape[0]
    counts = cnt[:, 0, :n_exp].sum(axis=0)
    ma, mb = _moe(h_bf, eid[:TOP_K].T, wgt[:TOP_K].T, counts, wp['w_gate'], wp['w_up'], wp['w_down'], MOE_BLOCK)
    yp = final_norm(x1, ma, mb, final_norm_g, _tile(n_p, 512), n_p, 0).reshape(b_p, s_p, d)
    ys = final_norm(x1, ma, mb, final_norm_g, tm_s, n_s, n_p).reshape(b_s, s_s, d)
    return (yp, ys, ckv_p[None], kr_p[None], sg_p[None], sc_p[None], ckv_s[None], kr_s[None], sg_s[None], sc_s[None])
```

```python
import functools

import numpy as np
import jax
import jax.numpy as jnp
from jax import lax
from jax.experimental import pallas as pl
from jax.experimental.pallas import tpu as pltpu

F32 = jnp.float32
BF16 = jnp.bfloat16

EPS = 1e-6
CHUNK = 64
ROPE_BASE = 10000.0
HEAD_DIM = 128
ROPE_DIM = 64
CONV_W = 4
TOP_K = 2
N_GROUPS = 8
EXPERTS_PER_GROUP = 8

LANES = 128
SUBLANES = 8
BF16_ROWS = 16
VMEM_LIMIT = 56 * 1024 * 1024
ATTN_TILE = 512
ATTN_HEADS_PER_STEP = 4
MOE_BLOCK = 128
IN_PROJ_TN = 1536
IN_PROJ_TM = 1024


def _cparams(sem):
    return pltpu.CompilerParams(dimension_semantics=sem, vmem_limit_bytes=VMEM_LIMIT)


def _dot(a, b):
    return jnp.dot(a, b, preferred_element_type=F32)


def _dot_nt(a, b):
    return lax.dot_general(a, b, (((1,), (1,)), ((), ())), preferred_element_type=F32)


def _dot_tn(a, b):
    return lax.dot_general(a, b, (((0,), (0,)), ((), ())), preferred_element_type=F32)


def _sigmoid(x):
    return 1.0 / (1.0 + jnp.exp(-x))


def _split_bf16(a):
    hi = a.astype(BF16)
    lo = (a - hi.astype(F32)).astype(BF16)
    return hi, lo


def _dot3(a_hi, a_lo, b_hi, b_lo):
    return _dot(a_hi, b_hi) + (_dot(a_hi, b_lo) + _dot(a_lo, b_hi))


def _rms_mm_body(x_ref, g_ref, w_ref, o_ref, h_ref):
    @pl.when(pl.program_id(1) == 0)
    def _():
        x = x_ref[...]
        ms = jnp.mean(x * x, axis=-1, keepdims=True)
        h_ref[...] = ((x * lax.rsqrt(ms + EPS)) * g_ref[...]).astype(BF16)

    o_ref[...] = _dot_nt(h_ref[...], w_ref[...])


def rms_matmul(x, g, w_t, tm, tn):
    n, k = x.shape
    m = w_t.shape[0]
    return pl.pallas_call(
        _rms_mm_body,
        grid=(n // tm, m // tn),
        in_specs=[pl.BlockSpec((tm, k), lambda i, j: (i, 0)),
                  pl.BlockSpec((1, k), lambda i, j: (0, 0)),
                  pl.BlockSpec((tn, k), lambda i, j: (j, 0))],
        out_specs=pl.BlockSpec((tm, tn), lambda i, j: (i, j)),
        out_shape=jax.ShapeDtypeStruct((n, m), F32),
        scratch_shapes=[pltpu.VMEM((tm, k), BF16)],
        compiler_params=_cparams(("parallel", "arbitrary")),
        name="rms_matmul",
    )(x, g.reshape(1, k), w_t)


def _conv_silu(x, prev8, w):
    row8 = lax.broadcasted_iota(jnp.int32, prev8.shape, 0)

    def shifted(k):
        xk = pltpu.roll(x, k, axis=0)
        top = jnp.where(row8 < k, pltpu.roll(prev8, k, axis=0), xk[0:SUBLANES])
        return jnp.concatenate([top, xk[SUBLANES:]], axis=0)

    y = shifted(CONV_W - 1) * w[0:1, :]
    for j in range(1, CONV_W - 1):
        y = y + shifted(CONV_W - 1 - j) * w[j:j + 1, :]
    y = y + x * w[CONV_W - 1:CONV_W, :]
    return y * _sigmoid(y)


def _mla_prep_body(ql_ref, kvl_ref, sm_ref, cs_ref, gq_ref, gkv_ref, wq_ref, *rest, n_heads, scale, with_kv):
    if with_kv:
        wk_ref, wvt_ref, ckv_ref, kr_ref, q_ref, k_ref, vt_ref = rest
    else:
        ckv_ref, kr_ref, q_ref = rest
    hd = HEAD_DIM
    ql = ql_ref[...]
    qn = ((ql * lax.rsqrt(jnp.mean(ql * ql, axis=-1, keepdims=True) + EPS)) * gq_ref[...]).astype(BF16)
    q_all = _dot(qn, wq_ref[...])
    kvl = kvl_ref[...]
    ckv = (kvl * lax.rsqrt(jnp.mean(kvl * kvl, axis=-1, keepdims=True) + EPS)) * gkv_ref[...]
    ckv_ref[...] = ckv
    if with_kv:
        ckv_b = ckv.astype(BF16)
        k_all = _dot(ckv_b, wk_ref[...])
        vt_all = _dot_nt(wvt_ref[...], ckv_b)

    cs = cs_ref[...]
    lane = lax.broadcasted_iota(jnp.int32, cs.shape, 1)

    def rope(t):
        t = t * cs
        return jnp.where(lane < ROPE_DIM, t + pltpu.roll(t, ROPE_DIM, axis=1), 0.0)

    kr = rope(sm_ref[:, 0:LANES])
    kr_ref[...] = kr
    kr_b = kr.astype(BF16)
    off = n_heads * hd
    for h in range(n_heads):
        q_ref[0, h, :, 0:hd] = (q_all[:, h * hd:(h + 1) * hd] * scale).astype(BF16)
        q_ref[0, h, :, hd:2 * hd] = (rope(q_all[:, off + h * hd:off + (h + 1) * hd]) * scale).astype(BF16)
        if with_kv:
            k_ref[0, h, :, 0:hd] = k_all[:, h * hd:(h + 1) * hd].astype(BF16)
            k_ref[0, h, :, hd:2 * hd] = kr_b
            vt_ref[0, h, 0, 0:hd, :] = vt_all[h * hd:(h + 1) * hd, :].astype(BF16)
            vt_ref[0, h, 0, hd:hd + BF16_ROWS, :] = jnp.ones((BF16_ROWS, vt_all.shape[1]), BF16)


def mla_prep(u, cs, gq, gkv, wq, wk, wvt, b, s, tm, t_att, col_ql, col_kvl, col_sm, n_heads, scale):
    lora = wq.shape[0]
    nt = s // tm
    row = lambda bi, si: bi * nt + si
    hd = HEAD_DIM
    with_kv = wk is not None
    full = lambda bi, si: (0, 0)
    in_specs = [pl.BlockSpec((tm, lora), lambda bi, si: (row(bi, si), col_ql)),
                pl.BlockSpec((tm, lora), lambda bi, si: (row(bi, si), col_kvl)),
                pl.BlockSpec((tm, 512), lambda bi, si: (row(bi, si), col_sm)),
                pl.BlockSpec((tm, LANES), lambda bi, si: (row(bi, si), 0)),
                pl.BlockSpec((1, lora), full),
                pl.BlockSpec((1, lora), full),
                pl.BlockSpec(wq.shape, full)]
    out_specs = [pl.BlockSpec((tm, lora), lambda bi, si: (row(bi, si), 0)),
                 pl.BlockSpec((tm, LANES), lambda bi, si: (row(bi, si), 0)),
                 pl.BlockSpec((1, n_heads, tm, 2 * hd), lambda bi, si: (bi, 0, si, 0))]
    out_shape = [jax.ShapeDtypeStruct((b * s, lora), F32),
                 jax.ShapeDtypeStruct((b * s, LANES), F32),
                 jax.ShapeDtypeStruct((b, n_heads, s, 2 * hd), BF16)]
    args = [u, u, u, cs, gq.reshape(1, lora), gkv.reshape(1, lora), wq]
    if with_kv:
        r = t_att // tm
        in_specs += [pl.BlockSpec(wk.shape, full), pl.BlockSpec(wvt.shape, full)]
        out_specs += [pl.BlockSpec((1, n_heads, tm, 2 * hd), lambda bi, si: (bi, 0, si, 0)),
                      pl.BlockSpec((1, n_heads, 1, hd + BF16_ROWS, tm),
                                   lambda bi, si: (bi, 0, si // r, 0, si % r))]
        out_shape += [jax.ShapeDtypeStruct((b, n_heads, s, 2 * hd), BF16),
                      jax.ShapeDtypeStruct((b, n_heads, s // t_att, hd + BF16_ROWS, t_att), BF16)]
        args += [wk, wvt]
    return pl.pallas_call(
        functools.partial(_mla_prep_body, n_heads=n_heads, scale=scale, with_kv=with_kv),
        grid=(b, nt),
        in_specs=in_specs,
        out_specs=out_specs,
        out_shape=out_shape,
        compiler_params=_cparams(("parallel", "parallel")),
        name="mla_prep",
    )(*args)


def _flash_body(q_ref, k_ref, vt_ref, o_ref, m_scr, acc_scr, *, t, gh):
    qi = pl.program_id(2)
    hd = HEAD_DIM
    m_scr[...] = jnp.full(m_scr.shape, -jnp.inf, F32)
    acc_scr[...] = jnp.zeros(acc_scr.shape, F32)

    def block(j, masked):
        start = pl.multiple_of(j * t, t)
        scs = [_dot_nt(k_ref[0, g, pl.ds(start, t), :], q_ref[0, g]) for g in range(gh)]
        for g in range(gh):
            sc = scs[g]
            if masked:
                krow = lax.broadcasted_iota(jnp.int32, sc.shape, 0)
                qcol = lax.broadcasted_iota(jnp.int32, sc.shape, 1)
                sc = jnp.where(krow // CHUNK <= qcol // CHUNK, sc, -jnp.inf)
            m_prev = m_scr[g]
            m_new = jnp.maximum(m_prev, jnp.max(sc, axis=0, keepdims=True))
            alpha = jnp.exp2(m_prev - m_new)
            p = jnp.exp2(sc - m_new)
            acc_scr[g] = alpha * acc_scr[g] + _dot(vt_ref[0, g, j], p.astype(BF16))
            m_scr[g] = m_new

    def full_block(j, carry):
        block(j, False)
        return carry

    lax.fori_loop(0, qi, full_block, 0)
    block(qi, True)
    for g in range(gh):
        acc = acc_scr[g]
        o_ref[0, :, g * hd:(g + 1) * hd] = jnp.transpose(acc[:hd] / acc[hd:hd + 1])


def flash_prompt(q, k, vt, t, gh):
    b, nh, s, dk = q.shape
    hd = HEAD_DIM
    hv = vt.shape[-2]
    assert t % CHUNK == 0 and s % t == 0 and nh % gh == 0 and vt.shape[-1] == t and hv > hd
    return pl.pallas_call(
        functools.partial(_flash_body, t=t, gh=gh),
        grid=(b, nh // gh, s // t),
        in_specs=[pl.BlockSpec((1, gh, t, dk), lambda bi, hp, qi: (bi, hp, qi, 0)),
                  pl.BlockSpec((1, gh, s, dk), lambda bi, hp, qi: (bi, hp, 0, 0)),
                  pl.BlockSpec((1, gh, s // t, hv, t), lambda bi, hp, qi: (bi, hp, 0, 0, 0))],
        out_specs=pl.BlockSpec((1, t, gh * hd), lambda bi, hp, qi: (bi, qi, hp)),
        out_shape=jax.ShapeDtypeStruct((b, s, nh * hd), F32),
        scratch_shapes=[pltpu.VMEM((gh, 1, t), F32), pltpu.VMEM((gh, hv, t), F32)],
        compiler_params=_cparams(("parallel", "parallel", "arbitrary")),
        name="flash_prompt",
    )(q, k, vt)


def _bmm_body(a_ref, b_ref, o_ref):
    o_ref[0] = _dot(a_ref[0].astype(BF16), b_ref[0])


def head_matmul(a, b):
    nh, m, k = a.shape
    n = b.shape[2]
    return pl.pallas_call(
        _bmm_body,
        grid=(nh,),
        in_specs=[pl.BlockSpec((1, m, k), lambda h: (h, 0, 0)),
                  pl.BlockSpec((1, k, n), lambda h: (h, 0, 0))],
        out_specs=pl.BlockSpec((1, m, n), lambda h: (h, 0, 0)),
        out_shape=jax.ShapeDtypeStruct((nh, m, n), F32),
        compiler_params=_cparams(("parallel",)),
        name="head_matmul",
    )(a, b)


def _attn_sample_body(qa_ref, qr_ref, kp_ref, krp_ref, kn_ref, krn_ref, o_ref, *, past, s_new, n_heads):
    qa = qa_ref[0].astype(BF16)
    qr = qr_ref[0].astype(BF16)
    kp = kp_ref[0].astype(BF16)
    krp_t = krp_ref[0].astype(BF16)
    kn = kn_ref[0].astype(BF16)
    krn = krn_ref[0].astype(BF16)
    s_past = _dot_nt(qa, kp) + _dot(qr[:, :ROPE_DIM], krp_t)
    s_n = _dot_nt(qa, kn) + _dot_nt(qr, krn)
    row = lax.broadcasted_iota(jnp.int32, s_n.shape, 0)
    col = lax.broadcasted_iota(jnp.int32, s_n.shape, 1)
    qpos = past + row // n_heads
    kpos = past + col
    valid = (col < s_new) & (kpos // CHUNK <= qpos // CHUNK)
    s_n = jnp.where(valid, s_n, -jnp.inf)
    m = jnp.maximum(jnp.max(s_past, axis=-1, keepdims=True), jnp.max(s_n, axis=-1, keepdims=True))
    pp = jnp.exp2(s_past - m)
    pn = jnp.exp2(s_n - m)
    l = jnp.sum(pp, axis=-1, keepdims=True) + jnp.sum(pn, axis=-1, keepdims=True)
    o_ref[0] = (_dot(pp.astype(BF16), kp) + _dot(pn.astype(BF16), kn)) / l


def attn_sample(qa, qr, ckv_past, kr_past, ckv_new, kr_new, s_new, n_heads):
    b, r, lora = qa.shape
    past = ckv_past.shape[1]
    return pl.pallas_call(
        functools.partial(_attn_sample_body, past=past, s_new=s_new, n_heads=n_heads),
        grid=(b,),
        in_specs=[pl.BlockSpec((1, r, lora), lambda i: (i, 0, 0)),
                  pl.BlockSpec((1, r, LANES), lambda i: (i, 0, 0)),
                  pl.BlockSpec((1, past, lora), lambda i: (i, 0, 0)),
                  pl.BlockSpec((1, ROPE_DIM, past), lambda i: (i, 0, 0)),
                  pl.BlockSpec((1, LANES, lora), lambda i: (i, 0, 0)),
                  pl.BlockSpec((1, LANES, LANES), lambda i: (i, 0, 0))],
        out_specs=pl.BlockSpec((1, r, lora), lambda i: (i, 0, 0)),
        out_shape=jax.ShapeDtypeStruct((b, r, lora), F32),
        compiler_params=_cparams(("parallel",)),
        name="attn_sample",
    )(qa, qr, ckv_past, kr_past, ckv_new, kr_new)


def _softplus(x):
    return jnp.maximum(x, 0.0) + jnp.log1p(jnp.exp(-jnp.abs(x)))


def _gdn_body(qkv_ref, z_ref, sm_ref, ga_ref, gb_ref, oa_ref, prev0_ref, s0_ref, convw_ref, alog_ref, dtb_ref,
              gn_ref, m_ref, sout_ref, s_scr, prev_scr, y_scr, *, L, n_heads):
    hd = HEAD_DIM
    heads = range(n_heads)
    c = pl.program_id(1)

    @pl.when(c == 0)
    def _():
        s_scr[...] = s0_ref[0]
        prev_scr[...] = prev0_ref[0]

    x = qkv_ref[...]
    y_scr[...] = _conv_silu(x, prev_scr[...], convw_ref[...])
    prev_scr[...] = x[L - SUBLANES:L]

    a = sm_ref[:, LANES:2 * LANES]
    bb = sm_ref[:, 2 * LANES:3 * LANES]
    g = -jnp.exp(alog_ref[...]) * _softplus(a + dtb_ref[...])
    beta = _sigmoid(bb)
    rowl = lax.broadcasted_iota(jnp.int32, g.shape, 0)
    cum = g
    d = 1
    while d < L:
        cum = cum + jnp.where(rowl >= d, pltpu.roll(cum, d, axis=0), 0.0)
        d *= 2
    cum_t = jnp.transpose(jnp.concatenate([cum, jnp.zeros((LANES - L, LANES), F32)], axis=0))

    row = lax.broadcasted_iota(jnp.int32, (L, L), 0)
    col = lax.broadcasted_iota(jnp.int32, (L, L), 1)
    tri = row >= col
    strict = row > col
    eye = jnp.where(row == col, 1.0, 0.0)
    gn = gn_ref[...]

    q_b, k_b, kb_b, rhs_b, qg_b, kd_b = [], [], [], [], [], []
    for h in heads:
        qh = y_scr[:, h * hd:(h + 1) * hd]
        kh = y_scr[:, (n_heads + h) * hd:(n_heads + h + 1) * hd]
        vh = y_scr[:, (2 * n_heads + h) * hd:(2 * n_heads + h + 1) * hd]
        qh = (qh * lax.rsqrt(jnp.sum(qh * qh, axis=-1, keepdims=True) + EPS)) * (hd ** -0.5)
        kh = kh * lax.rsqrt(jnp.sum(kh * kh, axis=-1, keepdims=True) + EPS)
        gcol = cum[:, h:h + 1]
        bcol = beta[:, h:h + 1]
        e_g = jnp.exp(gcol)
        kb = kh * bcol
        q_b.append(qh.astype(BF16))
        k_b.append(kh.astype(BF16))
        kb_b.append(kb.astype(BF16))
        rhs_b.append(jnp.concatenate([vh * bcol, kb * e_g], axis=1).astype(BF16))
        qg_b.append((qh * e_g).astype(BF16))
        kd_b.append((kh * jnp.exp(cum[L - 1:L, h:h + 1] - gcol)).astype(BF16))

    kk = [_dot_nt(kb_b[h], k_b[h]) for h in heads]
    qk = [_dot_nt(q_b[h], k_b[h]) for h in heads]

    nmat, attn_b = [], []
    for h in heads:
        diff = cum[:, h:h + 1] - cum_t[h:h + 1, 0:L]
        decay = jnp.where(tri, jnp.exp(jnp.where(tri, diff, 0.0)), 0.0)
        nmat.append(jnp.where(strict, kk[h] * decay, 0.0))
        attn_b.append(jnp.where(tri, qk[h] * decay, 0.0).astype(BF16))

    tmat = [eye - nmat[h] for h in heads]
    p_b = [nmat[h].astype(BF16) for h in heads]
    pmat = [_dot(p_b[h], p_b[h]) for h in heads]
    power = 2
    while True:
        p_b = [pmat[h].astype(BF16) for h in heads]
        tmat = [tmat[h] + _dot(tmat[h].astype(BF16), p_b[h]) for h in heads]
        power *= 2
        if power >= L:
            break
        pmat = [_dot(p_b[h], p_b[h]) for h in heads]

    uw = [_dot(tmat[h].astype(BF16), rhs_b[h]) for h in heads]
    s_old = [s_scr[h] for h in heads]
    s_b = [s_old[h].astype(BF16) for h in heads]
    ws = [_dot(jnp.concatenate([uw[h][:, hd:].astype(BF16), qg_b[h]], axis=0), s_b[h]) for h in heads]
    vn_b = [(uw[h][:, :hd] - ws[h][:L]).astype(BF16) for h in heads]
    av = [_dot(attn_b[h], vn_b[h]) for h in heads]
    ks = [_dot_tn(kd_b[h], vn_b[h]) for h in heads]
    for h in heads:
        sl = slice(h * hd, (h + 1) * hd)
        o = ws[h][L:] + av[h]
        s_scr[h] = s_old[h] * jnp.exp(cum[L - 1:L, h:h + 1]) + ks[h]
        zh = z_ref[:, sl]
        on = (o * lax.rsqrt(jnp.mean(o * o, axis=-1, keepdims=True) + EPS)) * gn
        ob = on * (zh * _sigmoid(zh))
        m_ref[:, sl] = (_sigmoid(ga_ref[:, sl]) * oa_ref[:, sl] + _sigmoid(gb_ref[:, sl]) * ob).astype(BF16)

    @pl.when(c == pl.num_programs(1) - 1)
    def _():
        sout_ref[0] = s_scr[...]


def gdn(u, o_a, prev0, s0, conv_w, a_log, dt_bias, gn, b, s, col_z, col_sm, col_ga, col_gb, n_heads):
    hd = HEAD_DIM
    L = min(s, CHUNK)
    nc = s // L
    qkv_w = 3 * n_heads * hd
    row = lambda bi, c: bi * nc + c
    alog = jnp.zeros((1, LANES), F32).at[0, :n_heads].set(a_log)
    dtb = jnp.zeros((1, LANES), F32).at[0, :n_heads].set(dt_bias)
    return pl.pallas_call(
        functools.partial(_gdn_body, L=L, n_heads=n_heads),
        grid=(b, nc),
        in_specs=[pl.BlockSpec((L, qkv_w), lambda bi, c: (row(bi, c), 0)),
                  pl.BlockSpec((L, n_heads * hd), lambda bi, c: (row(bi, c), col_z)),
                  pl.BlockSpec((L, 512), lambda bi, c: (row(bi, c), col_sm)),
                  pl.BlockSpec((L, n_heads * hd), lambda bi, c: (row(bi, c), col_ga)),
                  pl.BlockSpec((L, n_heads * hd), lambda bi, c: (row(bi, c), col_gb)),
                  pl.BlockSpec((L, n_heads * hd), lambda bi, c: (row(bi, c), 0)),
                  pl.BlockSpec((1, SUBLANES, qkv_w), lambda bi, c: (bi, 0, 0)),
                  pl.BlockSpec((1, n_heads, hd, hd), lambda bi, c: (bi, 0, 0, 0)),
                  pl.BlockSpec((CONV_W, qkv_w), lambda bi, c: (0, 0)),
                  pl.BlockSpec((1, LANES), lambda bi, c: (0, 0)),
                  pl.BlockSpec((1, LANES), lambda bi, c: (0, 0)),
                  pl.BlockSpec((1, hd), lambda bi, c: (0, 0))],
        out_specs=[pl.BlockSpec((L, n_heads * hd), lambda bi, c: (row(bi, c), 0)),
                   pl.BlockSpec((1, n_heads, hd, hd), lambda bi, c: (bi, 0, 0, 0))],
        out_shape=[jax.ShapeDtypeStruct((b * s, n_heads * hd), BF16),
                   jax.ShapeDtypeStruct((b, n_heads, hd, hd), F32)],
        scratch_shapes=[pltpu.VMEM((n_heads, hd, hd), F32),
                        pltpu.VMEM((SUBLANES, qkv_w), F32),
                        pltpu.VMEM((L, qkv_w), F32)],
        compiler_params=_cparams(("parallel", "arbitrary")),
        name="gdn",
    )(u, u, u, u, u, o_a, prev0, s0, conv_w, alog, dtb, gn.reshape(1, hd))


def _out_route_body(ma_ref, mb_ref, xa_ref, xb_ref, wo_ref, g_ref, whi_ref, wlo_ref, bias_ref,
                    x1_ref, h_ref, eid_ref, wgt_ref, cnt_ref, *, tiles_a):
    is_a = pl.program_id(0) < tiles_a
    m = jnp.where(is_a, ma_ref[...], mb_ref[...])
    x = jnp.where(is_a, xa_ref[...], xb_ref[...]) + _dot(m, wo_ref[...])
    x1_ref[...] = x
    hn = (x * lax.rsqrt(jnp.mean(x * x, axis=-1, keepdims=True) + EPS)) * g_ref[...]
    h_hi, h_lo = _split_bf16(hn)
    h_ref[...] = h_hi
    lg = jnp.transpose(_dot3(h_hi, h_lo, whi_ref[...], wlo_ref[...]) + bias_ref[...])
    rowi = lax.broadcasted_iota(jnp.int32, lg.shape, 0)
    big = jnp.int32(LANES)
    neg = -jnp.inf
    glog = jnp.where(rowi < N_GROUPS, lg, neg)
    gmax = jnp.max(glog, axis=0, keepdims=True)
    pg_top = 1.0 / jnp.sum(jnp.exp(glog - gmax), axis=0, keepdims=True)
    grp = jnp.min(jnp.where(glog == gmax, rowi, big), axis=0, keepdims=True)
    lo = N_GROUPS + grp * EXPERTS_PER_GROUP
    le = jnp.where((rowi >= lo) & (rowi < lo + EXPERTS_PER_GROUP), lg, neg)
    v1 = jnp.max(le, axis=0, keepdims=True)
    i1 = jnp.min(jnp.where(le == v1, rowi, big), axis=0, keepdims=True)
    le2 = jnp.where(rowi == i1, neg, le)
    v2 = jnp.max(le2, axis=0, keepdims=True)
    i2 = jnp.min(jnp.where(le2 == v2, rowi, big), axis=0, keepdims=True)
    e2 = jnp.exp(v2 - v1)
    den = 1.0 + e2
    w1 = pg_top * (1.0 / den)
    w2 = pg_top * (e2 / den)
    row8 = lax.broadcasted_iota(jnp.int32, eid_ref.shape, 0)
    eid_ref[...] = jnp.where(row8 == 0, i1 - N_GROUPS, jnp.where(row8 == 1, i2 - N_GROUPS, 0))
    wgt_ref[...] = jnp.where(row8 == 0, w1, jnp.where(row8 == 1, w2, 0.0))
    hit = jnp.where((rowi == i1 - N_GROUPS) | (rowi == i2 - N_GROUPS), 1.0, 0.0)
    per_expert = jnp.broadcast_to(jnp.sum(hit, axis=1, keepdims=True), (LANES, LANES))
    cnt_ref[0] = jnp.transpose(per_expert)[:SUBLANES].astype(jnp.int32)


def out_proj_route(ma, mb, xa, xb, w_out, g, whi, wlo, bias, tm):
    d = xa.shape[1]
    ta, tb = xa.shape[0] // tm, xb.shape[0] // tm
    n = (ta + tb) * tm
    rows_a = lambda i: (jnp.minimum(i, ta - 1), 0)
    rows_b = lambda i: (jnp.maximum(i - ta, 0), 0)
    fixed = lambda i: (0, 0)
    return pl.pallas_call(
        functools.partial(_out_route_body, tiles_a=ta),
        grid=(ta + tb,),
        in_specs=[pl.BlockSpec((tm, d), rows_a),
                  pl.BlockSpec((tm, d), rows_b),
                  pl.BlockSpec((tm, d), rows_a),
                  pl.BlockSpec((tm, d), rows_b),
                  pl.BlockSpec((d, d), fixed),
                  pl.BlockSpec((1, d), fixed),
                  pl.BlockSpec((d, LANES), fixed),
                  pl.BlockSpec((d, LANES), fixed),
                  pl.BlockSpec((1, LANES), fixed)],
        out_specs=[pl.BlockSpec((tm, d), lambda i: (i, 0)),
                   pl.BlockSpec((tm, d), lambda i: (i, 0)),
                   pl.BlockSpec((SUBLANES, tm), lambda i: (0, i)),
                   pl.BlockSpec((SUBLANES, tm), lambda i: (0, i)),
                   pl.BlockSpec((1, SUBLANES, LANES), lambda i: (i, 0, 0))],
        out_shape=[jax.ShapeDtypeStruct((n, d), F32),
                   jax.ShapeDtypeStruct((n, d), BF16),
                   jax.ShapeDtypeStruct((SUBLANES, n), jnp.int32),
                   jax.ShapeDtypeStruct((SUBLANES, n), F32),
                   jax.ShapeDtypeStruct((n // tm, SUBLANES, LANES), jnp.int32)],
        compiler_params=_cparams(("parallel",)),
        name="out_proj_route",
    )(ma, mb, xa, xb, w_out, g.reshape(1, d), whi, wlo, bias)


def _expert_body(be_tab, first_tab, slot_tab, next_tab, nused, x_ref, sw_ref, wg_hbm, wu_hbm, wd_hbm, o_ref,
                 stg_g, stg_u, stg_d, wg_s, wu_s, wd_s, sem):
    blk = pl.program_id(0)

    def weight_copies(e, slot):
        return (pltpu.make_async_copy(wg_hbm.at[e], stg_g.at[slot], sem.at[slot, 0]),
                pltpu.make_async_copy(wu_hbm.at[e], stg_u.at[slot], sem.at[slot, 1]),
                pltpu.make_async_copy(wd_hbm.at[e], stg_d.at[slot], sem.at[slot, 2]))

    @pl.when((blk == 0) & (nused[0] > 0))
    def _():
        for cp in weight_copies(be_tab[0], 0):
            cp.start()

    @pl.when(blk < nused[0])
    def _():
        @pl.when(first_tab[blk] == 1)
        def _():
            slot = slot_tab[blk]

            @pl.when(next_tab[blk] >= 0)
            def _():
                for cp in weight_copies(next_tab[blk], 1 - slot):
                    cp.start()

            for cp in weight_copies(be_tab[blk], slot):
                cp.wait()
            wg_s[...] = stg_g[slot].astype(BF16)
            wu_s[...] = stg_u[slot].astype(BF16)
            wd_s[...] = stg_d[slot].astype(BF16)

        x = x_ref[...]
        gate = _dot(x, wg_s[...])
        up = _dot(x, wu_s[...])
        hid = (gate * _sigmoid(gate)) * up
        o_ref[...] = (_dot(hid.astype(BF16), wd_s[...]) * sw_ref[...]).astype(o_ref.dtype)

    @pl.when(blk >= nused[0])
    def _():
        o_ref[...] = jnp.zeros(o_ref.shape, o_ref.dtype)


def experts(x_sorted, slot_w, block_e, first, slot_tab, next_tab, nused, w_gate, w_up, w_down, bm):
    n_slots, d = x_sorted.shape
    de = w_gate.shape[2]
    n_blocks = n_slots // bm
    row = lambda i, *_: (i, 0)
    grid_spec = pltpu.PrefetchScalarGridSpec(
        num_scalar_prefetch=5,
        grid=(n_blocks,),
        in_specs=[pl.BlockSpec((bm, d), row),
                  pl.BlockSpec((bm, 1), row),
                  pl.BlockSpec(memory_space=pl.ANY),
                  pl.BlockSpec(memory_space=pl.ANY),
                  pl.BlockSpec(memory_space=pl.ANY)],
        out_specs=pl.BlockSpec((bm, d), row),
        scratch_shapes=[pltpu.VMEM((2, d, de), F32), pltpu.VMEM((2, d, de), F32), pltpu.VMEM((2, de, d), F32),
                        pltpu.VMEM((d, de), BF16), pltpu.VMEM((d, de), BF16), pltpu.VMEM((de, d), BF16),
                        pltpu.SemaphoreType.DMA((2, 3))],
    )
    return pl.pallas_call(
        _expert_body,
        grid_spec=grid_spec,
        out_shape=jax.ShapeDtypeStruct((n_slots, d), BF16),
        compiler_params=_cparams(("arbitrary",)),
        name="experts",
    )(block_e, first, slot_tab, next_tab, nused, x_sorted, slot_w, w_gate, w_up, w_down)


def _final_body(x_ref, ma_ref, mb_ref, g_ref, o_ref):
    x = x_ref[...] + (ma_ref[...].astype(F32) + mb_ref[...].astype(F32))
    o_ref[...] = (x * lax.rsqrt(jnp.mean(x * x, axis=-1, keepdims=True) + EPS)) * g_ref[...]


def final_norm(x, ma, mb, g, tm, n, row0):
    d = x.shape[1]
    assert row0 % tm == 0 and n % tm == 0
    spec = pl.BlockSpec((tm, d), lambda i: (i, 0))
    mspec = pl.BlockSpec((tm, d), lambda i: (i + row0 // tm, 0))
    return pl.pallas_call(
        _final_body,
        grid=(n // tm,),
        in_specs=[mspec, mspec, mspec, pl.BlockSpec((1, d), lambda i: (0, 0))],
        out_specs=spec,
        out_shape=jax.ShapeDtypeStruct((n, d), F32),
        compiler_params=_cparams(("parallel",)),
        name="final_norm",
    )(x, ma, mb, g.reshape(1, d))


def _tile(n, pref):
    t = min(n, pref)
    while n % t:
        t //= 2
    return t


def _rope_table(pos, b):
    half = ROPE_DIM // 2
    inv = ROPE_BASE ** (-jnp.arange(half, dtype=F32) / half)
    ang = pos.astype(F32)[:, None] * inv[None, :]
    cos, sin = jnp.cos(ang), jnp.sin(ang)
    cs = jnp.concatenate([cos, cos, sin, sin], axis=-1)
    return jnp.tile(cs, (b, 1))


def _rot_cols(w):
    half = ROPE_DIM // 2
    return jnp.concatenate([-w[..., half:], w[..., :half]], axis=-1)


def _moe(h_bf, eid, wgt, counts, w_gate, w_up, w_down, bm):
    n, d = h_bf.shape
    n_exp = w_gate.shape[0]
    a = n * TOP_K
    e_flat = eid.reshape(-1)
    w_flat = wgt.reshape(-1)
    padded = (counts + bm - 1) // bm * bm
    pend = jnp.cumsum(padded)
    n_blocks = (a + n_exp * (bm - 1) + bm - 1) // bm
    n_slots = n_blocks * bm
    block_e = jnp.minimum(jnp.searchsorted(pend, jnp.arange(n_blocks, dtype=jnp.int32) * bm, side='right'),
                          n_exp - 1).astype(jnp.int32)
    first = jnp.concatenate([jnp.ones((1,), jnp.int32), (block_e[1:] != block_e[:-1]).astype(jnp.int32)])
    nused = (pend[-1] // bm).astype(jnp.int32).reshape(1)
    slot_tab = (jnp.cumsum(first) - 1) % 2
    e_ids = jnp.arange(n_exp, dtype=jnp.int32)
    later = lax.cummin(jnp.where(counts > 0, e_ids, n_exp), axis=0, reverse=True)
    next_e = jnp.concatenate([later[1:], jnp.full((1,), n_exp, jnp.int32)])
    next_tab = jnp.where(next_e < n_exp, next_e, -1)[block_e].astype(jnp.int32)
    n_pad = n_slots - a
    pad_end = jnp.cumsum(padded - counts)
    pad_key = jnp.sum(jnp.arange(n_pad, dtype=jnp.int32)[None, :] >= pad_end[:, None], axis=0, dtype=jnp.int32)
    keys = jnp.concatenate([e_flat, pad_key])
    toks = jnp.concatenate([jnp.arange(a, dtype=jnp.int32) // TOP_K, jnp.arange(n_pad, dtype=jnp.int32) % n])
    wts = jnp.concatenate([w_flat, jnp.zeros((n_pad,), F32)])
    iota = jnp.arange(n_slots, dtype=jnp.int32)
    _, slot_tok, slot_w, slot_src = lax.sort((keys, toks, wts, iota), num_keys=1, is_stable=True)
    x_sorted = h_bf[slot_tok]
    out = experts(x_sorted, slot_w.reshape(n_slots, 1), block_e, first, slot_tab.astype(jnp.int32), next_tab, nused,
                  w_gate, w_up, w_down, bm)
    _, slot_of = lax.sort((slot_src, iota), num_keys=1)
    slot_of = slot_of[:a].reshape(n, TOP_K)
    return out[slot_of[:, 0]], out[slot_of[:, 1]]


def _mixer(x3, pos, ckv_past, kr_past, s0, conv0, wp):
    b, s, d = x3.shape
    n = b * s
    n_heads = wp['n_heads']
    hd = HEAD_DIM
    x = x3.reshape(n, d)
    cols = wp['cols']

    u = rms_matmul(x, wp['attn_norm_g'], wp['w_all'], _tile(n, IN_PROJ_TM), IN_PROJ_TN)

    cs = _rope_table(pos, b)
    scale = (hd + ROPE_DIM) ** -0.5 * float(np.log2(np.e))
    prompt = ckv_past is None
    t_att = _tile(s, ATTN_TILE)
    tm = min(_tile(s, 256), t_att)
    res = mla_prep(u, cs, wp['q_norm_g'], wp['kv_norm_g'], wp['wq_ext'], wp['wk'] if prompt else None,
                   wp['wvt'] if prompt else None, b, s, tm, t_att, cols['q_lat'] // 512, cols['kv_lat'] // 512,
                   cols['small'] // 512, n_heads, scale)
    ckv, kr_pad, q = res[:3]
    kr = kr_pad[:, :ROPE_DIM]
    lora = ckv.shape[1]
    if prompt:
        o_a = flash_prompt(q, res[3], res[4], t_att, ATTN_HEADS_PER_STEP).reshape(n, n_heads * hd)
    else:
        qn = q[..., :hd].transpose(1, 0, 2, 3).reshape(n_heads, n, hd)
        q_abs = head_matmul(qn, wp['w_uk_t'])
        q_abs = q_abs.reshape(n_heads, b, s, lora).transpose(1, 2, 0, 3).reshape(b, s * n_heads, lora)
        qr = q[..., hd:].transpose(0, 2, 1, 3).reshape(b, s * n_heads, hd)
        kr_past_t = jnp.swapaxes(kr_past, 1, 2)
        ckv_new = jnp.pad(ckv.reshape(b, s, lora), ((0, 0), (0, LANES - s), (0, 0)))
        kr_new = jnp.pad(kr_pad.reshape(b, s, LANES), ((0, 0), (0, LANES - s), (0, 0)))
        o_lat = attn_sample(q_abs, qr, ckv_past, kr_past_t, ckv_new, kr_new, s, n_heads)
        o_lat = o_lat.reshape(b, s, n_heads, lora).transpose(2, 0, 1, 3).reshape(n_heads, n, lora)
        o_h = head_matmul(o_lat, wp['w_uv_h'])
        o_a = o_h.reshape(n_heads, n, hd).transpose(1, 0, 2).reshape(n, n_heads * hd)

    qkv_w = 3 * n_heads * hd
    prev0 = jnp.pad(conv0, ((0, 0), (SUBLANES - (CONV_W - 1), 0), (0, 0)))
    merged, s_new = gdn(u, o_a, prev0, s0, wp['conv_w'], wp['a_log'], wp['dt_bias'], wp['gdn_norm_g'], b, s,
                        cols['z'] // d, cols['small'] // 512, cols['gate_a'] // d, cols['gate_b'] // d, n_heads)
    tail = u.reshape(b, s, -1)[:, max(s - (CONV_W - 1), 0):, :qkv_w]
    conv_new = jnp.concatenate([conv0, tail], axis=1)[:, -(CONV_W - 1):]

    return (merged, x), ckv.reshape(b, s, lora), kr.reshape(b, s, ROPE_DIM), s_new, conv_new


def _prep_weights(l, attn_norm_g, w_in, q_norm_g, w_uq, kv_norm_g, w_uk, w_uv, conv_w, a_log, dt_bias, gdn_norm_g,
                  w_out, ffn_norm_g, w_group, b_group, w_router, b_router, w_gate, w_up, w_down):
    d = w_in.shape[1]
    q_lora, n_heads, _ = w_uq.shape[1:]
    kv_lora = w_uk.shape[1]
    hd = HEAD_DIM
    qkv_w = 3 * n_heads * hd
    sizes = (q_lora, kv_lora, ROPE_DIM, qkv_w, n_heads, n_heads, n_heads * hd, d, d)
    offs = np.concatenate([[0], np.cumsum(sizes)])
    wi_t = jnp.swapaxes(w_in[l], 0, 1)
    part = lambda i: wi_t[offs[i]:offs[i + 1]]
    w_qlat, w_kvlat, w_kr, w_qkv, w_a, w_b, w_z, w_ga, w_gb = [part(i) for i in range(9)]
    zpad = lambda c: jnp.zeros((c, d), F32)
    half = ROPE_DIM // 2
    w_kr_rot = jnp.concatenate([-w_kr[half:], w_kr[:half]], axis=0)
    small = jnp.concatenate([w_kr, w_kr_rot, w_a, zpad(LANES - n_heads), w_b, zpad(LANES - n_heads),
                             zpad(LANES)], axis=0)
    w_all = jnp.concatenate([w_qkv, w_z, w_ga, w_gb, w_qlat, w_kvlat, small], axis=0).astype(BF16)
    cols = {'qkv': 0, 'z': qkv_w, 'gate_a': qkv_w + n_heads * hd, 'gate_b': qkv_w + n_heads * hd + d,
            'q_lat': qkv_w + n_heads * hd + 2 * d}
    cols['kv_lat'] = cols['q_lat'] + q_lora
    cols['small'] = cols['kv_lat'] + kv_lora
    wq = w_uq[l]
    wq_nope = wq[:, :, :hd].reshape(q_lora, n_heads * hd)
    wq_r = wq[:, :, hd:]
    wq_rope = jnp.concatenate([wq_r, _rot_cols(wq_r)], axis=-1).reshape(q_lora, n_heads * hd)
    wq_ext = jnp.concatenate([wq_nope, wq_rope], axis=1).astype(BF16)
    wk = w_uk[l].reshape(kv_lora, n_heads * hd).astype(BF16)
    wvt = w_uv[l].reshape(kv_lora, n_heads * hd).T.astype(BF16)
    wr = jnp.concatenate([w_group[l], w_router[l].transpose(1, 0, 2).reshape(d, -1)], axis=1)
    wr = jnp.pad(wr, ((0, 0), (0, LANES - wr.shape[1])))
    wr_hi = wr.astype(BF16)
    wr_lo = (wr - wr_hi.astype(F32)).astype(BF16)
    br = jnp.concatenate([b_group[l], b_router[l].reshape(-1)])
    br = jnp.pad(br, (0, LANES - br.shape[0])).reshape(1, LANES)
    return {
        'n_heads': n_heads, 'cols': cols, 'w_all': w_all, 'attn_norm_g': attn_norm_g[l],
        'q_norm_g': q_norm_g[l], 'kv_norm_g': kv_norm_g[l], 'wq_ext': wq_ext, 'wk': wk, 'wvt': wvt,
        'w_uk_t': w_uk[l].transpose(1, 2, 0).astype(BF16),
        'w_uv_h': w_uv[l].transpose(1, 0, 2).astype(BF16),
        'conv_w': conv_w[l], 'a_log': a_log[l], 'dt_bias': dt_bias[l], 'gdn_norm_g': gdn_norm_g[l],
        'w_out': w_out[l].astype(BF16), 'ffn_norm_g': ffn_norm_g[l], 'wr_hi': wr_hi, 'wr_lo': wr_lo, 'br': br,
        'w_gate': w_gate[l], 'w_up': w_up[l], 'w_down': w_down[l],
    }


def kernel(x_prompt, x_sample, cache_ckv, cache_k_rope, state_gdn, state_conv, attn_norm_g, w_in, q_norm_g, w_uq, kv_norm_g, w_uk, w_uv, conv_w, a_log, dt_bias, gdn_norm_g, w_out, ffn_norm_g, w_group, b_group, w_router, b_router, w_gate, w_up, w_down, final_norm_g):
    depth = w_in.shape[0]
    assert depth == 1, "final RMSNorm is fused into the last layer; deeper trunks are not supported"
    b_p, s_p, d = x_prompt.shape
    b_s, s_s, _ = x_sample.shape
    past = cache_ckv.shape[2]
    n_heads = w_uq.shape[2]
    qkv_w = 3 * n_heads * HEAD_DIM
    pos_p = jnp.arange(s_p, dtype=jnp.int32)
    pos_s = past + jnp.arange(s_s, dtype=jnp.int32)
    wp = _prep_weights(0, attn_norm_g, w_in, q_norm_g, w_uq, kv_norm_g, w_uk, w_uv, conv_w, a_log, dt_bias,
                       gdn_norm_g, w_out, ffn_norm_g, w_group, b_group, w_router, b_router, w_gate, w_up, w_down)
    s0 = jnp.zeros((b_p, n_heads, HEAD_DIM, HEAD_DIM), F32)
    c0 = jnp.zeros((b_p, CONV_W - 1, qkv_w), F32)
    (m_p, xr_p), ckv_p, kr_p, sg_p, sc_p = _mixer(x_prompt, pos_p, None, None, s0, c0, wp)
    (m_s, xr_s), ckv_s, kr_s, sg_s, sc_s = _mixer(x_sample, pos_s, cache_ckv[0], cache_k_rope[0], state_gdn[0],
                                                  state_conv[0], wp)

    n_p, n_s = b_p * s_p, b_s * s_s
    tm_s = _tile(int(np.gcd(n_p, n_s)), 512)
    x1, h_bf, eid, wgt, cnt = out_proj_route(m_p, m_s, xr_p, xr_s, wp['w_out'], wp['ffn_norm_g'], wp['wr_hi'],
                                             wp['wr_lo'], wp['br'], tm_s)
    n_exp = wp['w_gate'].shape[0]
    counts = cnt[:, 0, :n_exp].sum(axis=0)
    ma, mb = _moe(h_bf, eid[:TOP_K].T, wgt[:TOP_K].T, counts, wp['w_gate'], wp['w_up'], wp['w_down'], MOE_BLOCK)
    yp = final_norm(x1, ma, mb, final_norm_g, _tile(n_p, 512), n_p, 0).reshape(b_p, s_p, d)
    ys = final_norm(x1, ma, mb, final_norm_g, tm_s, n_s, n_p).reshape(b_s, s_s, d)
    return (yp, ys, ckv_p[None], kr_p[None], sg_p[None], sc_p[None], ckv_s[None], kr_s[None], sg_s[None], sc_s[None])
```

```python
import functools

import numpy as np
import jax
import jax.numpy as jnp
from jax import lax
from jax.experimental import pallas as pl
from jax.experimental.pallas import tpu as pltpu

F32 = jnp.float32
BF16 = jnp.bfloat16

EPS = 1e-6
CHUNK = 64
ROPE_BASE = 10000.0
HEAD_DIM = 128
ROPE_DIM = 64
CONV_W = 4
TOP_K = 2
N_GROUPS = 8
EXPERTS_PER_GROUP = 8

LANES = 128
SUBLANES = 8
BF16_ROWS = 16
VMEM_LIMIT = 56 * 1024 * 1024
ATTN_TILE = 512
ATTN_HEADS_PER_STEP = 4
MOE_BLOCK = 256
IN_PROJ_TN = 1536
IN_PROJ_TM = 1024
GDN_CHUNKS_PER_STEP = 2


def _cparams(sem):
    return pltpu.CompilerParams(dimension_semantics=sem, vmem_limit_bytes=VMEM_LIMIT)


def _dot(a, b):
    return jnp.dot(a, b, preferred_element_type=F32)


def _dot_nt(a, b):
    return lax.dot_general(a, b, (((1,), (1,)), ((), ())), preferred_element_type=F32)


def _dot_tn(a, b):
    return lax.dot_general(a, b, (((0,), (0,)), ((), ())), preferred_element_type=F32)


def _sigmoid(x):
    return 1.0 / (1.0 + jnp.exp(-x))


def _split_bf16(a):
    hi = a.astype(BF16)
    lo = (a - hi.astype(F32)).astype(BF16)
    return hi, lo


def _dot3(a_hi, a_lo, b_hi, b_lo):
    return _dot(a_hi, b_hi) + (_dot(a_hi, b_lo) + _dot(a_lo, b_hi))


def _rms_mm_body(x_ref, g_ref, w_ref, o_ref, h_ref):
    @pl.when(pl.program_id(1) == 0)
    def _():
        x = x_ref[...]
        ms = jnp.mean(x * x, axis=-1, keepdims=True)
        h_ref[...] = ((x * lax.rsqrt(ms + EPS)) * g_ref[...]).astype(BF16)

    o_ref[...] = _dot_nt(h_ref[...], w_ref[...])


def rms_matmul(x, g, w_t, tm, tn):
    n, k = x.shape
    m = w_t.shape[0]
    return pl.pallas_call(
        _rms_mm_body,
        grid=(n // tm, m // tn),
        in_specs=[pl.BlockSpec((tm, k), lambda i, j: (i, 0)),
                  pl.BlockSpec((1, k), lambda i, j: (0, 0)),
                  pl.BlockSpec((tn, k), lambda i, j: (j, 0))],
        out_specs=pl.BlockSpec((tm, tn), lambda i, j: (i, j)),
        out_shape=jax.ShapeDtypeStruct((n, m), F32),
        scratch_shapes=[pltpu.VMEM((tm, k), BF16)],
        compiler_params=_cparams(("parallel", "arbitrary")),
        name="rms_matmul",
    )(x, g.reshape(1, k), w_t)


def _conv_silu(x, prev8, w):
    row8 = lax.broadcasted_iota(jnp.int32, prev8.shape, 0)

    def shifted(k):
        xk = pltpu.roll(x, k, axis=0)
        top = jnp.where(row8 < k, pltpu.roll(prev8, k, axis=0), xk[0:SUBLANES])
        return jnp.concatenate([top, xk[SUBLANES:]], axis=0)

    y = shifted(CONV_W - 1) * w[0:1, :]
    for j in range(1, CONV_W - 1):
        y = y + shifted(CONV_W - 1 - j) * w[j:j + 1, :]
    y = y + x * w[CONV_W - 1:CONV_W, :]
    return y * _sigmoid(y)


def _mla_prep_body(ql_ref, kvl_ref, sm_ref, cs_ref, gq_ref, gkv_ref, wq_ref, *rest, n_heads, scale, with_kv):
    if with_kv:
        wk_ref, wvt_ref, ckv_ref, kr_ref, q_ref, k_ref, vt_ref = rest
    else:
        ckv_ref, kr_ref, q_ref = rest
    hd = HEAD_DIM
    ql = ql_ref[...]
    qn = ((ql * lax.rsqrt(jnp.mean(ql * ql, axis=-1, keepdims=True) + EPS)) * gq_ref[...]).astype(BF16)
    q_all = _dot(qn, wq_ref[...])
    kvl = kvl_ref[...]
    ckv = (kvl * lax.rsqrt(jnp.mean(kvl * kvl, axis=-1, keepdims=True) + EPS)) * gkv_ref[...]
    ckv_ref[...] = ckv
    if with_kv:
        ckv_b = ckv.astype(BF16)
        k_all = _dot(ckv_b, wk_ref[...])
        vt_all = _dot_nt(wvt_ref[...], ckv_b)

    cs = cs_ref[...]
    lane = lax.broadcasted_iota(jnp.int32, cs.shape, 1)

    def rope(t):
        t = t * cs
        return jnp.where(lane < ROPE_DIM, t + pltpu.roll(t, ROPE_DIM, axis=1), 0.0)

    kr = rope(sm_ref[:, 0:LANES])
    kr_ref[...] = kr
    kr_b = kr.astype(BF16)
    off = n_heads * hd
    for h in range(n_heads):
        q_ref[0, h, :, 0:hd] = (q_all[:, h * hd:(h + 1) * hd] * scale).astype(BF16)
        q_ref[0, h, :, hd:2 * hd] = (rope(q_all[:, off + h * hd:off + (h + 1) * hd]) * scale).astype(BF16)
        if with_kv:
            k_ref[0, h, :, 0:hd] = k_all[:, h * hd:(h + 1) * hd].astype(BF16)
            k_ref[0, h, :, hd:2 * hd] = kr_b
            vt_ref[0, h, 0, 0:hd, :] = vt_all[h * hd:(h + 1) * hd, :].astype(BF16)
            vt_ref[0, h, 0, hd:hd + BF16_ROWS, :] = jnp.ones((BF16_ROWS, vt_all.shape[1]), BF16)


def mla_prep(u, cs, gq, gkv, wq, wk, wvt, b, s, tm, t_att, col_ql, col_kvl, col_sm, n_heads, scale):
    lora = wq.shape[0]
    nt = s // tm
    row = lambda bi, si: bi * nt + si
    hd = HEAD_DIM
    with_kv = wk is not None
    full = lambda bi, si: (0, 0)
    in_specs = [pl.BlockSpec((tm, lora), lambda bi, si: (row(bi, si), col_ql)),
                pl.BlockSpec((tm, lora), lambda bi, si: (row(bi, si), col_kvl)),
                pl.BlockSpec((tm, 512), lambda bi, si: (row(bi, si), col_sm)),
                pl.BlockSpec((tm, LANES), lambda bi, si: (row(bi, si), 0)),
                pl.BlockSpec((1, lora), full),
                pl.BlockSpec((1, lora), full),
                pl.BlockSpec(wq.shape, full)]
    out_specs = [pl.BlockSpec((tm, lora), lambda bi, si: (row(bi, si), 0)),
                 pl.BlockSpec((tm, LANES), lambda bi, si: (row(bi, si), 0)),
                 pl.BlockSpec((1, n_heads, tm, 2 * hd), lambda bi, si: (bi, 0, si, 0))]
    out_shape = [jax.ShapeDtypeStruct((b * s, lora), F32),
                 jax.ShapeDtypeStruct((b * s, LANES), F32),
                 jax.ShapeDtypeStruct((b, n_heads, s, 2 * hd), BF16)]
    args = [u, u, u, cs, gq.reshape(1, lora), gkv.reshape(1, lora), wq]
    if with_kv:
        r = t_att // tm
        in_specs += [pl.BlockSpec(wk.shape, full), pl.BlockSpec(wvt.shape, full)]
        out_specs += [pl.BlockSpec((1, n_heads, tm, 2 * hd), lambda bi, si: (bi, 0, si, 0)),
                      pl.BlockSpec((1, n_heads, 1, hd + BF16_ROWS, tm),
                                   lambda bi, si: (bi, 0, si // r, 0, si % r))]
        out_shape += [jax.ShapeDtypeStruct((b, n_heads, s, 2 * hd), BF16),
                      jax.ShapeDtypeStruct((b, n_heads, s // t_att, hd + BF16_ROWS, t_att), BF16)]
        args += [wk, wvt]
    return pl.pallas_call(
        functools.partial(_mla_prep_body, n_heads=n_heads, scale=scale, with_kv=with_kv),
        grid=(b, nt),
        in_specs=in_specs,
        out_specs=out_specs,
        out_shape=out_shape,
        compiler_params=_cparams(("parallel", "parallel")),
        name="mla_prep",
    )(*args)


def _flash_body(q_ref, k_ref, vt_ref, o_ref, m_scr, acc_scr, *, t, gh):
    qi = pl.program_id(2)
    hd = HEAD_DIM
    m_scr[...] = jnp.full(m_scr.shape, -jnp.inf, F32)
    acc_scr[...] = jnp.zeros(acc_scr.shape, F32)

    def block(j, masked):
        start = pl.multiple_of(j * t, t)
        scs = [_dot_nt(k_ref[0, g, pl.ds(start, t), :], q_ref[0, g]) for g in range(gh)]
        for g in range(gh):
            sc = scs[g]
            if masked:
                krow = lax.broadcasted_iota(jnp.int32, sc.shape, 0)
                qcol = lax.broadcasted_iota(jnp.int32, sc.shape, 1)
                sc = jnp.where(krow // CHUNK <= qcol // CHUNK, sc, -jnp.inf)
            m_prev = m_scr[g]
            m_new = jnp.maximum(m_prev, jnp.max(sc, axis=0, keepdims=True))
            alpha = jnp.exp2(m_prev - m_new)
            p = jnp.exp2(sc - m_new)
            acc_scr[g] = alpha * acc_scr[g] + _dot(vt_ref[0, g, j], p.astype(BF16))
            m_scr[g] = m_new

    def full_block(j, carry):
        block(j, False)
        return carry

    lax.fori_loop(0, qi, full_block, 0)
    block(qi, True)
    for g in range(gh):
        acc = acc_scr[g]
        o_ref[0, :, g * hd:(g + 1) * hd] = jnp.transpose(acc[:hd] / acc[hd:hd + 1])


def flash_prompt(q, k, vt, t, gh):
    b, nh, s, dk = q.shape
    hd = HEAD_DIM
    hv = vt.shape[-2]
    assert t % CHUNK == 0 and s % t == 0 and nh % gh == 0 and vt.shape[-1] == t and hv > hd
    return pl.pallas_call(
        functools.partial(_flash_body, t=t, gh=gh),
        grid=(b, nh // gh, s // t),
        in_specs=[pl.BlockSpec((1, gh, t, dk), lambda bi, hp, qi: (bi, hp, qi, 0)),
                  pl.BlockSpec((1, gh, s, dk), lambda bi, hp, qi: (bi, hp, 0, 0)),
                  pl.BlockSpec((1, gh, s // t, hv, t), lambda bi, hp, qi: (bi, hp, 0, 0, 0))],
        out_specs=pl.BlockSpec((1, t, gh * hd), lambda bi, hp, qi: (bi, qi, hp)),
        out_shape=jax.ShapeDtypeStruct((b, s, nh * hd), F32),
        scratch_shapes=[pltpu.VMEM((gh, 1, t), F32), pltpu.VMEM((gh, hv, t), F32)],
        compiler_params=_cparams(("parallel", "parallel", "arbitrary")),
        name="flash_prompt",
    )(q, k, vt)


def _bmm_body(a_ref, b_ref, o_ref):
    o_ref[0] = _dot(a_ref[0].astype(BF16), b_ref[0])


def head_matmul(a, b):
    nh, m, k = a.shape
    n = b.shape[2]
    return pl.pallas_call(
        _bmm_body,
        grid=(nh,),
        in_specs=[pl.BlockSpec((1, m, k), lambda h: (h, 0, 0)),
                  pl.BlockSpec((1, k, n), lambda h: (h, 0, 0))],
        out_specs=pl.BlockSpec((1, m, n), lambda h: (h, 0, 0)),
        out_shape=jax.ShapeDtypeStruct((nh, m, n), F32),
        compiler_params=_cparams(("parallel",)),
        name="head_matmul",
    )(a, b)


def _attn_sample_body(qa_ref, qr_ref, kp_ref, krp_ref, kn_ref, krn_ref, o_ref, *, past, s_new, n_heads):
    qa = qa_ref[0].astype(BF16)
    qr = qr_ref[0].astype(BF16)
    kp = kp_ref[0].astype(BF16)
    krp_t = krp_ref[0].astype(BF16)
    kn = kn_ref[0].astype(BF16)
    krn = krn_ref[0].astype(BF16)
    s_past = _dot_nt(qa, kp) + _dot(qr[:, :ROPE_DIM], krp_t)
    s_n = _dot_nt(qa, kn) + _dot_nt(qr, krn)
    row = lax.broadcasted_iota(jnp.int32, s_n.shape, 0)
    col = lax.broadcasted_iota(jnp.int32, s_n.shape, 1)
    qpos = past + row // n_heads
    kpos = past + col
    valid = (col < s_new) & (kpos // CHUNK <= qpos // CHUNK)
    s_n = jnp.where(valid, s_n, -jnp.inf)
    m = jnp.maximum(jnp.max(s_past, axis=-1, keepdims=True), jnp.max(s_n, axis=-1, keepdims=True))
    pp = jnp.exp2(s_past - m)
    pn = jnp.exp2(s_n - m)
    l = jnp.sum(pp, axis=-1, keepdims=True) + jnp.sum(pn, axis=-1, keepdims=True)
    o_ref[0] = (_dot(pp.astype(BF16), kp) + _dot(pn.astype(BF16), kn)) / l


def attn_sample(qa, qr, ckv_past, kr_past, ckv_new, kr_new, s_new, n_heads):
    b, r, lora = qa.shape
    past = ckv_past.shape[1]
    return pl.pallas_call(
        functools.partial(_attn_sample_body, past=past, s_new=s_new, n_heads=n_heads),
        grid=(b,),
        in_specs=[pl.BlockSpec((1, r, lora), lambda i: (i, 0, 0)),
                  pl.BlockSpec((1, r, LANES), lambda i: (i, 0, 0)),
                  pl.BlockSpec((1, past, lora), lambda i: (i, 0, 0)),
                  pl.BlockSpec((1, ROPE_DIM, past), lambda i: (i, 0, 0)),
                  pl.BlockSpec((1, LANES, lora), lambda i: (i, 0, 0)),
                  pl.BlockSpec((1, LANES, LANES), lambda i: (i, 0, 0))],
        out_specs=pl.BlockSpec((1, r, lora), lambda i: (i, 0, 0)),
        out_shape=jax.ShapeDtypeStruct((b, r, lora), F32),
        compiler_params=_cparams(("parallel",)),
        name="attn_sample",
    )(qa, qr, ckv_past, kr_past, ckv_new, kr_new)


def _softplus(x):
    return jnp.maximum(x, 0.0) + jnp.log1p(jnp.exp(-jnp.abs(x)))


def _gdn_body(qkv_ref, z_ref, sm_ref, ga_ref, gb_ref, oa_ref, prev0_ref, s0_ref, convw_ref, alog_ref, dtb_ref,
              gn_ref, m_ref, sout_ref, s_scr, prev_scr, y_scr, *, L, n_heads, chunks):
    c = pl.program_id(1)

    @pl.when(c == 0)
    def _():
        s_scr[...] = s0_ref[0]
        prev_scr[...] = prev0_ref[0]

    for cc in range(chunks):
        rows = pl.ds(cc * L, L)
        _gdn_chunk(qkv_ref.at[rows], z_ref.at[rows], sm_ref.at[rows], ga_ref.at[rows], gb_ref.at[rows],
                   oa_ref.at[rows], convw_ref, alog_ref, dtb_ref, gn_ref, m_ref.at[rows], s_scr, prev_scr, y_scr,
                   L=L, n_heads=n_heads)

    @pl.when(c == pl.num_programs(1) - 1)
    def _():
        sout_ref[0] = s_scr[...]


def _gdn_chunk(qkv_ref, z_ref, sm_ref, ga_ref, gb_ref, oa_ref, convw_ref, alog_ref, dtb_ref, gn_ref, m_ref,
               s_scr, prev_scr, y_scr, *, L, n_heads):
    hd = HEAD_DIM
    heads = range(n_heads)

    x = qkv_ref[...]
    y_scr[...] = _conv_silu(x, prev_scr[...], convw_ref[...])
    prev_scr[...] = x[L - SUBLANES:L]

    a = sm_ref[:, LANES:2 * LANES]
    bb = sm_ref[:, 2 * LANES:3 * LANES]
    g = -jnp.exp(alog_ref[...]) * _softplus(a + dtb_ref[...])
    beta = _sigmoid(bb)
    rowl = lax.broadcasted_iota(jnp.int32, g.shape, 0)
    cum = g
    d = 1
    while d < L:
        cum = cum + jnp.where(rowl >= d, pltpu.roll(cum, d, axis=0), 0.0)
        d *= 2
    cum_t = jnp.transpose(jnp.concatenate([cum, jnp.zeros((LANES - L, LANES), F32)], axis=0))

    row = lax.broadcasted_iota(jnp.int32, (L, L), 0)
    col = lax.broadcasted_iota(jnp.int32, (L, L), 1)
    tri = row >= col
    strict = row > col
    eye = jnp.where(row == col, 1.0, 0.0)
    gn = gn_ref[...]

    q_b, k_b, kb_b, rhs_b, qg_b, kd_b = [], [], [], [], [], []
    for h in heads:
        qh = y_scr[:, h * hd:(h + 1) * hd]
        kh = y_scr[:, (n_heads + h) * hd:(n_heads + h + 1) * hd]
        vh = y_scr[:, (2 * n_heads + h) * hd:(2 * n_heads + h + 1) * hd]
        qh = (qh * lax.rsqrt(jnp.sum(qh * qh, axis=-1, keepdims=True) + EPS)) * (hd ** -0.5)
        kh = kh * lax.rsqrt(jnp.sum(kh * kh, axis=-1, keepdims=True) + EPS)
        gcol = cum[:, h:h + 1]
        bcol = beta[:, h:h + 1]
        e_g = jnp.exp(gcol)
        kb = kh * bcol
        q_b.append(qh.astype(BF16))
        k_b.append(kh.astype(BF16))
        kb_b.append(kb.astype(BF16))
        rhs_b.append(jnp.concatenate([vh * bcol, kb * e_g], axis=1).astype(BF16))
        qg_b.append((qh * e_g).astype(BF16))
        kd_b.append((kh * jnp.exp(cum[L - 1:L, h:h + 1] - gcol)).astype(BF16))

    kk = [_dot_nt(kb_b[h], k_b[h]) for h in heads]
    qk = [_dot_nt(q_b[h], k_b[h]) for h in heads]

    nmat, attn_b = [], []
    for h in heads:
        diff = cum[:, h:h + 1] - cum_t[h:h + 1, 0:L]
        decay = jnp.where(tri, jnp.exp(jnp.where(tri, diff, 0.0)), 0.0)
        nmat.append(jnp.where(strict, kk[h] * decay, 0.0))
        attn_b.append(jnp.where(tri, qk[h] * decay, 0.0).astype(BF16))

    tmat = [eye - nmat[h] for h in heads]
    p_b = [nmat[h].astype(BF16) for h in heads]
    pmat = [_dot(p_b[h], p_b[h]) for h in heads]
    power = 2
    while True:
        p_b = [pmat[h].astype(BF16) for h in heads]
        tmat = [tmat[h] + _dot(tmat[h].astype(BF16), p_b[h]) for h in heads]
        power *= 2
        if power >= L:
            break
        pmat = [_dot(p_b[h], p_b[h]) for h in heads]

    uw = [_dot(tmat[h].astype(BF16), rhs_b[h]) for h in heads]
    s_old = [s_scr[h] for h in heads]
    s_b = [s_old[h].astype(BF16) for h in heads]
    ws = [_dot(jnp.concatenate([uw[h][:, hd:].astype(BF16), qg_b[h]], axis=0), s_b[h]) for h in heads]
    vn_b = [(uw[h][:, :hd] - ws[h][:L]).astype(BF16) for h in heads]
    av = [_dot(attn_b[h], vn_b[h]) for h in heads]
    ks = [_dot_tn(kd_b[h], vn_b[h]) for h in heads]
    for h in heads:
        sl = slice(h * hd, (h + 1) * hd)
        o = ws[h][L:] + av[h]
        s_scr[h] = s_old[h] * jnp.exp(cum[L - 1:L, h:h + 1]) + ks[h]
        zh = z_ref[:, sl]
        on = (o * lax.rsqrt(jnp.mean(o * o, axis=-1, keepdims=True) + EPS)) * gn
        ob = on * (zh * _sigmoid(zh))
        m_ref[:, sl] = (_sigmoid(ga_ref[:, sl]) * oa_ref[:, sl] + _sigmoid(gb_ref[:, sl]) * ob).astype(BF16)


def gdn(u, o_a, prev0, s0, conv_w, a_log, dt_bias, gn, b, s, col_z, col_sm, col_ga, col_gb, n_heads):
    hd = HEAD_DIM
    L = min(s, CHUNK)
    chunks = GDN_CHUNKS_PER_STEP if (s // L) % GDN_CHUNKS_PER_STEP == 0 else 1
    nc = s // (L * chunks)
    rows = L * chunks
    qkv_w = 3 * n_heads * hd
    row = lambda bi, c: bi * nc + c
    alog = jnp.zeros((1, LANES), F32).at[0, :n_heads].set(a_log)
    dtb = jnp.zeros((1, LANES), F32).at[0, :n_heads].set(dt_bias)
    return pl.pallas_call(
        functools.partial(_gdn_body, L=L, n_heads=n_heads, chunks=chunks),
        grid=(b, nc),
        in_specs=[pl.BlockSpec((rows, qkv_w), lambda bi, c: (row(bi, c), 0)),
                  pl.BlockSpec((rows, n_heads * hd), lambda bi, c: (row(bi, c), col_z)),
                  pl.BlockSpec((rows, 512), lambda bi, c: (row(bi, c), col_sm)),
                  pl.BlockSpec((rows, n_heads * hd), lambda bi, c: (row(bi, c), col_ga)),
                  pl.BlockSpec((rows, n_heads * hd), lambda bi, c: (row(bi, c), col_gb)),
                  pl.BlockSpec((rows, n_heads * hd), lambda bi, c: (row(bi, c), 0)),
                  pl.BlockSpec((1, SUBLANES, qkv_w), lambda bi, c: (bi, 0, 0)),
                  pl.BlockSpec((1, n_heads, hd, hd), lambda bi, c: (bi, 0, 0, 0)),
                  pl.BlockSpec((CONV_W, qkv_w), lambda bi, c: (0, 0)),
                  pl.BlockSpec((1, LANES), lambda bi, c: (0, 0)),
                  pl.BlockSpec((1, LANES), lambda bi, c: (0, 0)),
                  pl.BlockSpec((1, hd), lambda bi, c: (0, 0))],
        out_specs=[pl.BlockSpec((rows, n_heads * hd), lambda bi, c: (row(bi, c), 0)),
                   pl.BlockSpec((1, n_heads, hd, hd), lambda bi, c: (bi, 0, 0, 0))],
        out_shape=[jax.ShapeDtypeStruct((b * s, n_heads * hd), BF16),
                   jax.ShapeDtypeStruct((b, n_heads, hd, hd), F32)],
        scratch_shapes=[pltpu.VMEM((n_heads, hd, hd), F32),
                        pltpu.VMEM((SUBLANES, qkv_w), F32),
                        pltpu.VMEM((L, qkv_w), F32)],
        compiler_params=_cparams(("parallel", "arbitrary")),
        name="gdn",
    )(u, u, u, u, u, o_a, prev0, s0, conv_w, alog, dtb, gn.reshape(1, hd))


def _out_route_body(ma_ref, mb_ref, xa_ref, xb_ref, wo_ref, g_ref, whi_ref, wlo_ref, bias_ref,
                    x1_ref, h_ref, eid_ref, wgt_ref, cnt_ref, *, tiles_a):
    is_a = pl.program_id(0) < tiles_a
    m = jnp.where(is_a, ma_ref[...], mb_ref[...])
    x = jnp.where(is_a, xa_ref[...], xb_ref[...]) + _dot(m, wo_ref[...])
    x1_ref[...] = x
    hn = (x * lax.rsqrt(jnp.mean(x * x, axis=-1, keepdims=True) + EPS)) * g_ref[...]
    h_hi, h_lo = _split_bf16(hn)
    h_ref[...] = h_hi
    lg = jnp.transpose(_dot3(h_hi, h_lo, whi_ref[...], wlo_ref[...]) + bias_ref[...])
    rowi = lax.broadcasted_iota(jnp.int32, lg.shape, 0)
    big = jnp.int32(LANES)
    neg = -jnp.inf
    glog = jnp.where(rowi < N_GROUPS, lg, neg)
    gmax = jnp.max(glog, axis=0, keepdims=True)
    pg_top = 1.0 / jnp.sum(jnp.exp(glog - gmax), axis=0, keepdims=True)
    grp = jnp.min(jnp.where(glog == gmax, rowi, big), axis=0, keepdims=True)
    lo = N_GROUPS + grp * EXPERTS_PER_GROUP
    le = jnp.where((rowi >= lo) & (rowi < lo + EXPERTS_PER_GROUP), lg, neg)
    v1 = jnp.max(le, axis=0, keepdims=True)
    i1 = jnp.min(jnp.where(le == v1, rowi, big), axis=0, keepdims=True)
    le2 = jnp.where(rowi == i1, neg, le)
    v2 = jnp.max(le2, axis=0, keepdims=True)
    i2 = jnp.min(jnp.where(le2 == v2, rowi, big), axis=0, keepdims=True)
    e2 = jnp.exp(v2 - v1)
    den = 1.0 + e2
    w1 = pg_top * (1.0 / den)
    w2 = pg_top * (e2 / den)
    row8 = lax.broadcasted_iota(jnp.int32, eid_ref.shape, 0)
    eid_ref[...] = jnp.where(row8 == 0, i1 - N_GROUPS, jnp.where(row8 == 1, i2 - N_GROUPS, 0))
    wgt_ref[...] = jnp.where(row8 == 0, w1, jnp.where(row8 == 1, w2, 0.0))
    hit = jnp.where((rowi == i1 - N_GROUPS) | (rowi == i2 - N_GROUPS), 1.0, 0.0)
    per_expert = jnp.broadcast_to(jnp.sum(hit, axis=1, keepdims=True), (LANES, LANES))
    cnt_ref[0] = jnp.transpose(per_expert)[:SUBLANES].astype(jnp.int32)


def out_proj_route(ma, mb, xa, xb, w_out, g, whi, wlo, bias, tm):
    d = xa.shape[1]
    ta, tb = xa.shape[0] // tm, xb.shape[0] // tm
    n = (ta + tb) * tm
    rows_a = lambda i: (jnp.minimum(i, ta - 1), 0)
    rows_b = lambda i: (jnp.maximum(i - ta, 0), 0)
    fixed = lambda i: (0, 0)
    return pl.pallas_call(
        functools.partial(_out_route_body, tiles_a=ta),
        grid=(ta + tb,),
        in_specs=[pl.BlockSpec((tm, d), rows_a),
                  pl.BlockSpec((tm, d), rows_b),
                  pl.BlockSpec((tm, d), rows_a),
                  pl.BlockSpec((tm, d), rows_b),
                  pl.BlockSpec((d, d), fixed),
                  pl.BlockSpec((1, d), fixed),
                  pl.BlockSpec((d, LANES), fixed),
                  pl.BlockSpec((d, LANES), fixed),
                  pl.BlockSpec((1, LANES), fixed)],
        out_specs=[pl.BlockSpec((tm, d), lambda i: (i, 0)),
                   pl.BlockSpec((tm, d), lambda i: (i, 0)),
                   pl.BlockSpec((SUBLANES, tm), lambda i: (0, i)),
                   pl.BlockSpec((SUBLANES, tm), lambda i: (0, i)),
                   pl.BlockSpec((1, SUBLANES, LANES), lambda i: (i, 0, 0))],
        out_shape=[jax.ShapeDtypeStruct((n, d), F32),
                   jax.ShapeDtypeStruct((n, d), BF16),
                   jax.ShapeDtypeStruct((SUBLANES, n), jnp.int32),
                   jax.ShapeDtypeStruct((SUBLANES, n), F32),
                   jax.ShapeDtypeStruct((n // tm, SUBLANES, LANES), jnp.int32)],
        compiler_params=_cparams(("parallel",)),
        name="out_proj_route",
    )(ma, mb, xa, xb, w_out, g.reshape(1, d), whi, wlo, bias)


def _expert_body(be_tab, first_tab, slot_tab, next_tab, nused, x_ref, sw_ref, wg_hbm, wu_hbm, wd_hbm, o_ref,
                 stg_g, stg_u, stg_d, wg_s, wu_s, wd_s, sem):
    blk = pl.program_id(0)

    def weight_copies(e, slot):
        return (pltpu.make_async_copy(wg_hbm.at[e], stg_g.at[slot], sem.at[slot, 0]),
                pltpu.make_async_copy(wu_hbm.at[e], stg_u.at[slot], sem.at[slot, 1]),
                pltpu.make_async_copy(wd_hbm.at[e], stg_d.at[slot], sem.at[slot, 2]))

    @pl.when((blk == 0) & (nused[0] > 0))
    def _():
        for cp in weight_copies(be_tab[0], 0):
            cp.start()

    @pl.when(blk < nused[0])
    def _():
        @pl.when(first_tab[blk] == 1)
        def _():
            slot = slot_tab[blk]

            @pl.when(next_tab[blk] >= 0)
            def _():
                for cp in weight_copies(next_tab[blk], 1 - slot):
                    cp.start()

            for cp in weight_copies(be_tab[blk], slot):
                cp.wait()
            wg_s[...] = stg_g[slot].astype(BF16)
            wu_s[...] = stg_u[slot].astype(BF16)
            wd_s[...] = stg_d[slot].astype(BF16)

        x = x_ref[...]
        gate = _dot(x, wg_s[...])
        up = _dot(x, wu_s[...])
        hid = (gate * _sigmoid(gate)) * up
        o_ref[...] = (_dot(hid.astype(BF16), wd_s[...]) * sw_ref[...]).astype(o_ref.dtype)

    @pl.when(blk >= nused[0])
    def _():
        o_ref[...] = jnp.zeros(o_ref.shape, o_ref.dtype)


def experts(x_sorted, slot_w, block_e, first, slot_tab, next_tab, nused, w_gate, w_up, w_down, bm):
    n_slots, d = x_sorted.shape
    de = w_gate.shape[2]
    n_blocks = n_slots // bm
    row = lambda i, *_: (i, 0)
    grid_spec = pltpu.PrefetchScalarGridSpec(
        num_scalar_prefetch=5,
        grid=(n_blocks,),
        in_specs=[pl.BlockSpec((bm, d), row),
                  pl.BlockSpec((bm, 1), row),
                  pl.BlockSpec(memory_space=pl.ANY),
                  pl.BlockSpec(memory_space=pl.ANY),
                  pl.BlockSpec(memory_space=pl.ANY)],
        out_specs=pl.BlockSpec((bm, d), row),
        scratch_shapes=[pltpu.VMEM((2, d, de), F32), pltpu.VMEM((2, d, de), F32), pltpu.VMEM((2, de, d), F32),
                        pltpu.VMEM((d, de), BF16), pltpu.VMEM((d, de), BF16), pltpu.VMEM((de, d), BF16),
                        pltpu.SemaphoreType.DMA((2, 3))],
    )
    return pl.pallas_call(
        _expert_body,
        grid_spec=grid_spec,
        out_shape=jax.ShapeDtypeStruct((n_slots, d), BF16),
        compiler_params=_cparams(("arbitrary",)),
        name="experts",
    )(block_e, first, slot_tab, next_tab, nused, x_sorted, slot_w, w_gate, w_up, w_down)


def _final_body(x_ref, ma_ref, mb_ref, g_ref, o_ref):
    x = x_ref[...] + (ma_ref[...].astype(F32) + mb_ref[...].astype(F32))
    o_ref[...] = (x * lax.rsqrt(jnp.mean(x * x, axis=-1, keepdims=True) + EPS)) * g_ref[...]


def final_norm(x, ma, mb, g, tm, n, row0):
    d = x.shape[1]
    assert row0 % tm == 0 and n % tm == 0
    spec = pl.BlockSpec((tm, d), lambda i: (i, 0))
    mspec = pl.BlockSpec((tm, d), lambda i: (i + row0 // tm, 0))
    return pl.pallas_call(
        _final_body,
        grid=(n // tm,),
        in_specs=[mspec, mspec, mspec, pl.BlockSpec((1, d), lambda i: (0, 0))],
        out_specs=spec,
        out_shape=jax.ShapeDtypeStruct((n, d), F32),
        compiler_params=_cparams(("parallel",)),
        name="final_norm",
    )(x, ma, mb, g.reshape(1, d))


def _tile(n, pref):
    t = min(n, pref)
    while n % t:
        t //= 2
    return t


def _rope_table(pos, b):
    half = ROPE_DIM // 2
    inv = ROPE_BASE ** (-jnp.arange(half, dtype=F32) / half)
    ang = pos.astype(F32)[:, None] * inv[None, :]
    cos, sin = jnp.cos(ang), jnp.sin(ang)
    cs = jnp.concatenate([cos, cos, sin, sin], axis=-1)
    return jnp.tile(cs, (b, 1))


def _rot_cols(w):
    half = ROPE_DIM // 2
    return jnp.concatenate([-w[..., half:], w[..., :half]], axis=-1)


def _moe(h_bf, eid, wgt, counts, w_gate, w_up, w_down, bm):
    n, d = h_bf.shape
    n_exp = w_gate.shape[0]
    a = n * TOP_K
    e_flat = eid.reshape(-1)
    w_flat = wgt.reshape(-1)
    padded = (counts + bm - 1) // bm * bm
    pend = jnp.cumsum(padded)
    n_blocks = (a + n_exp * (bm - 1) + bm - 1) // bm
    n_slots = n_blocks * bm
    block_start = jnp.arange(n_blocks, dtype=jnp.int32) * bm
    block_e = jnp.minimum(jnp.sum(pend[:, None] <= block_start[None, :], axis=0, dtype=jnp.int32), n_exp - 1)
    first = jnp.concatenate([jnp.ones((1,), jnp.int32), (block_e[1:] != block_e[:-1]).astype(jnp.int32)])
    nused = (pend[-1] // bm).astype(jnp.int32).reshape(1)
    slot_tab = (jnp.cumsum(first) - 1) % 2
    e_ids = jnp.arange(n_exp, dtype=jnp.int32)
    later = lax.cummin(jnp.where(counts > 0, e_ids, n_exp), axis=0, reverse=True)
    next_e = jnp.concatenate([later[1:], jnp.full((1,), n_exp, jnp.int32)])
    next_tab = jnp.where(next_e < n_exp, next_e, -1)[block_e].astype(jnp.int32)
    n_pad = n_slots - a
    pad_end = jnp.cumsum(padded - counts)
    pad_key = jnp.sum(jnp.arange(n_pad, dtype=jnp.int32)[None, :] >= pad_end[:, None], axis=0, dtype=jnp.int32)
    keys = jnp.concatenate([e_flat, pad_key])
    toks = jnp.concatenate([jnp.arange(a, dtype=jnp.int32) // TOP_K, jnp.arange(n_pad, dtype=jnp.int32) % n])
    wts = jnp.concatenate([w_flat, jnp.zeros((n_pad,), F32)])
    iota = jnp.arange(n_slots, dtype=jnp.int32)
    _, slot_tok, slot_w, slot_src = lax.sort((keys, toks, wts, iota), num_keys=1, is_stable=True)
    x_sorted = h_bf[slot_tok]
    out = experts(x_sorted, slot_w.reshape(n_slots, 1), block_e, first, slot_tab.astype(jnp.int32), next_tab, nused,
                  w_gate, w_up, w_down, bm)
    _, slot_of = lax.sort((slot_src, iota), num_keys=1)
    slot_of = slot_of[:a].reshape(n, TOP_K)
    return out[slot_of[:, 0]], out[slot_of[:, 1]]


def _mixer(x3, pos, ckv_past, kr_past, s0, conv0, wp):
    b, s, d = x3.shape
    n = b * s
    n_heads = wp['n_heads']
    hd = HEAD_DIM
    x = x3.reshape(n, d)
    cols = wp['cols']

    u = rms_matmul(x, wp['attn_norm_g'], wp['w_all'], _tile(n, IN_PROJ_TM), IN_PROJ_TN)

    cs = _rope_table(pos, b)
    scale = (hd + ROPE_DIM) ** -0.5 * float(np.log2(np.e))
    prompt = ckv_past is None
    t_att = _tile(s, ATTN_TILE)
    tm = min(_tile(s, 256), t_att)
    res = mla_prep(u, cs, wp['q_norm_g'], wp['kv_norm_g'], wp['wq_ext'], wp['wk'] if prompt else None,
                   wp['wvt'] if prompt else None, b, s, tm, t_att, cols['q_lat'] // 512, cols['kv_lat'] // 512,
                   cols['small'] // 512, n_heads, scale)
    ckv, kr_pad, q = res[:3]
    kr = kr_pad[:, :ROPE_DIM]
    lora = ckv.shape[1]
    if prompt:
        o_a = flash_prompt(q, res[3], res[4], t_att, ATTN_HEADS_PER_STEP).reshape(n, n_heads * hd)
    else:
        qn = q[..., :hd].transpose(1, 0, 2, 3).reshape(n_heads, n, hd)
        q_abs = head_matmul(qn, wp['w_uk_t'])
        q_abs = q_abs.reshape(n_heads, b, s, lora).transpose(1, 2, 0, 3).reshape(b, s * n_heads, lora)
        qr = q[..., hd:].transpose(0, 2, 1, 3).reshape(b, s * n_heads, hd)
        kr_past_t = jnp.swapaxes(kr_past, 1, 2)
        ckv_new = jnp.pad(ckv.reshape(b, s, lora), ((0, 0), (0, LANES - s), (0, 0)))
        kr_new = jnp.pad(kr_pad.reshape(b, s, LANES), ((0, 0), (0, LANES - s), (0, 0)))
        o_lat = attn_sample(q_abs, qr, ckv_past, kr_past_t, ckv_new, kr_new, s, n_heads)
        o_lat = o_lat.reshape(b, s, n_heads, lora).transpose(2, 0, 1, 3).reshape(n_heads, n, lora)
        o_h = head_matmul(o_lat, wp['w_uv_h'])
        o_a = o_h.reshape(n_heads, n, hd).transpose(1, 0, 2).reshape(n, n_heads * hd)

    qkv_w = 3 * n_heads * hd
    prev0 = jnp.pad(conv0, ((0, 0), (SUBLANES - (CONV_W - 1), 0), (0, 0)))
    merged, s_new = gdn(u, o_a, prev0, s0, wp['conv_w'], wp['a_log'], wp['dt_bias'], wp['gdn_norm_g'], b, s,
                        cols['z'] // d, cols['small'] // 512, cols['gate_a'] // d, cols['gate_b'] // d, n_heads)
    tail = u.reshape(b, s, -1)[:, max(s - (CONV_W - 1), 0):, :qkv_w]
    conv_new = jnp.concatenate([conv0, tail], axis=1)[:, -(CONV_W - 1):]

    return (merged, x), ckv.reshape(b, s, lora), kr.reshape(b, s, ROPE_DIM), s_new, conv_new


def _prep_weights(l, attn_norm_g, w_in, q_norm_g, w_uq, kv_norm_g, w_uk, w_uv, conv_w, a_log, dt_bias, gdn_norm_g,
                  w_out, ffn_norm_g, w_group, b_group, w_router, b_router, w_gate, w_up, w_down):
    d = w_in.shape[1]
    q_lora, n_heads, _ = w_uq.shape[1:]
    kv_lora = w_uk.shape[1]
    hd = HEAD_DIM
    qkv_w = 3 * n_heads * hd
    sizes = (q_lora, kv_lora, ROPE_DIM, qkv_w, n_heads, n_heads, n_heads * hd, d, d)
    offs = np.concatenate([[0], np.cumsum(sizes)])
    wi_t = jnp.swapaxes(w_in[l], 0, 1)
    part = lambda i: wi_t[offs[i]:offs[i + 1]]
    w_qlat, w_kvlat, w_kr, w_qkv, w_a, w_b, w_z, w_ga, w_gb = [part(i) for i in range(9)]
    zpad = lambda c: jnp.zeros((c, d), F32)
    half = ROPE_DIM // 2
    w_kr_rot = jnp.concatenate([-w_kr[half:], w_kr[:half]], axis=0)
    small = jnp.concatenate([w_kr, w_kr_rot, w_a, zpad(LANES - n_heads), w_b, zpad(LANES - n_heads),
                             zpad(LANES)], axis=0)
    w_all = jnp.concatenate([w_qkv, w_z, w_ga, w_gb, w_qlat, w_kvlat, small], axis=0).astype(BF16)
    cols = {'qkv': 0, 'z': qkv_w, 'gate_a': qkv_w + n_heads * hd, 'gate_b': qkv_w + n_heads * hd + d,
            'q_lat': qkv_w + n_heads * hd + 2 * d}
    cols['kv_lat'] = cols['q_lat'] + q_lora
    cols['small'] = cols['kv_lat'] + kv_lora
    wq = w_uq[l]
    wq_nope = wq[:, :, :hd].reshape(q_lora, n_heads * hd)
    wq_r = wq[:, :, hd:]
    wq_rope = jnp.concatenate([wq_r, _rot_cols(wq_r)], axis=-1).reshape(q_lora, n_heads * hd)
    wq_ext = jnp.concatenate([wq_nope, wq_rope], axis=1).astype(BF16)
    wk = w_uk[l].reshape(kv_lora, n_heads * hd).astype(BF16)
    wvt = w_uv[l].reshape(kv_lora, n_heads * hd).T.astype(BF16)
    wr = jnp.concatenate([w_group[l], w_router[l].transpose(1, 0, 2).reshape(d, -1)], axis=1)
    wr = jnp.pad(wr, ((0, 0), (0, LANES - wr.shape[1])))
    wr_hi = wr.astype(BF16)
    wr_lo = (wr - wr_hi.astype(F32)).astype(BF16)
    br = jnp.concatenate([b_group[l], b_router[l].reshape(-1)])
    br = jnp.pad(br, (0, LANES - br.shape[0])).reshape(1, LANES)
    return {
        'n_heads': n_heads, 'cols': cols, 'w_all': w_all, 'attn_norm_g': attn_norm_g[l],
        'q_norm_g': q_norm_g[l], 'kv_norm_g': kv_norm_g[l], 'wq_ext': wq_ext, 'wk': wk, 'wvt': wvt,
        'w_uk_t': w_uk[l].transpose(1, 2, 0).astype(BF16),
        'w_uv_h': w_uv[l].transpose(1, 0, 2).astype(BF16),
        'conv_w': conv_w[l], 'a_log': a_log[l], 'dt_bias': dt_bias[l], 'gdn_norm_g': gdn_norm_g[l],
        'w_out': w_out[l].astype(BF16), 'ffn_norm_g': ffn_norm_g[l], 'wr_hi': wr_hi, 'wr_lo': wr_lo, 'br': br,
        'w_gate': w_gate[l], 'w_up': w_up[l], 'w_down': w_down[l],
    }


def kernel(x_prompt, x_sample, cache_ckv, cache_k_rope, state_gdn, state_conv, attn_norm_g, w_in, q_norm_g, w_uq, kv_norm_g, w_uk, w_uv, conv_w, a_log, dt_bias, gdn_norm_g, w_out, ffn_norm_g, w_group, b_group, w_router, b_router, w_gate, w_up, w_down, final_norm_g):
    depth = w_in.shape[0]
    assert depth == 1, "final RMSNorm is fused into the last layer; deeper trunks are not supported"
    b_p, s_p, d = x_prompt.shape
    b_s, s_s, _ = x_sample.shape
    past = cache_ckv.shape[2]
    n_heads = w_uq.shape[2]
    qkv_w = 3 * n_heads * HEAD_DIM
    pos_p = jnp.arange(s_p, dtype=jnp.int32)
    pos_s = past + jnp.arange(s_s, dtype=jnp.int32)
    wp = _prep_weights(0, attn_norm_g, w_in, q_norm_g, w_uq, kv_norm_g, w_uk, w_uv, conv_w, a_log, dt_bias,
                       gdn_norm_g, w_out, ffn_norm_g, w_group, b_group, w_router, b_router, w_gate, w_up, w_down)
    s0 = jnp.zeros((b_p, n_heads, HEAD_DIM, HEAD_DIM), F32)
    c0 = jnp.zeros((b_p, CONV_W - 1, qkv_w), F32)
    (m_p, xr_p), ckv_p, kr_p, sg_p, sc_p = _mixer(x_prompt, pos_p, None, None, s0, c0, wp)
    (m_s, xr_s), ckv_s, kr_s, sg_s, sc_s = _mixer(x_sample, pos_s, cache_ckv[0], cache_k_rope[0], state_gdn[0],
                                                  state_conv[0], wp)

    n_p, n_s = b_p * s_p, b_s * s_s
    tm_s = _tile(int(np.gcd(n_p, n_s)), 512)
    x1, h_bf, eid, wgt, cnt = out_proj_route(m_p, m_s, xr_p, xr_s, wp['w_out'], wp['ffn_norm_g'], wp['wr_hi'],
                                             wp['wr_lo'], wp['br'], tm_s)
    n_exp = wp['w_gate'].shape[0]
    counts = cnt[:, 0, :n_exp].sum(axis=0)
    ma, mb = _moe(h_bf, eid[:TOP_K].T, wgt[:TOP_K].T, counts, wp['w_gate'], wp['w_up'], wp['w_down'], MOE_BLOCK)
    yp = final_norm(x1, ma, mb, final_norm_g, _tile(n_p, 512), n_p, 0).reshape(b_p, s_p, d)
    ys = final_norm(x1, ma, mb, final_norm_g, tm_s, n_s, n_p).reshape(b_s, s_s, d)
    return (yp, ys, ckv_p[None], kr_p[None], sg_p[None], sc_p[None], ckv_s[None], kr_s[None], sg_s[None], sc_s[None])
```

```python
import functools

import numpy as np
import jax
import jax.numpy as jnp
from jax import lax
from jax.experimental import pallas as pl
from jax.experimental.pallas import tpu as pltpu

F32 = jnp.float32
BF16 = jnp.bfloat16

EPS = 1e-6
CHUNK = 64
ROPE_BASE = 10000.0
HEAD_DIM = 128
ROPE_DIM = 64
CONV_W = 4
TOP_K = 2
N_GROUPS = 8
EXPERTS_PER_GROUP = 8

LANES = 128
SUBLANES = 8
BF16_ROWS = 16
VMEM_LIMIT = 56 * 1024 * 1024
ATTN_TILE = 512
ATTN_HEADS_PER_STEP = 4
MOE_BLOCK = 512
IN_PROJ_TN = 1536
IN_PROJ_TM = 1024
GDN_CHUNKS_PER_STEP = 2


def _cparams(sem):
    return pltpu.CompilerParams(dimension_semantics=sem, vmem_limit_bytes=VMEM_LIMIT)


def _dot(a, b):
    return jnp.dot(a, b, preferred_element_type=F32)


def _dot_nt(a, b):
    return lax.dot_general(a, b, (((1,), (1,)), ((), ())), preferred_element_type=F32)


def _dot_tn(a, b):
    return lax.dot_general(a, b, (((0,), (0,)), ((), ())), preferred_element_type=F32)


def _sigmoid(x):
    return 1.0 / (1.0 + jnp.exp(-x))


def _split_bf16(a):
    hi = a.astype(BF16)
    lo = (a - hi.astype(F32)).astype(BF16)
    return hi, lo


def _dot3(a_hi, a_lo, b_hi, b_lo):
    return _dot(a_hi, b_hi) + (_dot(a_hi, b_lo) + _dot(a_lo, b_hi))


def _rms_mm_body(x_ref, g_ref, w_ref, o_ref, h_ref):
    @pl.when(pl.program_id(1) == 0)
    def _():
        x = x_ref[...]
        ms = jnp.mean(x * x, axis=-1, keepdims=True)
        h_ref[...] = ((x * lax.rsqrt(ms + EPS)) * g_ref[...]).astype(BF16)

    o_ref[...] = _dot_nt(h_ref[...], w_ref[...])


def rms_matmul(x, g, w_t, tm, tn):
    n, k = x.shape
    m = w_t.shape[0]
    return pl.pallas_call(
        _rms_mm_body,
        grid=(n // tm, m // tn),
        in_specs=[pl.BlockSpec((tm, k), lambda i, j: (i, 0)),
                  pl.BlockSpec((1, k), lambda i, j: (0, 0)),
                  pl.BlockSpec((tn, k), lambda i, j: (j, 0))],
        out_specs=pl.BlockSpec((tm, tn), lambda i, j: (i, j)),
        out_shape=jax.ShapeDtypeStruct((n, m), F32),
        scratch_shapes=[pltpu.VMEM((tm, k), BF16)],
        compiler_params=_cparams(("parallel", "arbitrary")),
        name="rms_matmul",
    )(x, g.reshape(1, k), w_t)


def _conv_silu(x, prev8, w):
    row8 = lax.broadcasted_iota(jnp.int32, prev8.shape, 0)

    def shifted(k):
        xk = pltpu.roll(x, k, axis=0)
        top = jnp.where(row8 < k, pltpu.roll(prev8, k, axis=0), xk[0:SUBLANES])
        return jnp.concatenate([top, xk[SUBLANES:]], axis=0)

    y = shifted(CONV_W - 1) * w[0:1, :]
    for j in range(1, CONV_W - 1):
        y = y + shifted(CONV_W - 1 - j) * w[j:j + 1, :]
    y = y + x * w[CONV_W - 1:CONV_W, :]
    return y * _sigmoid(y)


def _mla_prep_body(ql_ref, kvl_ref, sm_ref, cs_ref, gq_ref, gkv_ref, wq_ref, *rest, n_heads, scale, with_kv):
    if with_kv:
        wk_ref, wvt_ref, ckv_ref, kr_ref, q_ref, k_ref, vt_ref = rest
    else:
        ckv_ref, kr_ref, q_ref = rest
    hd = HEAD_DIM
    ql = ql_ref[...]
    qn = ((ql * lax.rsqrt(jnp.mean(ql * ql, axis=-1, keepdims=True) + EPS)) * gq_ref[...]).astype(BF16)
    q_all = _dot(qn, wq_ref[...])
    kvl = kvl_ref[...]
    ckv = (kvl * lax.rsqrt(jnp.mean(kvl * kvl, axis=-1, keepdims=True) + EPS)) * gkv_ref[...]
    ckv_ref[...] = ckv
    if with_kv:
        ckv_b = ckv.astype(BF16)
        k_all = _dot(ckv_b, wk_ref[...])
        vt_all = _dot_nt(wvt_ref[...], ckv_b)

    cs = cs_ref[...]
    lane = lax.broadcasted_iota(jnp.int32, cs.shape, 1)

    def rope(t):
        t = t * cs
        return jnp.where(lane < ROPE_DIM, t + pltpu.roll(t, ROPE_DIM, axis=1), 0.0)

    kr = rope(sm_ref[:, 0:LANES])
    kr_ref[...] = kr
    kr_b = kr.astype(BF16)
    off = n_heads * hd
    for h in range(n_heads):
        q_ref[0, h, :, 0:hd] = (q_all[:, h * hd:(h + 1) * hd] * scale).astype(BF16)
        q_ref[0, h, :, hd:2 * hd] = (rope(q_all[:, off + h * hd:off + (h + 1) * hd]) * scale).astype(BF16)
        if with_kv:
            k_ref[0, h, :, 0:hd] = k_all[:, h * hd:(h + 1) * hd].astype(BF16)
            k_ref[0, h, :, hd:2 * hd] = kr_b
            vt_ref[0, h, 0, 0:hd, :] = vt_all[h * hd:(h + 1) * hd, :].astype(BF16)
            vt_ref[0, h, 0, hd:hd + BF16_ROWS, :] = jnp.ones((BF16_ROWS, vt_all.shape[1]), BF16)


def mla_prep(u, cs, gq, gkv, wq, wk, wvt, b, s, tm, t_att, col_ql, col_kvl, col_sm, n_heads, scale):
    lora = wq.shape[0]
    nt = s // tm
    row = lambda bi, si: bi * nt + si
    hd = HEAD_DIM
    with_kv = wk is not None
    full = lambda bi, si: (0, 0)
    in_specs = [pl.BlockSpec((tm, lora), lambda bi, si: (row(bi, si), col_ql)),
                pl.BlockSpec((tm, lora), lambda bi, si: (row(bi, si), col_kvl)),
                pl.BlockSpec((tm, 512), lambda bi, si: (row(bi, si), col_sm)),
                pl.BlockSpec((tm, LANES), lambda bi, si: (row(bi, si), 0)),
                pl.BlockSpec((1, lora), full),
                pl.BlockSpec((1, lora), full),
                pl.BlockSpec(wq.shape, full)]
    out_specs = [pl.BlockSpec((tm, lora), lambda bi, si: (row(bi, si), 0)),
                 pl.BlockSpec((tm, LANES), lambda bi, si: (row(bi, si), 0)),
                 pl.BlockSpec((1, n_heads, tm, 2 * hd), lambda bi, si: (bi, 0, si, 0))]
    out_shape = [jax.ShapeDtypeStruct((b * s, lora), F32),
                 jax.ShapeDtypeStruct((b * s, LANES), F32),
                 jax.ShapeDtypeStruct((b, n_heads, s, 2 * hd), BF16)]
    args = [u, u, u, cs, gq.reshape(1, lora), gkv.reshape(1, lora), wq]
    if with_kv:
        r = t_att // tm
        in_specs += [pl.BlockSpec(wk.shape, full), pl.BlockSpec(wvt.shape, full)]
        out_specs += [pl.BlockSpec((1, n_heads, tm, 2 * hd), lambda bi, si: (bi, 0, si, 0)),
                      pl.BlockSpec((1, n_heads, 1, hd + BF16_ROWS, tm),
                                   lambda bi, si: (bi, 0, si // r, 0, si % r))]
        out_shape += [jax.ShapeDtypeStruct((b, n_heads, s, 2 * hd), BF16),
                      jax.ShapeDtypeStruct((b, n_heads, s // t_att, hd + BF16_ROWS, t_att), BF16)]
        args += [wk, wvt]
    return pl.pallas_call(
        functools.partial(_mla_prep_body, n_heads=n_heads, scale=scale, with_kv=with_kv),
        grid=(b, nt),
        in_specs=in_specs,
        out_specs=out_specs,
        out_shape=out_shape,
        compiler_params=_cparams(("parallel", "parallel")),
        name="mla_prep",
    )(*args)


def _flash_body(q_ref, k_ref, vt_ref, o_ref, m_scr, acc_scr, *, t, gh):
    qi = pl.program_id(2)
    hd = HEAD_DIM
    m_scr[...] = jnp.full(m_scr.shape, -jnp.inf, F32)
    acc_scr[...] = jnp.zeros(acc_scr.shape, F32)

    def block(j, masked):
        start = pl.multiple_of(j * t, t)
        scs = [_dot_nt(k_ref[0, g, pl.ds(start, t), :], q_ref[0, g]) for g in range(gh)]
        for g in range(gh):
            sc = scs[g]
            if masked:
                krow = lax.broadcasted_iota(jnp.int32, sc.shape, 0)
                qcol = lax.broadcasted_iota(jnp.int32, sc.shape, 1)
                sc = jnp.where(krow // CHUNK <= qcol // CHUNK, sc, -jnp.inf)
            m_prev = m_scr[g]
            m_new = jnp.maximum(m_prev, jnp.max(sc, axis=0, keepdims=True))
            alpha = jnp.exp2(m_prev - m_new)
            p = jnp.exp2(sc - m_new)
            acc_scr[g] = alpha * acc_scr[g] + _dot(vt_ref[0, g, j], p.astype(BF16))
            m_scr[g] = m_new

    def full_block(j, carry):
        block(j, False)
        return carry

    lax.fori_loop(0, qi, full_block, 0)
    block(qi, True)
    for g in range(gh):
        acc = acc_scr[g]
        o_ref[0, :, g * hd:(g + 1) * hd] = jnp.transpose(acc[:hd] / acc[hd:hd + 1])


def flash_prompt(q, k, vt, t, gh):
    b, nh, s, dk = q.shape
    hd = HEAD_DIM
    hv = vt.shape[-2]
    assert t % CHUNK == 0 and s % t == 0 and nh % gh == 0 and vt.shape[-1] == t and hv > hd
    return pl.pallas_call(
        functools.partial(_flash_body, t=t, gh=gh),
        grid=(b, nh // gh, s // t),
        in_specs=[pl.BlockSpec((1, gh, t, dk), lambda bi, hp, qi: (bi, hp, qi, 0)),
                  pl.BlockSpec((1, gh, s, dk), lambda bi, hp, qi: (bi, hp, 0, 0)),
                  pl.BlockSpec((1, gh, s // t, hv, t), lambda bi, hp, qi: (bi, hp, 0, 0, 0))],
        out_specs=pl.BlockSpec((1, t, gh * hd), lambda bi, hp, qi: (bi, qi, hp)),
        out_shape=jax.ShapeDtypeStruct((b, s, nh * hd), F32),
        scratch_shapes=[pltpu.VMEM((gh, 1, t), F32), pltpu.VMEM((gh, hv, t), F32)],
        compiler_params=_cparams(("parallel", "parallel", "arbitrary")),
        name="flash_prompt",
    )(q, k, vt)


def _bmm_body(a_ref, b_ref, o_ref):
    o_ref[0] = _dot(a_ref[0].astype(BF16), b_ref[0])


def head_matmul(a, b):
    nh, m, k = a.shape
    n = b.shape[2]
    return pl.pallas_call(
        _bmm_body,
        grid=(nh,),
        in_specs=[pl.BlockSpec((1, m, k), lambda h: (h, 0, 0)),
                  pl.BlockSpec((1, k, n), lambda h: (h, 0, 0))],
        out_specs=pl.BlockSpec((1, m, n), lambda h: (h, 0, 0)),
        out_shape=jax.ShapeDtypeStruct((nh, m, n), F32),
        compiler_params=_cparams(("parallel",)),
        name="head_matmul",
    )(a, b)


def _attn_sample_body(qa_ref, qr_ref, kp_ref, krp_ref, kn_ref, krn_ref, o_ref, *, past, s_new, n_heads):
    qa = qa_ref[0].astype(BF16)
    qr = qr_ref[0].astype(BF16)
    kp = kp_ref[0].astype(BF16)
    krp_t = krp_ref[0].astype(BF16)
    kn = kn_ref[0].astype(BF16)
    krn = krn_ref[0].astype(BF16)
    s_past = _dot_nt(qa, kp) + _dot(qr[:, :ROPE_DIM], krp_t)
    s_n = _dot_nt(qa, kn) + _dot_nt(qr, krn)
    row = lax.broadcasted_iota(jnp.int32, s_n.shape, 0)
    col = lax.broadcasted_iota(jnp.int32, s_n.shape, 1)
    qpos = past + row // n_heads
    kpos = past + col
    valid = (col < s_new) & (kpos // CHUNK <= qpos // CHUNK)
    s_n = jnp.where(valid, s_n, -jnp.inf)
    m = jnp.maximum(jnp.max(s_past, axis=-1, keepdims=True), jnp.max(s_n, axis=-1, keepdims=True))
    pp = jnp.exp2(s_past - m)
    pn = jnp.exp2(s_n - m)
    l = jnp.sum(pp, axis=-1, keepdims=True) + jnp.sum(pn, axis=-1, keepdims=True)
    o_ref[0] = (_dot(pp.astype(BF16), kp) + _dot(pn.astype(BF16), kn)) / l


def attn_sample(qa, qr, ckv_past, kr_past, ckv_new, kr_new, s_new, n_heads):
    b, r, lora = qa.shape
    past = ckv_past.shape[1]
    return pl.pallas_call(
        functools.partial(_attn_sample_body, past=past, s_new=s_new, n_heads=n_heads),
        grid=(b,),
        in_specs=[pl.BlockSpec((1, r, lora), lambda i: (i, 0, 0)),
                  pl.BlockSpec((1, r, LANES), lambda i: (i, 0, 0)),
                  pl.BlockSpec((1, past, lora), lambda i: (i, 0, 0)),
                  pl.BlockSpec((1, ROPE_DIM, past), lambda i: (i, 0, 0)),
                  pl.BlockSpec((1, LANES, lora), lambda i: (i, 0, 0)),
                  pl.BlockSpec((1, LANES, LANES), lambda i: (i, 0, 0))],
        out_specs=pl.BlockSpec((1, r, lora), lambda i: (i, 0, 0)),
        out_shape=jax.ShapeDtypeStruct((b, r, lora), F32),
        compiler_params=_cparams(("parallel",)),
        name="attn_sample",
    )(qa, qr, ckv_past, kr_past, ckv_new, kr_new)


def _softplus(x):
    return jnp.maximum(x, 0.0) + jnp.log1p(jnp.exp(-jnp.abs(x)))


def _gdn_body(qkv_ref, z_ref, sm_ref, ga_ref, gb_ref, oa_ref, prev0_ref, s0_ref, convw_ref, alog_ref, dtb_ref,
              gn_ref, m_ref, sout_ref, s_scr, prev_scr, y_scr, *, L, n_heads, chunks):
    c = pl.program_id(1)

    @pl.when(c == 0)
    def _():
        s_scr[...] = s0_ref[0]
        prev_scr[...] = prev0_ref[0]

    for cc in range(chunks):
        rows = pl.ds(cc * L, L)
        _gdn_chunk(qkv_ref.at[rows], z_ref.at[rows], sm_ref.at[rows], ga_ref.at[rows], gb_ref.at[rows],
                   oa_ref.at[rows], convw_ref, alog_ref, dtb_ref, gn_ref, m_ref.at[rows], s_scr, prev_scr, y_scr,
                   L=L, n_heads=n_heads)

    @pl.when(c == pl.num_programs(1) - 1)
    def _():
        sout_ref[0] = s_scr[...]


def _gdn_chunk(qkv_ref, z_ref, sm_ref, ga_ref, gb_ref, oa_ref, convw_ref, alog_ref, dtb_ref, gn_ref, m_ref,
               s_scr, prev_scr, y_scr, *, L, n_heads):
    hd = HEAD_DIM
    heads = range(n_heads)

    x = qkv_ref[...]
    y_scr[...] = _conv_silu(x, prev_scr[...], convw_ref[...])
    prev_scr[...] = x[L - SUBLANES:L]

    a = sm_ref[:, LANES:2 * LANES]
    bb = sm_ref[:, 2 * LANES:3 * LANES]
    g = -jnp.exp(alog_ref[...]) * _softplus(a + dtb_ref[...])
    beta = _sigmoid(bb)
    rowl = lax.broadcasted_iota(jnp.int32, g.shape, 0)
    cum = g
    d = 1
    while d < L:
        cum = cum + jnp.where(rowl >= d, pltpu.roll(cum, d, axis=0), 0.0)
        d *= 2
    cum_t = jnp.transpose(jnp.concatenate([cum, jnp.zeros((LANES - L, LANES), F32)], axis=0))

    row = lax.broadcasted_iota(jnp.int32, (L, L), 0)
    col = lax.broadcasted_iota(jnp.int32, (L, L), 1)
    tri = row >= col
    strict = row > col
    eye = jnp.where(row == col, 1.0, 0.0)
    gn = gn_ref[...]

    q_b, k_b, kb_b, rhs_b, qg_b, kd_b = [], [], [], [], [], []
    for h in heads:
        qh = y_scr[:, h * hd:(h + 1) * hd]
        kh = y_scr[:, (n_heads + h) * hd:(n_heads + h + 1) * hd]
        vh = y_scr[:, (2 * n_heads + h) * hd:(2 * n_heads + h + 1) * hd]
        qh = (qh * lax.rsqrt(jnp.sum(qh * qh, axis=-1, keepdims=True) + EPS)) * (hd ** -0.5)
        kh = kh * lax.rsqrt(jnp.sum(kh * kh, axis=-1, keepdims=True) + EPS)
        gcol = cum[:, h:h + 1]
        bcol = beta[:, h:h + 1]
        e_g = jnp.exp(gcol)
        kb = kh * bcol
        q_b.append(qh.astype(BF16))
        k_b.append(kh.astype(BF16))
        kb_b.append(kb.astype(BF16))
        rhs_b.append(jnp.concatenate([vh * bcol, kb * e_g], axis=1).astype(BF16))
        qg_b.append((qh * e_g).astype(BF16))
        kd_b.append((kh * jnp.exp(cum[L - 1:L, h:h + 1] - gcol)).astype(BF16))

    kk = [_dot_nt(kb_b[h], k_b[h]) for h in heads]
    qk = [_dot_nt(q_b[h], k_b[h]) for h in heads]

    nmat, attn_b = [], []
    for h in heads:
        diff = cum[:, h:h + 1] - cum_t[h:h + 1, 0:L]
        decay = jnp.where(tri, jnp.exp(jnp.where(tri, diff, 0.0)), 0.0)
        nmat.append(jnp.where(strict, kk[h] * decay, 0.0))
        attn_b.append(jnp.where(tri, qk[h] * decay, 0.0).astype(BF16))

    tmat = [eye - nmat[h] for h in heads]
    p_b = [nmat[h].astype(BF16) for h in heads]
    pmat = [_dot(p_b[h], p_b[h]) for h in heads]
    power = 2
    while True:
        p_b = [pmat[h].astype(BF16) for h in heads]
        tmat = [tmat[h] + _dot(tmat[h].astype(BF16), p_b[h]) for h in heads]
        power *= 2
        if power >= L:
            break
        pmat = [_dot(p_b[h], p_b[h]) for h in heads]

    uw = [_dot(tmat[h].astype(BF16), rhs_b[h]) for h in heads]
    s_old = [s_scr[h] for h in heads]
    s_b = [s_old[h].astype(BF16) for h in heads]
    ws = [_dot(jnp.concatenate([uw[h][:, hd:].astype(BF16), qg_b[h]], axis=0), s_b[h]) for h in heads]
    vn_b = [(uw[h][:, :hd] - ws[h][:L]).astype(BF16) for h in heads]
    av = [_dot(attn_b[h], vn_b[h]) for h in heads]
    ks = [_dot_tn(kd_b[h], vn_b[h]) for h in heads]
    for h in heads:
        sl = slice(h * hd, (h + 1) * hd)
        o = ws[h][L:] + av[h]
        s_scr[h] = s_old[h] * jnp.exp(cum[L - 1:L, h:h + 1]) + ks[h]
        zh = z_ref[:, sl]
        on = (o * lax.rsqrt(jnp.mean(o * o, axis=-1, keepdims=True) + EPS)) * gn
        ob = on * (zh * _sigmoid(zh))
        m_ref[:, sl] = (_sigmoid(ga_ref[:, sl]) * oa_ref[:, sl] + _sigmoid(gb_ref[:, sl]) * ob).astype(BF16)


def gdn(u, o_a, prev0, s0, conv_w, a_log, dt_bias, gn, b, s, col_z, col_sm, col_ga, col_gb, n_heads):
    hd = HEAD_DIM
    L = min(s, CHUNK)
    chunks = GDN_CHUNKS_PER_STEP if (s // L) % GDN_CHUNKS_PER_STEP == 0 else 1
    nc = s // (L * chunks)
    rows = L * chunks
    qkv_w = 3 * n_heads * hd
    row = lambda bi, c: bi * nc + c
    alog = jnp.zeros((1, LANES), F32).at[0, :n_heads].set(a_log)
    dtb = jnp.zeros((1, LANES), F32).at[0, :n_heads].set(dt_bias)
    return pl.pallas_call(
        functools.partial(_gdn_body, L=L, n_heads=n_heads, chunks=chunks),
        grid=(b, nc),
        in_specs=[pl.BlockSpec((rows, qkv_w), lambda bi, c: (row(bi, c), 0)),
                  pl.BlockSpec((rows, n_heads * hd), lambda bi, c: (row(bi, c), col_z)),
                  pl.BlockSpec((rows, 512), lambda bi, c: (row(bi, c), col_sm)),
                  pl.BlockSpec((rows, n_heads * hd), lambda bi, c: (row(bi, c), col_ga)),
                  pl.BlockSpec((rows, n_heads * hd), lambda bi, c: (row(bi, c), col_gb)),
                  pl.BlockSpec((rows, n_heads * hd), lambda bi, c: (row(bi, c), 0)),
                  pl.BlockSpec((1, SUBLANES, qkv_w), lambda bi, c: (bi, 0, 0)),
                  pl.BlockSpec((1, n_heads, hd, hd), lambda bi, c: (bi, 0, 0, 0)),
                  pl.BlockSpec((CONV_W, qkv_w), lambda bi, c: (0, 0)),
                  pl.BlockSpec((1, LANES), lambda bi, c: (0, 0)),
                  pl.BlockSpec((1, LANES), lambda bi, c: (0, 0)),
                  pl.BlockSpec((1, hd), lambda bi, c: (0, 0))],
        out_specs=[pl.BlockSpec((rows, n_heads * hd), lambda bi, c: (row(bi, c), 0)),
                   pl.BlockSpec((1, n_heads, hd, hd), lambda bi, c: (bi, 0, 0, 0))],
        out_shape=[jax.ShapeDtypeStruct((b * s, n_heads * hd), BF16),
                   jax.ShapeDtypeStruct((b, n_heads, hd, hd), F32)],
        scratch_shapes=[pltpu.VMEM((n_heads, hd, hd), F32),
                        pltpu.VMEM((SUBLANES, qkv_w), F32),
                        pltpu.VMEM((L, qkv_w), F32)],
        compiler_params=_cparams(("parallel", "arbitrary")),
        name="gdn",
    )(u, u, u, u, u, o_a, prev0, s0, conv_w, alog, dtb, gn.reshape(1, hd))


def _out_route_body(ma_ref, mb_ref, xa_ref, xb_ref, wo_ref, g_ref, whi_ref, wlo_ref, bias_ref,
                    x1_ref, h_ref, eid_ref, wgt_ref, cnt_ref, *, tiles_a):
    is_a = pl.program_id(0) < tiles_a
    m = jnp.where(is_a, ma_ref[...], mb_ref[...])
    x = jnp.where(is_a, xa_ref[...], xb_ref[...]) + _dot(m, wo_ref[...])
    x1_ref[...] = x
    hn = (x * lax.rsqrt(jnp.mean(x * x, axis=-1, keepdims=True) + EPS)) * g_ref[...]
    h_hi, h_lo = _split_bf16(hn)
    h_ref[...] = h_hi
    lg = jnp.transpose(_dot3(h_hi, h_lo, whi_ref[...], wlo_ref[...]) + bias_ref[...])
    rowi = lax.broadcasted_iota(jnp.int32, lg.shape, 0)
    big = jnp.int32(LANES)
    neg = -jnp.inf
    glog = jnp.where(rowi < N_GROUPS, lg, neg)
    gmax = jnp.max(glog, axis=0, keepdims=True)
    pg_top = 1.0 / jnp.sum(jnp.exp(glog - gmax), axis=0, keepdims=True)
    grp = jnp.min(jnp.where(glog == gmax, rowi, big), axis=0, keepdims=True)
    lo = N_GROUPS + grp * EXPERTS_PER_GROUP
    le = jnp.where((rowi >= lo) & (rowi < lo + EXPERTS_PER_GROUP), lg, neg)
    v1 = jnp.max(le, axis=0, keepdims=True)
    i1 = jnp.min(jnp.where(le == v1, rowi, big), axis=0, keepdims=True)
    le2 = jnp.where(rowi == i1, neg, le)
    v2 = jnp.max(le2, axis=0, keepdims=True)
    i2 = jnp.min(jnp.where(le2 == v2, rowi, big), axis=0, keepdims=True)
    e2 = jnp.exp(v2 - v1)
    den = 1.0 + e2
    w1 = pg_top * (1.0 / den)
    w2 = pg_top * (e2 / den)
    row8 = lax.broadcasted_iota(jnp.int32, eid_ref.shape, 0)
    eid_ref[...] = jnp.where(row8 == 0, i1 - N_GROUPS, jnp.where(row8 == 1, i2 - N_GROUPS, 0))
    wgt_ref[...] = jnp.where(row8 == 0, w1, jnp.where(row8 == 1, w2, 0.0))
    hit = jnp.where((rowi == i1 - N_GROUPS) | (rowi == i2 - N_GROUPS), 1.0, 0.0)
    per_expert = jnp.broadcast_to(jnp.sum(hit, axis=1, keepdims=True), (LANES, LANES))
    cnt_ref[0] = jnp.transpose(per_expert)[:SUBLANES].astype(jnp.int32)


def out_proj_route(ma, mb, xa, xb, w_out, g, whi, wlo, bias, tm):
    d = xa.shape[1]
    ta, tb = xa.shape[0] // tm, xb.shape[0] // tm
    n = (ta + tb) * tm
    rows_a = lambda i: (jnp.minimum(i, ta - 1), 0)
    rows_b = lambda i: (jnp.maximum(i - ta, 0), 0)
    fixed = lambda i: (0, 0)
    return pl.pallas_call(
        functools.partial(_out_route_body, tiles_a=ta),
        grid=(ta + tb,),
        in_specs=[pl.BlockSpec((tm, d), rows_a),
                  pl.BlockSpec((tm, d), rows_b),
                  pl.BlockSpec((tm, d), rows_a),
                  pl.BlockSpec((tm, d), rows_b),
                  pl.BlockSpec((d, d), fixed),
                  pl.BlockSpec((1, d), fixed),
                  pl.BlockSpec((d, LANES), fixed),
                  pl.BlockSpec((d, LANES), fixed),
                  pl.BlockSpec((1, LANES), fixed)],
        out_specs=[pl.BlockSpec((tm, d), lambda i: (i, 0)),
                   pl.BlockSpec((tm, d), lambda i: (i, 0)),
                   pl.BlockSpec((SUBLANES, tm), lambda i: (0, i)),
                   pl.BlockSpec((SUBLANES, tm), lambda i: (0, i)),
                   pl.BlockSpec((1, SUBLANES, LANES), lambda i: (i, 0, 0))],
        out_shape=[jax.ShapeDtypeStruct((n, d), F32),
                   jax.ShapeDtypeStruct((n, d), BF16),
                   jax.ShapeDtypeStruct((SUBLANES, n), jnp.int32),
                   jax.ShapeDtypeStruct((SUBLANES, n), F32),
                   jax.ShapeDtypeStruct((n // tm, SUBLANES, LANES), jnp.int32)],
        compiler_params=_cparams(("parallel",)),
        name="out_proj_route",
    )(ma, mb, xa, xb, w_out, g.reshape(1, d), whi, wlo, bias)


def _expert_body(be_tab, first_tab, slot_tab, next_tab, nused, x_ref, sw_ref, wg_hbm, wu_hbm, wd_hbm, o_ref,
                 stg_g, stg_u, stg_d, wg_s, wu_s, wd_s, sem):
    blk = pl.program_id(0)

    def weight_copies(e, slot):
        return (pltpu.make_async_copy(wg_hbm.at[e], stg_g.at[slot], sem.at[slot, 0]),
                pltpu.make_async_copy(wu_hbm.at[e], stg_u.at[slot], sem.at[slot, 1]),
                pltpu.make_async_copy(wd_hbm.at[e], stg_d.at[slot], sem.at[slot, 2]))

    @pl.when((blk == 0) & (nused[0] > 0))
    def _():
        for cp in weight_copies(be_tab[0], 0):
            cp.start()

    @pl.when(blk < nused[0])
    def _():
        @pl.when(first_tab[blk] == 1)
        def _():
            slot = slot_tab[blk]

            @pl.when(next_tab[blk] >= 0)
            def _():
                for cp in weight_copies(next_tab[blk], 1 - slot):
                    cp.start()

            for cp in weight_copies(be_tab[blk], slot):
                cp.wait()
            wg_s[...] = stg_g[slot].astype(BF16)
            wu_s[...] = stg_u[slot].astype(BF16)
            wd_s[...] = stg_d[slot].astype(BF16)

        x = x_ref[...]
        gate = _dot(x, wg_s[...])
        up = _dot(x, wu_s[...])
        hid = (gate * _sigmoid(gate)) * up
        w_col = jnp.transpose(jnp.broadcast_to(sw_ref[0], (LANES, x.shape[0])))[:, 0:1]
        o_ref[...] = (_dot(hid.astype(BF16), wd_s[...]) * w_col).astype(o_ref.dtype)

    @pl.when(blk >= nused[0])
    def _():
        o_ref[...] = jnp.zeros(o_ref.shape, o_ref.dtype)


def experts(x_sorted, slot_w, block_e, first, slot_tab, next_tab, nused, w_gate, w_up, w_down, bm):
    n_slots, d = x_sorted.shape
    de = w_gate.shape[2]
    n_blocks = n_slots // bm
    row = lambda i, *_: (i, 0)
    grid_spec = pltpu.PrefetchScalarGridSpec(
        num_scalar_prefetch=5,
        grid=(n_blocks,),
        in_specs=[pl.BlockSpec((bm, d), row),
                  pl.BlockSpec((1, 1, bm), lambda i, *_: (i, 0, 0)),
                  pl.BlockSpec(memory_space=pl.ANY),
                  pl.BlockSpec(memory_space=pl.ANY),
                  pl.BlockSpec(memory_space=pl.ANY)],
        out_specs=pl.BlockSpec((bm, d), row),
        scratch_shapes=[pltpu.VMEM((2, d, de), F32), pltpu.VMEM((2, d, de), F32), pltpu.VMEM((2, de, d), F32),
                        pltpu.VMEM((d, de), BF16), pltpu.VMEM((d, de), BF16), pltpu.VMEM((de, d), BF16),
                        pltpu.SemaphoreType.DMA((2, 3))],
    )
    return pl.pallas_call(
        _expert_body,
        grid_spec=grid_spec,
        out_shape=jax.ShapeDtypeStruct((n_slots, d), BF16),
        compiler_params=_cparams(("arbitrary",)),
        name="experts",
    )(block_e, first, slot_tab, next_tab, nused, x_sorted, slot_w, w_gate, w_up, w_down)


def _final_body(x_ref, ma_ref, mb_ref, g_ref, o_ref):
    x = x_ref[...] + (ma_ref[...].astype(F32) + mb_ref[...].astype(F32))
    o_ref[...] = (x * lax.rsqrt(jnp.mean(x * x, axis=-1, keepdims=True) + EPS)) * g_ref[...]


def final_norm(x, ma, mb, g, tm, n, row0):
    d = x.shape[1]
    assert row0 % tm == 0 and n % tm == 0
    spec = pl.BlockSpec((tm, d), lambda i: (i, 0))
    mspec = pl.BlockSpec((tm, d), lambda i: (i + row0 // tm, 0))
    return pl.pallas_call(
        _final_body,
        grid=(n // tm,),
        in_specs=[mspec, mspec, mspec, pl.BlockSpec((1, d), lambda i: (0, 0))],
        out_specs=spec,
        out_shape=jax.ShapeDtypeStruct((n, d), F32),
        compiler_params=_cparams(("parallel",)),
        name="final_norm",
    )(x, ma, mb, g.reshape(1, d))


def _tile(n, pref):
    t = min(n, pref)
    while n % t:
        t //= 2
    return t


def _rope_table(pos, b):
    half = ROPE_DIM // 2
    inv = ROPE_BASE ** (-jnp.arange(half, dtype=F32) / half)
    ang = pos.astype(F32)[:, None] * inv[None, :]
    cos, sin = jnp.cos(ang), jnp.sin(ang)
    cs = jnp.concatenate([cos, cos, sin, sin], axis=-1)
    return jnp.tile(cs, (b, 1))


def _rot_cols(w):
    half = ROPE_DIM // 2
    return jnp.concatenate([-w[..., half:], w[..., :half]], axis=-1)


def _moe(h_bf, eid, wgt, counts, w_gate, w_up, w_down, bm):
    n, d = h_bf.shape
    n_exp = w_gate.shape[0]
    a = n * TOP_K
    e_flat = eid.reshape(-1)
    w_flat = wgt.reshape(-1)
    padded = (counts + bm - 1) // bm * bm
    pend = jnp.cumsum(padded)
    n_blocks = (a + n_exp * (bm - 1) + bm - 1) // bm
    n_slots = n_blocks * bm
    block_start = jnp.arange(n_blocks, dtype=jnp.int32) * bm
    block_e = jnp.minimum(jnp.sum(pend[:, None] <= block_start[None, :], axis=0, dtype=jnp.int32), n_exp - 1)
    first = jnp.concatenate([jnp.ones((1,), jnp.int32), (block_e[1:] != block_e[:-1]).astype(jnp.int32)])
    nused = (pend[-1] // bm).astype(jnp.int32).reshape(1)
    slot_tab = (jnp.cumsum(first) - 1) % 2
    e_ids = jnp.arange(n_exp, dtype=jnp.int32)
    later = lax.cummin(jnp.where(counts > 0, e_ids, n_exp), axis=0, reverse=True)
    next_e = jnp.concatenate([later[1:], jnp.full((1,), n_exp, jnp.int32)])
    next_tab = jnp.where(next_e < n_exp, next_e, -1)[block_e].astype(jnp.int32)
    n_pad = n_slots - a
    pad_end = jnp.cumsum(padded - counts)
    pad_key = jnp.sum(jnp.arange(n_pad, dtype=jnp.int32)[None, :] >= pad_end[:, None], axis=0, dtype=jnp.int32)
    keys = jnp.concatenate([e_flat, pad_key])
    toks = jnp.concatenate([jnp.arange(a, dtype=jnp.int32) // TOP_K, jnp.arange(n_pad, dtype=jnp.int32) % n])
    wts = jnp.concatenate([w_flat, jnp.zeros((n_pad,), F32)])
    iota = jnp.arange(n_slots, dtype=jnp.int32)
    _, slot_tok, slot_w, slot_src = lax.sort((keys, toks, wts, iota), num_keys=1, is_stable=True)
    x_sorted = h_bf[slot_tok]
    out = experts(x_sorted, slot_w.reshape(n_blocks, 1, bm), block_e, first, slot_tab.astype(jnp.int32), next_tab, nused,
                  w_gate, w_up, w_down, bm)
    _, slot_of = lax.sort((slot_src, iota), num_keys=1)
    slot_of = slot_of[:a].reshape(n, TOP_K)
    return out[slot_of[:, 0]], out[slot_of[:, 1]]


def _mixer(x3, pos, ckv_past, kr_past, s0, conv0, wp):
    b, s, d = x3.shape
    n = b * s
    n_heads = wp['n_heads']
    hd = HEAD_DIM
    x = x3.reshape(n, d)
    cols = wp['cols']

    u = rms_matmul(x, wp['attn_norm_g'], wp['w_all'], _tile(n, IN_PROJ_TM), IN_PROJ_TN)

    cs = _rope_table(pos, b)
    scale = (hd + ROPE_DIM) ** -0.5 * float(np.log2(np.e))
    prompt = ckv_past is None
    t_att = _tile(s, ATTN_TILE)
    tm = min(_tile(s, 256), t_att)
    res = mla_prep(u, cs, wp['q_norm_g'], wp['kv_norm_g'], wp['wq_ext'], wp['wk'] if prompt else None,
                   wp['wvt'] if prompt else None, b, s, tm, t_att, cols['q_lat'] // 512, cols['kv_lat'] // 512,
                   cols['small'] // 512, n_heads, scale)
    ckv, kr_pad, q = res[:3]
    kr = kr_pad[:, :ROPE_DIM]
    lora = ckv.shape[1]
    if prompt:
        o_a = flash_prompt(q, res[3], res[4], t_att, ATTN_HEADS_PER_STEP).reshape(n, n_heads * hd)
    else:
        qn = q[..., :hd].transpose(1, 0, 2, 3).reshape(n_heads, n, hd)
        q_abs = head_matmul(qn, wp['w_uk_t'])
        q_abs = q_abs.reshape(n_heads, b, s, lora).transpose(1, 2, 0, 3).reshape(b, s * n_heads, lora)
        qr = q[..., hd:].transpose(0, 2, 1, 3).reshape(b, s * n_heads, hd)
        kr_past_t = jnp.swapaxes(kr_past, 1, 2)
        ckv_new = jnp.pad(ckv.reshape(b, s, lora), ((0, 0), (0, LANES - s), (0, 0)))
        kr_new = jnp.pad(kr_pad.reshape(b, s, LANES), ((0, 0), (0, LANES - s), (0, 0)))
        o_lat = attn_sample(q_abs, qr, ckv_past, kr_past_t, ckv_new, kr_new, s, n_heads)
        o_lat = o_lat.reshape(b, s, n_heads, lora).transpose(2, 0, 1, 3).reshape(n_heads, n, lora)
        o_h = head_matmul(o_lat, wp['w_uv_h'])
        o_a = o_h.reshape(n_heads, n, hd).transpose(1, 0, 2).reshape(n, n_heads * hd)

    qkv_w = 3 * n_heads * hd
    prev0 = jnp.pad(conv0, ((0, 0), (SUBLANES - (CONV_W - 1), 0), (0, 0)))
    merged, s_new = gdn(u, o_a, prev0, s0, wp['conv_w'], wp['a_log'], wp['dt_bias'], wp['gdn_norm_g'], b, s,
                        cols['z'] // d, cols['small'] // 512, cols['gate_a'] // d, cols['gate_b'] // d, n_heads)
    tail = u.reshape(b, s, -1)[:, max(s - (CONV_W - 1), 0):, :qkv_w]
    conv_new = jnp.concatenate([conv0, tail], axis=1)[:, -(CONV_W - 1):]

    return (merged, x), ckv.reshape(b, s, lora), kr.reshape(b, s, ROPE_DIM), s_new, conv_new


def _prep_weights(l, attn_norm_g, w_in, q_norm_g, w_uq, kv_norm_g, w_uk, w_uv, conv_w, a_log, dt_bias, gdn_norm_g,
                  w_out, ffn_norm_g, w_group, b_group, w_router, b_router, w_gate, w_up, w_down):
    d = w_in.shape[1]
    q_lora, n_heads, _ = w_uq.shape[1:]
    kv_lora = w_uk.shape[1]
    hd = HEAD_DIM
    qkv_w = 3 * n_heads * hd
    sizes = (q_lora, kv_lora, ROPE_DIM, qkv_w, n_heads, n_heads, n_heads * hd, d, d)
    offs = np.concatenate([[0], np.cumsum(sizes)])
    wi_t = jnp.swapaxes(w_in[l], 0, 1)
    part = lambda i: wi_t[offs[i]:offs[i + 1]]
    w_qlat, w_kvlat, w_kr, w_qkv, w_a, w_b, w_z, w_ga, w_gb = [part(i) for i in range(9)]
    zpad = lambda c: jnp.zeros((c, d), F32)
    half = ROPE_DIM // 2
    w_kr_rot = jnp.concatenate([-w_kr[half:], w_kr[:half]], axis=0)
    small = jnp.concatenate([w_kr, w_kr_rot, w_a, zpad(LANES - n_heads), w_b, zpad(LANES - n_heads),
                             zpad(LANES)], axis=0)
    w_all = jnp.concatenate([w_qkv, w_z, w_ga, w_gb, w_qlat, w_kvlat, small], axis=0).astype(BF16)
    cols = {'qkv': 0, 'z': qkv_w, 'gate_a': qkv_w + n_heads * hd, 'gate_b': qkv_w + n_heads * hd + d,
            'q_lat': qkv_w + n_heads * hd + 2 * d}
    cols['kv_lat'] = cols['q_lat'] + q_lora
    cols['small'] = cols['kv_lat'] + kv_lora
    wq = w_uq[l]
    wq_nope = wq[:, :, :hd].reshape(q_lora, n_heads * hd)
    wq_r = wq[:, :, hd:]
    wq_rope = jnp.concatenate([wq_r, _rot_cols(wq_r)], axis=-1).reshape(q_lora, n_heads * hd)
    wq_ext = jnp.concatenate([wq_nope, wq_rope], axis=1).astype(BF16)
    wk = w_uk[l].reshape(kv_lora, n_heads * hd).astype(BF16)
    wvt = w_uv[l].reshape(kv_lora, n_heads * hd).T.astype(BF16)
    wr = jnp.concatenate([w_group[l], w_router[l].transpose(1, 0, 2).reshape(d, -1)], axis=1)
    wr = jnp.pad(wr, ((0, 0), (0, LANES - wr.shape[1])))
    wr_hi = wr.astype(BF16)
    wr_lo = (wr - wr_hi.astype(F32)).astype(BF16)
    br = jnp.concatenate([b_group[l], b_router[l].reshape(-1)])
    br = jnp.pad(br, (0, LANES - br.shape[0])).reshape(1, LANES)
    return {
        'n_heads': n_heads, 'cols': cols, 'w_all': w_all, 'attn_norm_g': attn_norm_g[l],
        'q_norm_g': q_norm_g[l], 'kv_norm_g': kv_norm_g[l], 'wq_ext': wq_ext, 'wk': wk, 'wvt': wvt,
        'w_uk_t': w_uk[l].transpose(1, 2, 0).astype(BF16),
        'w_uv_h': w_uv[l].transpose(1, 0, 2).astype(BF16),
        'conv_w': conv_w[l], 'a_log': a_log[l], 'dt_bias': dt_bias[l], 'gdn_norm_g': gdn_norm_g[l],
        'w_out': w_out[l].astype(BF16), 'ffn_norm_g': ffn_norm_g[l], 'wr_hi': wr_hi, 'wr_lo': wr_lo, 'br': br,
        'w_gate': w_gate[l], 'w_up': w_up[l], 'w_down': w_down[l],
    }


def kernel(x_prompt, x_sample, cache_ckv, cache_k_rope, state_gdn, state_conv, attn_norm_g, w_in, q_norm_g, w_uq, kv_norm_g, w_uk, w_uv, conv_w, a_log, dt_bias, gdn_norm_g, w_out, ffn_norm_g, w_group, b_group, w_router, b_router, w_gate, w_up, w_down, final_norm_g):
    depth = w_in.shape[0]
    assert depth == 1, "final RMSNorm is fused into the last layer; deeper trunks are not supported"
    b_p, s_p, d = x_prompt.shape
    b_s, s_s, _ = x_sample.shape
    past = cache_ckv.shape[2]
    n_heads = w_uq.shape[2]
    qkv_w = 3 * n_heads * HEAD_DIM
    pos_p = jnp.arange(s_p, dtype=jnp.int32)
    pos_s = past + jnp.arange(s_s, dtype=jnp.int32)
    wp = _prep_weights(0, attn_norm_g, w_in, q_norm_g, w_uq, kv_norm_g, w_uk, w_uv, conv_w, a_log, dt_bias,
                       gdn_norm_g, w_out, ffn_norm_g, w_group, b_group, w_router, b_router, w_gate, w_up, w_down)
    s0 = jnp.zeros((b_p, n_heads, HEAD_DIM, HEAD_DIM), F32)
    c0 = jnp.zeros((b_p, CONV_W - 1, qkv_w), F32)
    (m_p, xr_p), ckv_p, kr_p, sg_p, sc_p = _mixer(x_prompt, pos_p, None, None, s0, c0, wp)
    (m_s, xr_s), ckv_s, kr_s, sg_s, sc_s = _mixer(x_sample, pos_s, cache_ckv[0], cache_k_rope[0], state_gdn[0],
                                                  state_conv[0], wp)

    n_p, n_s = b_p * s_p, b_s * s_s
    tm_s = _tile(int(np.gcd(n_p, n_s)), 512)
    x1, h_bf, eid, wgt, cnt = out_proj_route(m_p, m_s, xr_p, xr_s, wp['w_out'], wp['ffn_norm_g'], wp['wr_hi'],
                                             wp['wr_lo'], wp['br'], tm_s)
    n_exp = wp['w_gate'].shape[0]
    counts = cnt[:, 0, :n_exp].sum(axis=0)
    ma, mb = _moe(h_bf, eid[:TOP_K].T, wgt[:TOP_K].T, counts, wp['w_gate'], wp['w_up'], wp['w_down'], MOE_BLOCK)
    yp = final_norm(x1, ma, mb, final_norm_g, _tile(n_p, 512), n_p, 0).reshape(b_p, s_p, d)
    ys = final_norm(x1, ma, mb, final_norm_g, tm_s, n_s, n_p).reshape(b_s, s_s, d)
    return (yp, ys, ckv_p[None], kr_p[None], sg_p[None], sc_p[None], ckv_s[None], kr_s[None], sg_s[None], sc_s[None])
```
